```python
import math
import jax, jax.numpy as jnp
from jax import lax
import numpy as np

D_MODEL = 1024
BATCH = 4
SEQ = 4096
DEPTH = 1

MEM_LEN = 256
HEAD_DIM = 64
N_Q_HEADS = 8
N_KV_HEADS = 2
GROUP = N_Q_HEADS // N_KV_HEADS
WINDOW = 128
BLOCK = 128
ATTN_W = N_Q_HEADS * HEAD_DIM
KV_W = N_KV_HEADS * HEAD_DIM
CONV_W = D_MODEL // 2
CONV_K = 31
N_CROSS_HEADS = 4
CROSS_HEAD_DIM = 128
CROSS_W = N_CROSS_HEADS * CROSS_HEAD_DIM
N_BRANCH = 3
RMS_EPS = 1e-6
LN_EPS = 1e-5
NEG_BIG = -1e30

SPLIT_SIZES = [ATTN_W, KV_W, KV_W, ATTN_W, 2 * CONV_W, CONV_W, CROSS_W, CROSS_W, N_BRANCH * D_MODEL]
IN_W = sum(SPLIT_SIZES)
SPLIT_POINTS = [int(v) for v in np.cumsum(SPLIT_SIZES)[:-1]]

ALIBI_SLOPES = np.array([2.0 ** (-8.0 * (i + 1) / N_Q_HEADS) for i in range(N_Q_HEADS)], dtype=np.float32)

kernel_name = "hybrid_swa_conformer_conv_memxattn_gated"


def rmsnorm(x, g):
    xf = x.astype(jnp.float32)
    y = xf * lax.rsqrt(jnp.mean(xf * xf, axis=-1, keepdims=True) + RMS_EPS)
    return (y * g.astype(jnp.float32)).astype(x.dtype)


def layernorm(x, g, b):
    xf = x.astype(jnp.float32)
    mu = jnp.mean(xf, axis=-1, keepdims=True)
    var = jnp.mean(jnp.square(xf - mu), axis=-1, keepdims=True)
    y = (xf - mu) * lax.rsqrt(var + LN_EPS)
    return (y * g.astype(jnp.float32) + b.astype(jnp.float32)).astype(x.dtype)


def _band(t, nb):
    B = t.shape[0]
    pad = jnp.zeros((B, BLOCK) + t.shape[2:], t.dtype)
    tp = jnp.concatenate([pad, t], axis=1).reshape((B, nb + 1, BLOCK) + t.shape[2:])
    return jnp.concatenate([tp[:, :-1], tp[:, 1:]], axis=2)


def sliding_window_attention(q, k, v, pos, sinks):
    B, S = q.shape[0], q.shape[1]
    nb = S // BLOCK
    qb = q.reshape(B, nb, BLOCK, N_KV_HEADS, GROUP, HEAD_DIM)
    kw = _band(k, nb)
    vw = _band(v, nb)
    pad_pos = jnp.broadcast_to(pos[:, :1] - (WINDOW + 1), (B, BLOCK))
    pos_p = jnp.concatenate([pad_pos, pos], axis=1).reshape(B, nb + 1, BLOCK)
    kpos = jnp.concatenate([pos_p[:, :-1], pos_p[:, 1:]], axis=2)
    qpos = pos.reshape(B, nb, BLOCK)
    delta = qpos[:, :, :, None] - kpos[:, :, None, :]
    allowed = (delta >= 0) & (delta < WINDOW)
    scale = 1.0 / math.sqrt(HEAD_DIM)
    s = jnp.einsum('bnqhgd,bnkhd->bnhgqk', qb, kw).astype(jnp.float32) * scale
    slopes = jnp.asarray(ALIBI_SLOPES).reshape(N_KV_HEADS, GROUP)
    s = s - slopes[None, None, :, :, None, None] * delta[:, :, None, None, :, :].astype(jnp.float32)
    s = jnp.where(allowed[:, :, None, None, :, :], s, NEG_BIG)
    sink = jnp.broadcast_to(sinks.astype(jnp.float32).reshape(N_KV_HEADS, GROUP)[None, None, :, :, None, None],
                            s.shape[:-1] + (1,))
    p = jax.nn.softmax(jnp.concatenate([s, sink], axis=-1), axis=-1)[..., :-1]
    o = jnp.einsum('bnhgqk,bnkhd->bnqhgd', p.astype(v.dtype), vw)
    return o.reshape(B, S, ATTN_W)


def conformer_conv(u, b_glu, w_dw, b_dw, ln_g, ln_b):
    u = u + b_glu
    a = u[..., :CONV_W] * jax.nn.sigmoid(u[..., CONV_W:])
    c = lax.conv_general_dilated(a, w_dw, window_strides=(1,), padding=[(CONV_K - 1, 0)],
                                 dimension_numbers=('NWC', 'WIO', 'NWC'),
                                 feature_group_count=CONV_W) + b_dw
    return jax.nn.silu(layernorm(c, ln_g, ln_b))


def memory_cross_attention(qc, mem_n, w_mem_kv):
    B, S = qc.shape[0], qc.shape[1]
    kv = mem_n @ w_mem_kv
    km = kv[..., :CROSS_W].reshape(B, MEM_LEN, N_CROSS_HEADS, CROSS_HEAD_DIM)
    vm = kv[..., CROSS_W:].reshape(B, MEM_LEN, N_CROSS_HEADS, CROSS_HEAD_DIM)
    q = qc.reshape(B, S, N_CROSS_HEADS, CROSS_HEAD_DIM)
    s = jnp.einsum('bshd,bmhd->bhsm', q, km).astype(jnp.float32) / math.sqrt(CROSS_HEAD_DIM)
    p = jax.nn.softmax(s, axis=-1)
    o = jnp.einsum('bhsm,bmhd->bshd', p.astype(vm.dtype), vm)
    return o.reshape(B, S, CROSS_W)


def setup_inputs(seed: int = 0) -> dict:
    key = jax.random.key(seed)
    ks = jax.random.split(key, 20)
    nrm = lambda k, shape, fan_in: jax.random.normal(k, shape, jnp.float32) * (fan_in ** -0.5)
    x = jax.random.normal(ks[0], (BATCH, SEQ, D_MODEL), jnp.float32)
    mem = jax.random.normal(ks[1], (BATCH, MEM_LEN, D_MODEL), jnp.float32)
    offset = jax.random.randint(ks[2], (BATCH, 1), 0, 1024, dtype=jnp.int32)
    positions = offset + jnp.arange(SEQ, dtype=jnp.int32)[None, :]
    return {
        "x": x,
        "mem": mem,
        "positions": positions,
        "norm_g": 1.0 + 0.02 * jax.random.normal(ks[3], (DEPTH, D_MODEL), jnp.float32),
        "w_in": nrm(ks[4], (DEPTH, D_MODEL, IN_W), D_MODEL),
        "attn_sinks": 0.5 * jax.random.normal(ks[5], (DEPTH, N_Q_HEADS), jnp.float32),
        "w_o_attn": nrm(ks[6], (DEPTH, ATTN_W, D_MODEL), ATTN_W),
        "b_glu": 0.02 * jax.random.normal(ks[7], (DEPTH, 2 * CONV_W), jnp.float32),
        "w_dw": nrm(ks[8], (DEPTH, CONV_K, 1, CONV_W), CONV_K),
        "b_dw": 0.02 * jax.random.normal(ks[9], (DEPTH, CONV_W), jnp.float32),
        "ln_g": 1.0 + 0.02 * jax.random.normal(ks[10], (DEPTH, CONV_W), jnp.float32),
        "ln_b": 0.02 * jax.random.normal(ks[11], (DEPTH, CONV_W), jnp.float32),
        "w_pw": nrm(ks[12], (DEPTH, CONV_W, D_MODEL), CONV_W),
        "b_pw": 0.02 * jax.random.normal(ks[13], (DEPTH, D_MODEL), jnp.float32),
        "mem_norm_g": 1.0 + 0.02 * jax.random.normal(ks[14], (DEPTH, D_MODEL), jnp.float32),
        "w_mem_kv": nrm(ks[15], (DEPTH, D_MODEL, 2 * CROSS_W), D_MODEL),
        "w_o_cross": nrm(ks[16], (DEPTH, CROSS_W, D_MODEL), CROSS_W),
        "w_out": nrm(ks[17], (DEPTH, D_MODEL, D_MODEL), D_MODEL),
        "final_norm_g": 1.0 + 0.02 * jax.random.normal(ks[18], (D_MODEL,), jnp.float32),
    }


def reference(x, mem, positions, norm_g, w_in, attn_sinks, w_o_attn, b_glu, w_dw, b_dw,
              ln_g, ln_b, w_pw, b_pw, mem_norm_g, w_mem_kv, w_o_cross, w_out, final_norm_g):
    B, S = x.shape[0], x.shape[1]
    for l in range(DEPTH):
        h = rmsnorm(x, norm_g[l])
        proj = h @ w_in[l]
        (q, k, v, g_attn, u_conv, g_conv, q_cross, g_cross, merge) = jnp.split(proj, SPLIT_POINTS, axis=-1)
        a = sliding_window_attention(q.reshape(B, S, N_Q_HEADS, HEAD_DIM),
                                     k.reshape(B, S, N_KV_HEADS, HEAD_DIM),
                                     v.reshape(B, S, N_KV_HEADS, HEAD_DIM),
                                     positions, attn_sinks[l])
        y_attn = (a * jax.nn.silu(g_attn)) @ w_o_attn[l]
        c = conformer_conv(u_conv, b_glu[l], w_dw[l], b_dw[l], ln_g[l], ln_b[l])
        y_conv = (c * jax.nn.silu(g_conv)) @ w_pw[l] + b_pw[l]
        mem_n = rmsnorm(mem, mem_norm_g[l])
        m = memory_cross_attention(q_cross, mem_n, w_mem_kv[l])
        y_cross = (m * jax.nn.silu(g_cross)) @ w_o_cross[l]
        gates = jax.nn.sigmoid(merge).reshape(B, S, N_BRANCH, D_MODEL)
        merged = gates[:, :, 0] * y_attn + gates[:, :, 1] * y_conv + gates[:, :, 2] * y_cross
        x = x + merged @ w_out[l]
    return rmsnorm(x, final_norm_g)
```

```python
import functools
import math

import jax
import jax.numpy as jnp
from jax import lax
from jax.experimental import pallas as pl
from jax.experimental.pallas import tpu as pltpu

D_MODEL = 1024
MEM_LEN = 256
HEAD_DIM = 64
N_Q_HEADS = 8
N_KV_HEADS = 2
GROUP = N_Q_HEADS // N_KV_HEADS
WINDOW = 128
BLOCK = 128
ATTN_W = N_Q_HEADS * HEAD_DIM
KV_W = N_KV_HEADS * HEAD_DIM
CONV_W = D_MODEL // 2
CONV_K = 31
N_CROSS_HEADS = 4
CROSS_HEAD_DIM = 128
CROSS_W = N_CROSS_HEADS * CROSS_HEAD_DIM
RMS_EPS = 1e-6
LN_EPS = 1e-5
NEG_BIG = -1e30
ALIBI_SLOPES = tuple(2.0 ** (-8.0 * (i + 1) / N_Q_HEADS) for i in range(N_Q_HEADS))

_Q0 = 0
_K0 = _Q0 + ATTN_W
_V0 = _K0 + KV_W
_GA0 = _V0 + KV_W
_UC0 = _GA0 + ATTN_W
_GC0 = _UC0 + 2 * CONV_W
_QX0 = _GC0 + CONV_W
_GX0 = _QX0 + CROSS_W
_MG0 = _GX0 + CROSS_W
IN_W = _MG0 + 3 * D_MODEL

SEQ_TILE = 256
CONV_CARRY = 32
VMEM_LIMIT_BYTES = 56 * 1024 * 1024

_BF16 = jnp.bfloat16
_F32 = jnp.float32


def _dot(a, b):
    return jnp.dot(a, b, preferred_element_type=_F32)


def _dot_nt(a, b):
    return lax.dot_general(a, b, (((1,), (1,)), ((), ())), preferred_element_type=_F32)


def _rmsnorm(xf, g):
    y = xf * lax.rsqrt(jnp.mean(xf * xf, axis=-1, keepdims=True) + RMS_EPS)
    return y * g


def _sigmoid(x):
    return jax.nn.sigmoid(x)


def _silu(x):
    return x * jax.nn.sigmoid(x)


def _mem_kv_kernel(mem_ref, g_ref, w_ref, km_ref, vm_ref):
    mem_n = _rmsnorm(mem_ref[0], g_ref[...]).astype(_BF16)
    kv = _dot(mem_n, w_ref[...])
    km_ref[0] = kv[:, :CROSS_W].astype(_BF16)
    vm_ref[0] = kv[:, CROSS_W:].astype(_BF16)


def _trunk_kernel(sinks_ref, x_ref, qpos_ref, kpos_ref, km_ref, vm_ref, norm_g_ref, w_in_ref,
                  w_oa_ref, b_glu_ref, w_dw_ref, b_dw_ref, ln_g_ref, ln_b_ref, w_pw_ref, b_pw_ref,
                  w_oc_ref, w_out_ref, fng_ref, out_ref,
                  kmask_ref, vmask_ref, attn_ref, conv_ref):
    ts = x_ref.shape[1]
    n_blk = ts // BLOCK
    seq_step = pl.program_id(1)

    @pl.when(seq_step == 0)
    def _():
        kmask_ref[:, 0:BLOCK, :] = jnp.zeros((4, BLOCK, KV_W), _BF16)
        vmask_ref[:, 0:BLOCK, :] = jnp.zeros((4, BLOCK, KV_W), _BF16)
        conv_ref[0:CONV_CARRY, :] = jnp.zeros((CONV_CARRY, CONV_W), _F32)

    x = x_ref[0]
    h = _rmsnorm(x, norm_g_ref[...]).astype(_BF16)

    qkvg = _dot(h, w_in_ref[:, _Q0:_UC0])
    q = (qkvg[:, _Q0:_K0] * (1.0 / math.sqrt(HEAD_DIM))).astype(_BF16)
    k = qkvg[:, _K0:_V0]
    v = qkvg[:, _V0:_GA0]
    g_attn = qkvg[:, _GA0:_UC0]

    lo_half = lax.broadcasted_iota(jnp.int32, (ts, KV_W), 1) < HEAD_DIM
    zero = jnp.zeros((ts, KV_W), _F32)
    for src, dst_ref in ((k, kmask_ref), (v, vmask_ref)):
        rolled = pltpu.roll(src, HEAD_DIM, 1)
        dst_ref[0, BLOCK:, :] = jnp.where(lo_half, src, zero).astype(_BF16)
        dst_ref[1, BLOCK:, :] = jnp.where(lo_half, zero, rolled).astype(_BF16)
        dst_ref[2, BLOCK:, :] = jnp.where(lo_half, rolled, zero).astype(_BF16)
        dst_ref[3, BLOCK:, :] = jnp.where(lo_half, zero, src).astype(_BF16)

    lo_out = lax.broadcasted_iota(jnp.int32, (2 * BLOCK, 2 * HEAD_DIM), 1) < HEAD_DIM
    for j in range(n_blk):
        rows = slice(j * BLOCK, (j + 1) * BLOCK)
        band = slice(j * BLOCK, (j + 2) * BLOCK)
        qcol = jnp.broadcast_to(qpos_ref[0, j], (BLOCK, BLOCK)).T
        delta = jnp.concatenate([qcol, qcol], axis=1) - kpos_ref[0, j]
        allowed = (delta >= 0) & (delta < WINDOW)
        delta_f = delta.astype(_F32)
        for g in range(N_KV_HEADS):
            qg = jnp.concatenate([q[rows, (2 * g) * 128:(2 * g + 1) * 128],
                                  q[rows, (2 * g + 1) * 128:(2 * g + 2) * 128]], axis=0)
            probs = []
            inv_l = []
            for a in range(2):
                s = _dot_nt(qg, kmask_ref[2 * g + a, band, :])
                e_parts = []
                l_parts = []
                for r in range(2):
                    head = 2 * (2 * g + r) + a
                    sr = s[r * BLOCK:(r + 1) * BLOCK]
                    sr = jnp.where(allowed, sr - ALIBI_SLOPES[head] * delta_f, NEG_BIG)
                    sink = sinks_ref[0, head]
                    m = jnp.maximum(jnp.max(sr, axis=1, keepdims=True), sink)
                    e = jnp.exp(sr - m)
                    l_parts.append(jnp.sum(e, axis=1, keepdims=True) + jnp.exp(sink - m))
                    e_parts.append(e.astype(_BF16))
                probs.append(jnp.concatenate(e_parts, axis=0))
                inv_l.append(1.0 / jnp.concatenate(l_parts, axis=0))
            o = (_dot(probs[0], vmask_ref[2 * g, band, :])
                 + _dot(probs[1], vmask_ref[2 * g + 1, band, :]))
            o = o * jnp.where(lo_out, inv_l[0], inv_l[1])
            attn_ref[rows, (2 * g) * 128:(2 * g + 1) * 128] = o[:BLOCK]
            attn_ref[rows, (2 * g + 1) * 128:(2 * g + 2) * 128] = o[BLOCK:]

    kmask_ref[:, 0:BLOCK, :] = kmask_ref[:, ts:ts + BLOCK, :]
    vmask_ref[:, 0:BLOCK, :] = vmask_ref[:, ts:ts + BLOCK, :]

    y_attn = _dot((attn_ref[...] * _silu(g_attn)).astype(_BF16), w_oa_ref[...])
    merged = _sigmoid(_dot(h, w_in_ref[:, _MG0:_MG0 + D_MODEL])) * y_attn

    uc = _dot(h, w_in_ref[:, _UC0:_QX0])
    u = uc[:, :2 * CONV_W] + b_glu_ref[...]
    g_conv = uc[:, 2 * CONV_W:]
    conv_ref[CONV_CARRY:, :] = u[:, :CONV_W] * _sigmoid(u[:, CONV_W:])
    c = jnp.broadcast_to(b_dw_ref[...], (ts, CONV_W))
    first = CONV_CARRY - (CONV_K - 1)
    for tap in range(CONV_K):
        c = c + w_dw_ref[tap:tap + 1, :] * conv_ref[first + tap:first + tap + ts, :]
    conv_ref[0:CONV_CARRY, :] = conv_ref[ts:ts + CONV_CARRY, :]
    mu = jnp.mean(c, axis=-1, keepdims=True)
    cc = c - mu
    var = jnp.mean(cc * cc, axis=-1, keepdims=True)
    cn = cc * lax.rsqrt(var + LN_EPS) * ln_g_ref[...] + ln_b_ref[...]
    y_conv = _dot((_silu(cn) * _silu(g_conv)).astype(_BF16), w_pw_ref[...]) + b_pw_ref[...]
    merged = merged + _sigmoid(_dot(h, w_in_ref[:, _MG0 + D_MODEL:_MG0 + 2 * D_MODEL])) * y_conv

    qx = _dot(h, w_in_ref[:, _QX0:_MG0])
    g_cross = qx[:, CROSS_W:]
    heads = []
    for hh in range(N_CROSS_HEADS):
        cols = slice(hh * CROSS_HEAD_DIM, (hh + 1) * CROSS_HEAD_DIM)
        s = _dot_nt(qx[:, cols].astype(_BF16), km_ref[0, :, cols]) * (1.0 / math.sqrt(CROSS_HEAD_DIM))
        m = jnp.max(s, axis=1, keepdims=True)
        e = jnp.exp(s - m)
        l = jnp.sum(e, axis=1, keepdims=True)
        heads.append(_dot(e.astype(_BF16), vm_ref[0, :, cols]) * (1.0 / l))
    m_cross = jnp.concatenate(heads, axis=1)
    y_cross = _dot((m_cross * _silu(g_cross)).astype(_BF16), w_oc_ref[...])
    merged = merged + _sigmoid(_dot(h, w_in_ref[:, _MG0 + 2 * D_MODEL:IN_W])) * y_cross

    x_new = x + _dot(merged.astype(_BF16), w_out_ref[...])
    out_ref[0] = _rmsnorm(x_new, fng_ref[...])


def _const_spec(shape):
    nd = len(shape)
    return pl.BlockSpec(shape, lambda b, s: (0,) * nd, pipeline_mode=pl.Buffered(1))


@jax.jit
def kernel(x, mem, positions, norm_g, w_in, attn_sinks, w_o_attn, b_glu, w_dw, b_dw, ln_g, ln_b,
           w_pw, b_pw, mem_norm_g, w_mem_kv, w_o_cross, w_out, final_norm_g):
    batch, seq, d_model = x.shape
    assert d_model == D_MODEL and norm_g.shape[0] == 1 and w_in.shape == (1, D_MODEL, IN_W)
    assert seq % SEQ_TILE == 0 and SEQ_TILE % BLOCK == 0 and mem.shape[1] == MEM_LEN
    ts = SEQ_TILE
    n_blk_tile = ts // BLOCK
    n_blk = seq // BLOCK

    km, vm = pl.pallas_call(
        _mem_kv_kernel,
        grid=(batch,),
        in_specs=[pl.BlockSpec((1, MEM_LEN, D_MODEL), lambda b: (b, 0, 0)),
                  pl.BlockSpec((1, D_MODEL), lambda b: (0, 0)),
                  pl.BlockSpec((D_MODEL, 2 * CROSS_W), lambda b: (0, 0))],
        out_specs=[pl.BlockSpec((1, MEM_LEN, CROSS_W), lambda b: (b, 0, 0)),
                   pl.BlockSpec((1, MEM_LEN, CROSS_W), lambda b: (b, 0, 0))],
        out_shape=[jax.ShapeDtypeStruct((batch, MEM_LEN, CROSS_W), _BF16)] * 2,
        name="mem_kv",
    )(mem, mem_norm_g, w_mem_kv[0].astype(_BF16))

    pad_pos = jnp.broadcast_to(positions[:, :1] - (WINDOW + 1), (batch, BLOCK))
    pos_p = jnp.concatenate([pad_pos, positions], axis=1).reshape(batch, n_blk + 1, BLOCK)
    kpos = jnp.concatenate([pos_p[:, :-1], pos_p[:, 1:]], axis=2).reshape(batch, n_blk, 1, 2 * BLOCK)
    qpos = positions.reshape(batch, n_blk, 1, BLOCK)

    row = lambda p: p.reshape(1, -1)
    in_specs = [
        pl.BlockSpec(memory_space=pltpu.SMEM),
        pl.BlockSpec((1, ts, D_MODEL), lambda b, s: (b, s, 0)),
        pl.BlockSpec((1, n_blk_tile, 1, BLOCK), lambda b, s: (b, s, 0, 0)),
        pl.BlockSpec((1, n_blk_tile, 1, 2 * BLOCK), lambda b, s: (b, s, 0, 0)),
        pl.BlockSpec((1, MEM_LEN, CROSS_W), lambda b, s: (b, 0, 0)),
        pl.BlockSpec((1, MEM_LEN, CROSS_W), lambda b, s: (b, 0, 0)),
        _const_spec((1, D_MODEL)),
        _const_spec((D_MODEL, IN_W)),
        _const_spec((ATTN_W, D_MODEL)),
        _const_spec((1, 2 * CONV_W)),
        _const_spec((CONV_K, CONV_W)),
        _const_spec((1, CONV_W)),
        _const_spec((1, CONV_W)),
        _const_spec((1, CONV_W)),
        _const_spec((CONV_W, D_MODEL)),
        _const_spec((1, D_MODEL)),
        _const_spec((CROSS_W, D_MODEL)),
        _const_spec((D_MODEL, D_MODEL)),
        _const_spec((1, D_MODEL)),
    ]
    return pl.pallas_call(
        _trunk_kernel,
        grid=(batch, seq // ts),
        in_specs=in_specs,
        out_specs=pl.BlockSpec((1, ts, D_MODEL), lambda b, s: (b, s, 0)),
        out_shape=jax.ShapeDtypeStruct((batch, seq, D_MODEL), x.dtype),
        scratch_shapes=[
            pltpu.VMEM((4, ts + BLOCK, KV_W), _BF16),
            pltpu.VMEM((4, ts + BLOCK, KV_W), _BF16),
            pltpu.VMEM((ts, ATTN_W), _F32),
            pltpu.VMEM((ts + CONV_CARRY, CONV_W), _F32),
        ],
        compiler_params=pltpu.CompilerParams(
            dimension_semantics=("arbitrary", "arbitrary"),
            vmem_limit_bytes=VMEM_LIMIT_BYTES),
        name="trunk",
    )(attn_sinks, x, qpos, kpos, km, vm, norm_g, w_in[0].astype(_BF16),
      w_o_attn[0].astype(_BF16), b_glu, w_dw.reshape(CONV_K, CONV_W), b_dw, ln_g, ln_b,
      w_pw[0].astype(_BF16), b_pw, w_o_cross[0].astype(_BF16), w_out[0].astype(_BF16),
      row(final_norm_g))
```

```python
import math

import jax
import jax.numpy as jnp
from jax import lax
from jax.experimental import pallas as pl
from jax.experimental.pallas import tpu as pltpu

D_MODEL = 1024
MEM_LEN = 256
HEAD_DIM = 64
N_Q_HEADS = 8
N_KV_HEADS = 2
GROUP = N_Q_HEADS // N_KV_HEADS
WINDOW = 128
BLOCK = 128
ATTN_W = N_Q_HEADS * HEAD_DIM
KV_W = N_KV_HEADS * HEAD_DIM
CONV_W = D_MODEL // 2
CONV_K = 31
N_CROSS_HEADS = 4
CROSS_HEAD_DIM = 128
CROSS_W = N_CROSS_HEADS * CROSS_HEAD_DIM
RMS_EPS = 1e-6
LN_EPS = 1e-5
NEG_BIG = -1e30
ALIBI_SLOPES = tuple(2.0 ** (-8.0 * (i + 1) / N_Q_HEADS) for i in range(N_Q_HEADS))

_Q0 = 0
_K0 = _Q0 + ATTN_W
_V0 = _K0 + KV_W
_GA0 = _V0 + KV_W
_UC0 = _GA0 + ATTN_W
_GC0 = _UC0 + 2 * CONV_W
_QX0 = _GC0 + CONV_W
_GX0 = _QX0 + CROSS_W
_MG0 = _GX0 + CROSS_W
IN_W = _MG0 + 3 * D_MODEL

LANES = 128
MXU_COLS = 256
SEQ_TILE = 256
CONV_CARRY = 32
CONV_ROWS = 128
VMEM_LIMIT_BYTES = 56 * 1024 * 1024

_BF16 = jnp.bfloat16
_F32 = jnp.float32


def _dot(a, b):
    return jnp.dot(a, b, preferred_element_type=_F32)


def _dot_nt(a, b):
    return lax.dot_general(a, b, (((1,), (1,)), ((), ())), preferred_element_type=_F32)


def _rmsnorm(xf, g):
    y = xf * lax.rsqrt(jnp.mean(xf * xf, axis=-1, keepdims=True) + RMS_EPS)
    return y * g


def _sigmoid(x):
    return jax.nn.sigmoid(x)


def _silu(x):
    return x * jax.nn.sigmoid(x)


def _after(value, anchor):
    rows = 32 // value.dtype.itemsize
    never = pl.program_id(0) < 0
    head = jnp.where(never, anchor[:rows, :LANES].astype(value.dtype), value[:rows, :LANES])
    top = jnp.concatenate([head, value[:rows, LANES:]], axis=1)
    return jnp.concatenate([top, value[rows:]], axis=0)


def _mem_kv_kernel(mem_ref, g_ref, w_ref, km_ref, vm_ref):
    mem_n = _rmsnorm(mem_ref[0], g_ref[...]).astype(_BF16)
    kv = _dot(mem_n, w_ref[...])
    km_ref[0] = kv[:, :CROSS_W].astype(_BF16)
    vm_ref[0] = kv[:, CROSS_W:].astype(_BF16)


def _trunk_kernel(sinks_ref, x_ref, qpos_ref, kpos_ref, km_ref, vm_ref, norm_g_ref, w_in_ref,
                  w_oa_ref, b_glu_ref, w_dw_ref, b_dw_ref, ln_g_ref, ln_b_ref, w_pw_ref, b_pw_ref,
                  w_oc_ref, w_out_ref, fng_ref, out_ref,
                  kmask_ref, vmask_ref, attn_ref, conv_ref):
    ts = x_ref.shape[1]
    n_blk = ts // BLOCK
    n_col = D_MODEL // MXU_COLS
    seq_step = pl.program_id(1)

    @pl.when(seq_step == 0)
    def _():
        kmask_ref[:, 0:BLOCK, :] = jnp.zeros((4, BLOCK, KV_W), _BF16)
        vmask_ref[:, 0:BLOCK, :] = jnp.zeros((4, BLOCK, KV_W), _BF16)
        conv_ref[:, 0:CONV_CARRY, :] = jnp.zeros((CONV_W // LANES, CONV_CARRY, LANES), _F32)

    x = x_ref[0]
    h = _rmsnorm(x, norm_g_ref[...]).astype(_BF16)

    def proj(c0, width=MXU_COLS):
        return _dot(h, w_in_ref[:, c0:c0 + width])

    def cols(c):
        return slice(c * MXU_COLS, (c + 1) * MXU_COLS)

    qkvg = proj(_Q0, _UC0 - _Q0)
    q = (qkvg[:, _Q0:_K0] * (1.0 / math.sqrt(HEAD_DIM))).astype(_BF16)
    k = qkvg[:, _K0:_V0]
    v = qkvg[:, _V0:_GA0]
    g_attn = qkvg[:, _GA0:_UC0]

    lo_half = lax.broadcasted_iota(jnp.int32, (ts, KV_W), 1) < HEAD_DIM
    zero = jnp.zeros((ts, KV_W), _F32)
    for src, dst_ref in ((k, kmask_ref), (v, vmask_ref)):
        rolled = pltpu.roll(src, HEAD_DIM, 1)
        dst_ref[0, BLOCK:, :] = jnp.where(lo_half, src, zero).astype(_BF16)
        dst_ref[1, BLOCK:, :] = jnp.where(lo_half, zero, rolled).astype(_BF16)
        dst_ref[2, BLOCK:, :] = jnp.where(lo_half, rolled, zero).astype(_BF16)
        dst_ref[3, BLOCK:, :] = jnp.where(lo_half, zero, src).astype(_BF16)

    lo_out = lax.broadcasted_iota(jnp.int32, (2 * BLOCK, 2 * HEAD_DIM), 1) < HEAD_DIM
    band_cache = {}

    def band_terms(j):
        if j not in band_cache:
            qcol = jnp.broadcast_to(qpos_ref[0, j], (BLOCK, BLOCK)).T
            delta = jnp.concatenate([qcol, qcol], axis=1) - kpos_ref[0, j]
            band_cache[j] = ((delta >= 0) & (delta < WINDOW), delta.astype(_F32))
        return band_cache[j]

    def scores(j, g):
        rows = slice(j * BLOCK, (j + 1) * BLOCK)
        band = slice(j * BLOCK, (j + 2) * BLOCK)
        qg = jnp.concatenate([q[rows, (2 * g) * LANES:(2 * g + 1) * LANES],
                              q[rows, (2 * g + 1) * LANES:(2 * g + 2) * LANES]], axis=0)
        return [_dot_nt(qg, kmask_ref[2 * g + a, band, :]) for a in range(2)]

    def attend(j, g, s_pair, anchors):
        rows = slice(j * BLOCK, (j + 1) * BLOCK)
        band = slice(j * BLOCK, (j + 2) * BLOCK)
        allowed, delta_f = band_terms(j)
        probs = []
        inv_l = []
        for a in range(2):
            s = s_pair[a]
            e_parts = []
            l_parts = []
            for r in range(2):
                head = 2 * (2 * g + r) + a
                sr = s[r * BLOCK:(r + 1) * BLOCK]
                sr = jnp.where(allowed, sr - ALIBI_SLOPES[head] * delta_f, NEG_BIG)
                sink = sinks_ref[0, head]
                m = jnp.maximum(jnp.max(sr, axis=1, keepdims=True), sink)
                e = jnp.exp(sr - m)
                l_parts.append(jnp.sum(e, axis=1, keepdims=True) + jnp.exp(sink - m))
                e_parts.append(e.astype(_BF16))
            p_a = jnp.concatenate(e_parts, axis=0)
            if a < len(anchors):
                p_a = _after(p_a, anchors[a])
            probs.append(p_a)
            inv_l.append(1.0 / jnp.concatenate(l_parts, axis=0))
        o = (_dot(probs[0], vmask_ref[2 * g, band, :])
             + _dot(probs[1], vmask_ref[2 * g + 1, band, :]))
        o = o * jnp.where(lo_out, inv_l[0], inv_l[1])
        attn_ref[rows, (2 * g) * LANES:(2 * g + 1) * LANES] = o[:BLOCK]
        attn_ref[rows, (2 * g + 1) * LANES:(2 * g + 2) * LANES] = o[BLOCK:]

    n_uc = (_QX0 - _UC0) // MXU_COLS
    attn_todo = [(j, g) for j in range(n_blk) for g in range(N_KV_HEADS)]
    s_all = [scores(j, g) for j, g in attn_todo]
    uc = [_dot(_after(h, s_all[i % len(attn_todo)][0]), w_in_ref[:, _UC0 + i * MXU_COLS:_UC0 + (i + 1) * MXU_COLS])
          for i in range(n_uc)]
    for i, (j, g) in enumerate(attn_todo):
        attend(j, g, s_all[i], uc[i::len(attn_todo)])

    kmask_ref[:, 0:BLOCK, :] = kmask_ref[:, ts:ts + BLOCK, :]
    vmask_ref[:, 0:BLOCK, :] = vmask_ref[:, ts:ts + BLOCK, :]

    n_glu = CONV_W // MXU_COLS
    for c in range(n_glu):
        a_c = ((uc[c] + b_glu_ref[:, cols(c)])
               * _sigmoid(uc[n_glu + c] + b_glu_ref[:, CONV_W + c * MXU_COLS:CONV_W + (c + 1) * MXU_COLS]))
        for i in range(MXU_COLS // LANES):
            conv_ref[c * (MXU_COLS // LANES) + i, CONV_CARRY:, :] = a_c[:, i * LANES:(i + 1) * LANES]
    g_conv = jnp.concatenate(uc[2 * n_glu:], axis=1)

    first_tap_row = CONV_CARRY - (CONV_K - 1)
    conv_out = {}

    def conv_piece(lc, r0):
        lanes = slice(lc * LANES, (lc + 1) * LANES)
        acc = jnp.broadcast_to(b_dw_ref[:, lanes], (CONV_ROWS, LANES))
        for t in range(CONV_K):
            off = r0 + first_tap_row + t
            acc = acc + w_dw_ref[t:t + 1, lanes] * conv_ref[lc, off:off + CONV_ROWS, :]
        conv_out[(lc, r0)] = acc

    conv_act = {}

    def conv_norm(r0):
        c = jnp.concatenate([conv_out.pop((lc, r0)) for lc in range(CONV_W // LANES)], axis=1)
        mu = jnp.mean(c, axis=-1, keepdims=True)
        cc = c - mu
        var = jnp.mean(cc * cc, axis=-1, keepdims=True)
        cn = cc * lax.rsqrt(var + LN_EPS) * ln_g_ref[...] + ln_b_ref[...]
        conv_act[r0] = (_silu(cn) * _silu(g_conv[r0:r0 + CONV_ROWS])).astype(_BF16)

    cross_out = {}
    qx = {}

    def cross_head(hh):
        per_chunk = MXU_COLS // CROSS_HEAD_DIM
        qh = qx[hh // per_chunk][:, (hh % per_chunk) * CROSS_HEAD_DIM:(hh % per_chunk + 1) * CROSS_HEAD_DIM]
        kv_cols = slice(hh * CROSS_HEAD_DIM, (hh + 1) * CROSS_HEAD_DIM)
        s = _dot_nt(qh.astype(_BF16), km_ref[0, :, kv_cols]) * (1.0 / math.sqrt(CROSS_HEAD_DIM))
        m = jnp.max(s, axis=1, keepdims=True)
        e = jnp.exp(s - m)
        l = jnp.sum(e, axis=1, keepdims=True)
        cross_out[hh] = _dot(e.astype(_BF16), vm_ref[0, :, kv_cols]) * (1.0 / l)

    vec_todo = []
    for r0 in range(0, ts, CONV_ROWS):
        vec_todo += [(conv_piece, (lc, r0)) for lc in range(CONV_W // LANES)]
        vec_todo.append((conv_norm, (r0,)))
    n_qx = (_MG0 - _QX0) // MXU_COLS
    gates = []
    for i in range(n_qx + 3 * n_col):
        if i < n_qx:
            qx[i] = proj(_QX0 + i * MXU_COLS)
            if i == CROSS_W // MXU_COLS - 1:
                vec_todo[2:2] = [(cross_head, (hh,)) for hh in range(N_CROSS_HEADS)]
        else:
            gates.append(_sigmoid(proj(_MG0 + (i - n_qx) * MXU_COLS)))
        if vec_todo:
            fn, args = vec_todo.pop(0)
            fn(*args)
    while vec_todo:
        fn, args = vec_todo.pop(0)
        fn(*args)
    conv_ref[:, 0:CONV_CARRY, :] = conv_ref[:, ts:ts + CONV_CARRY, :]

    g_cross = jnp.concatenate([qx[i] for i in range(CROSS_W // MXU_COLS, n_qx)], axis=1)
    attn_act = (attn_ref[...] * _silu(g_attn)).astype(_BF16)
    conv_all = jnp.concatenate([conv_act[r0] for r0 in range(0, ts, CONV_ROWS)], axis=0)
    cross_act = (jnp.concatenate([cross_out[hh] for hh in range(N_CROSS_HEADS)], axis=1)
                 * _silu(g_cross)).astype(_BF16)
    merged = []
    for c in range(n_col):
        y_attn = _dot(attn_act, w_oa_ref[:, cols(c)])
        y_conv = _dot(conv_all, w_pw_ref[:, cols(c)]) + b_pw_ref[:, cols(c)]
        y_cross = _dot(cross_act, w_oc_ref[:, cols(c)])
        merged.append((gates[c] * y_attn + gates[n_col + c] * y_conv
                       + gates[2 * n_col + c] * y_cross).astype(_BF16))
    merged = jnp.concatenate(merged, axis=1)

    x_new = jnp.concatenate([x[:, cols(c)] + _dot(merged, w_out_ref[:, cols(c)]) for c in range(n_col)],
                            axis=1)
    out_ref[0] = _rmsnorm(x_new, fng_ref[...])


def _const_spec(shape):
    nd = len(shape)
    return pl.BlockSpec(shape, lambda b, s: (0,) * nd, pipeline_mode=pl.Buffered(1))


@jax.jit
def kernel(x, mem, positions, norm_g, w_in, attn_sinks, w_o_attn, b_glu, w_dw, b_dw, ln_g, ln_b,
           w_pw, b_pw, mem_norm_g, w_mem_kv, w_o_cross, w_out, final_norm_g):
    batch, seq, d_model = x.shape
    assert d_model == D_MODEL and norm_g.shape[0] == 1 and w_in.shape == (1, D_MODEL, IN_W)
    assert seq % SEQ_TILE == 0 and SEQ_TILE % BLOCK == 0 and mem.shape[1] == MEM_LEN
    ts = SEQ_TILE
    n_blk_tile = ts // BLOCK
    n_blk = seq // BLOCK

    km, vm = pl.pallas_call(
        _mem_kv_kernel,
        grid=(batch,),
        in_specs=[pl.BlockSpec((1, MEM_LEN, D_MODEL), lambda b: (b, 0, 0)),
                  pl.BlockSpec((1, D_MODEL), lambda b: (0, 0)),
                  pl.BlockSpec((D_MODEL, 2 * CROSS_W), lambda b: (0, 0))],
        out_specs=[pl.BlockSpec((1, MEM_LEN, CROSS_W), lambda b: (b, 0, 0)),
                   pl.BlockSpec((1, MEM_LEN, CROSS_W), lambda b: (b, 0, 0))],
        out_shape=[jax.ShapeDtypeStruct((batch, MEM_LEN, CROSS_W), _BF16)] * 2,
        name="mem_kv",
    )(mem, mem_norm_g, w_mem_kv[0].astype(_BF16))

    pad_pos = jnp.broadcast_to(positions[:, :1] - (WINDOW + 1), (batch, BLOCK))
    pos_p = jnp.concatenate([pad_pos, positions], axis=1).reshape(batch, n_blk + 1, BLOCK)
    kpos = jnp.concatenate([pos_p[:, :-1], pos_p[:, 1:]], axis=2).reshape(batch, n_blk, 1, 2 * BLOCK)
    qpos = positions.reshape(batch, n_blk, 1, BLOCK)

    in_specs = [
        pl.BlockSpec(memory_space=pltpu.SMEM),
        pl.BlockSpec((1, ts, D_MODEL), lambda b, s: (b, s, 0)),
        pl.BlockSpec((1, n_blk_tile, 1, BLOCK), lambda b, s: (b, s, 0, 0)),
        pl.BlockSpec((1, n_blk_tile, 1, 2 * BLOCK), lambda b, s: (b, s, 0, 0)),
        pl.BlockSpec((1, MEM_LEN, CROSS_W), lambda b, s: (b, 0, 0)),
        pl.BlockSpec((1, MEM_LEN, CROSS_W), lambda b, s: (b, 0, 0)),
        _const_spec((1, D_MODEL)),
        _const_spec((D_MODEL, IN_W)),
        _const_spec((ATTN_W, D_MODEL)),
        _const_spec((1, 2 * CONV_W)),
        _const_spec((CONV_K, CONV_W)),
        _const_spec((1, CONV_W)),
        _const_spec((1, CONV_W)),
        _const_spec((1, CONV_W)),
        _const_spec((CONV_W, D_MODEL)),
        _const_spec((1, D_MODEL)),
        _const_spec((CROSS_W, D_MODEL)),
        _const_spec((D_MODEL, D_MODEL)),
        _const_spec((1, D_MODEL)),
    ]
    return pl.pallas_call(
        _trunk_kernel,
        grid=(batch, seq // ts),
        in_specs=in_specs,
        out_specs=pl.BlockSpec((1, ts, D_MODEL), lambda b, s: (b, s, 0)),
        out_shape=jax.ShapeDtypeStruct((batch, seq, D_MODEL), x.dtype),
        scratch_shapes=[
            pltpu.VMEM((4, ts + BLOCK, KV_W), _BF16),
            pltpu.VMEM((4, ts + BLOCK, KV_W), _BF16),
            pltpu.VMEM((ts, ATTN_W), _F32),
            pltpu.VMEM((CONV_W // LANES, ts + CONV_CARRY, LANES), _F32),
        ],
        compiler_params=pltpu.CompilerParams(
            dimension_semantics=("arbitrary", "arbitrary"),
            vmem_limit_bytes=VMEM_LIMIT_BYTES),
        name="trunk",
    )(attn_sinks, x, qpos, kpos, km, vm, norm_g, w_in[0].astype(_BF16),
      w_o_attn[0].astype(_BF16), b_glu, w_dw.reshape(CONV_K, CONV_W), b_dw, ln_g, ln_b,
      w_pw[0].astype(_BF16), b_pw, w_o_cross[0].astype(_BF16), w_out[0].astype(_BF16),
      final_norm_g.reshape(1, D_MODEL))
```

```python
import math

import jax
import jax.numpy as jnp
from jax import lax
from jax.experimental import pallas as pl
from jax.experimental.pallas import tpu as pltpu

D_MODEL = 1024
MEM_LEN = 256
HEAD_DIM = 64
N_Q_HEADS = 8
N_KV_HEADS = 2
GROUP = N_Q_HEADS // N_KV_HEADS
WINDOW = 128
BLOCK = 128
ATTN_W = N_Q_HEADS * HEAD_DIM
KV_W = N_KV_HEADS * HEAD_DIM
CONV_W = D_MODEL // 2
CONV_K = 31
N_CROSS_HEADS = 4
CROSS_HEAD_DIM = 128
CROSS_W = N_CROSS_HEADS * CROSS_HEAD_DIM
RMS_EPS = 1e-6
LN_EPS = 1e-5
NEG_BIG = -1e30
ALIBI_SLOPES = tuple(2.0 ** (-8.0 * (i + 1) / N_Q_HEADS) for i in range(N_Q_HEADS))

_Q0 = 0
_K0 = _Q0 + ATTN_W
_V0 = _K0 + KV_W
_GA0 = _V0 + KV_W
_UC0 = _GA0 + ATTN_W
_GC0 = _UC0 + 2 * CONV_W
_QX0 = _GC0 + CONV_W
_GX0 = _QX0 + CROSS_W
_MG0 = _GX0 + CROSS_W
IN_W = _MG0 + 3 * D_MODEL

LANES = 128
MXU_COLS = 256
SEQ_TILE = 512
CONV_CARRY = 32
CONV_ROWS = 128
VMEM_LIMIT_BYTES = 56 * 1024 * 1024

_BF16 = jnp.bfloat16
_F32 = jnp.float32


def _dot(a, b):
    return jnp.dot(a, b, preferred_element_type=_F32)


def _dot_nt(a, b):
    return lax.dot_general(a, b, (((1,), (1,)), ((), ())), preferred_element_type=_F32)


def _rmsnorm(xf, g):
    y = xf * lax.rsqrt(jnp.mean(xf * xf, axis=-1, keepdims=True) + RMS_EPS)
    return y * g


def _sigmoid(x):
    return jax.nn.sigmoid(x)


def _silu(x):
    return x * jax.nn.sigmoid(x)


def _after(value, anchor):
    rows = 32 // value.dtype.itemsize
    never = pl.program_id(0) < 0
    head = jnp.where(never, anchor[:rows, :LANES].astype(value.dtype), value[:rows, :LANES])
    top = jnp.concatenate([head, value[:rows, LANES:]], axis=1)
    return jnp.concatenate([top, value[rows:]], axis=0)


def _mem_kv_kernel(mem_ref, g_ref, w_ref, km_ref, vm_ref):
    mem_n = _rmsnorm(mem_ref[0], g_ref[...]).astype(_BF16)
    kv = _dot(mem_n, w_ref[...])
    km_ref[0] = kv[:, :CROSS_W].astype(_BF16)
    vm_ref[0] = kv[:, CROSS_W:].astype(_BF16)


def _trunk_kernel(sinks_ref, x_ref, qpos_ref, kpos_ref, km_ref, vm_ref, norm_g_ref, w_in_ref,
                  w_oa_ref, b_glu_ref, w_dw_ref, b_dw_ref, ln_g_ref, ln_b_ref, w_pw_ref, b_pw_ref,
                  w_oc_ref, w_out_ref, fng_ref, out_ref,
                  kmask_ref, vmask_ref, attn_ref, conv_ref):
    ts = x_ref.shape[1]
    n_blk = ts // BLOCK
    n_col = D_MODEL // MXU_COLS
    seq_step = pl.program_id(1)

    @pl.when(seq_step == 0)
    def _():
        kmask_ref[:, 0:BLOCK, :] = jnp.zeros((4, BLOCK, KV_W), _BF16)
        vmask_ref[:, 0:BLOCK, :] = jnp.zeros((4, BLOCK, KV_W), _BF16)
        conv_ref[:, 0:CONV_CARRY, :] = jnp.zeros((CONV_W // LANES, CONV_CARRY, LANES), _F32)

    x = x_ref[0]
    h = _rmsnorm(x, norm_g_ref[...]).astype(_BF16)

    def proj(c0, width=MXU_COLS):
        return _dot(h, w_in_ref[:, c0:c0 + width])

    def cols(c):
        return slice(c * MXU_COLS, (c + 1) * MXU_COLS)

    qkvg = proj(_Q0, _UC0 - _Q0)
    q = (qkvg[:, _Q0:_K0] * (1.0 / math.sqrt(HEAD_DIM))).astype(_BF16)
    k = qkvg[:, _K0:_V0]
    v = qkvg[:, _V0:_GA0]
    g_attn = qkvg[:, _GA0:_UC0]

    lo_half = lax.broadcasted_iota(jnp.int32, (ts, KV_W), 1) < HEAD_DIM
    zero = jnp.zeros((ts, KV_W), _F32)
    for src, dst_ref in ((k, kmask_ref), (v, vmask_ref)):
        rolled = pltpu.roll(src, HEAD_DIM, 1)
        dst_ref[0, BLOCK:, :] = jnp.where(lo_half, src, zero).astype(_BF16)
        dst_ref[1, BLOCK:, :] = jnp.where(lo_half, zero, rolled).astype(_BF16)
        dst_ref[2, BLOCK:, :] = jnp.where(lo_half, rolled, zero).astype(_BF16)
        dst_ref[3, BLOCK:, :] = jnp.where(lo_half, zero, src).astype(_BF16)

    lo_out = lax.broadcasted_iota(jnp.int32, (2 * BLOCK, 2 * HEAD_DIM), 1) < HEAD_DIM
    band_cache = {}

    def band_terms(j):
        if j not in band_cache:
            qcol = jnp.broadcast_to(qpos_ref[0, j], (BLOCK, BLOCK)).T
            delta = jnp.concatenate([qcol, qcol], axis=1) - kpos_ref[0, j]
            band_cache[j] = ((delta >= 0) & (delta < WINDOW), delta.astype(_F32))
        return band_cache[j]

    def scores(j, g):
        rows = slice(j * BLOCK, (j + 1) * BLOCK)
        band = slice(j * BLOCK, (j + 2) * BLOCK)
        qg = jnp.concatenate([q[rows, (2 * g) * LANES:(2 * g + 1) * LANES],
                              q[rows, (2 * g + 1) * LANES:(2 * g + 2) * LANES]], axis=0)
        return [_dot_nt(qg, kmask_ref[2 * g + a, band, :]) for a in range(2)]

    def attend(j, g, s_pair, anchors):
        rows = slice(j * BLOCK, (j + 1) * BLOCK)
        band = slice(j * BLOCK, (j + 2) * BLOCK)
        allowed, delta_f = band_terms(j)
        probs = []
        inv_l = []
        for a in range(2):
            s = s_pair[a]
            e_parts = []
            l_parts = []
            for r in range(2):
                head = 2 * (2 * g + r) + a
                sr = s[r * BLOCK:(r + 1) * BLOCK]
                sr = jnp.where(allowed, sr - ALIBI_SLOPES[head] * delta_f, NEG_BIG)
                sink = sinks_ref[0, head]
                m = jnp.maximum(jnp.max(sr, axis=1, keepdims=True), sink)
                e = jnp.exp(sr - m)
                l_parts.append(jnp.sum(e, axis=1, keepdims=True) + jnp.exp(sink - m))
                e_parts.append(e.astype(_BF16))
            p_a = jnp.concatenate(e_parts, axis=0)
            if a < len(anchors):
                p_a = _after(p_a, anchors[a])
            probs.append(p_a)
            inv_l.append(1.0 / jnp.concatenate(l_parts, axis=0))
        o = (_dot(probs[0], vmask_ref[2 * g, band, :])
             + _dot(probs[1], vmask_ref[2 * g + 1, band, :]))
        o = o * jnp.where(lo_out, inv_l[0], inv_l[1])
        attn_ref[rows, (2 * g) * LANES:(2 * g + 1) * LANES] = o[:BLOCK]
        attn_ref[rows, (2 * g + 1) * LANES:(2 * g + 2) * LANES] = o[BLOCK:]

    n_uc = (_QX0 - _UC0) // MXU_COLS
    attn_todo = [(j, g) for j in range(n_blk) for g in range(N_KV_HEADS)]
    s_all = [scores(j, g) for j, g in attn_todo]
    uc = [_dot(_after(h, s_all[i % len(attn_todo)][0]), w_in_ref[:, _UC0 + i * MXU_COLS:_UC0 + (i + 1) * MXU_COLS])
          for i in range(n_uc)]
    for i, (j, g) in enumerate(attn_todo):
        attend(j, g, s_all[i], uc[i::len(attn_todo)])

    kmask_ref[:, 0:BLOCK, :] = kmask_ref[:, ts:ts + BLOCK, :]
    vmask_ref[:, 0:BLOCK, :] = vmask_ref[:, ts:ts + BLOCK, :]

    n_glu = CONV_W // MXU_COLS
    for c in range(n_glu):
        a_c = ((uc[c] + b_glu_ref[:, cols(c)])
               * _sigmoid(uc[n_glu + c] + b_glu_ref[:, CONV_W + c * MXU_COLS:CONV_W + (c + 1) * MXU_COLS]))
        for i in range(MXU_COLS // LANES):
            conv_ref[c * (MXU_COLS // LANES) + i, CONV_CARRY:, :] = a_c[:, i * LANES:(i + 1) * LANES]
    g_conv = jnp.concatenate(uc[2 * n_glu:], axis=1)

    first_tap_row = CONV_CARRY - (CONV_K - 1)
    conv_out = {}

    def conv_piece(lc, r0):
        lanes = slice(lc * LANES, (lc + 1) * LANES)
        acc = jnp.broadcast_to(b_dw_ref[:, lanes], (CONV_ROWS, LANES))
        for t in range(CONV_K):
            off = r0 + first_tap_row + t
            acc = acc + w_dw_ref[t:t + 1, lanes] * conv_ref[lc, off:off + CONV_ROWS, :]
        conv_out[(lc, r0)] = acc

    conv_act = {}

    def conv_norm(r0):
        c = jnp.concatenate([conv_out.pop((lc, r0)) for lc in range(CONV_W // LANES)], axis=1)
        mu = jnp.mean(c, axis=-1, keepdims=True)
        cc = c - mu
        var = jnp.mean(cc * cc, axis=-1, keepdims=True)
        cn = cc * lax.rsqrt(var + LN_EPS) * ln_g_ref[...] + ln_b_ref[...]
        conv_act[r0] = (_silu(cn) * _silu(g_conv[r0:r0 + CONV_ROWS])).astype(_BF16)

    cross_out = {}
    qx = {}

    def cross_head(hh):
        per_chunk = MXU_COLS // CROSS_HEAD_DIM
        qh = qx[hh // per_chunk][:, (hh % per_chunk) * CROSS_HEAD_DIM:(hh % per_chunk + 1) * CROSS_HEAD_DIM]
        kv_cols = slice(hh * CROSS_HEAD_DIM, (hh + 1) * CROSS_HEAD_DIM)
        s = _dot_nt(qh.astype(_BF16), km_ref[0, :, kv_cols]) * (1.0 / math.sqrt(CROSS_HEAD_DIM))
        m = jnp.max(s, axis=1, keepdims=True)
        e = jnp.exp(s - m)
        l = jnp.sum(e, axis=1, keepdims=True)
        cross_out[hh] = _dot(e.astype(_BF16), vm_ref[0, :, kv_cols]) * (1.0 / l)

    vec_todo = []
    for r0 in range(0, ts, CONV_ROWS):
        vec_todo += [(conv_piece, (lc, r0)) for lc in range(CONV_W // LANES)]
        vec_todo.append((conv_norm, (r0,)))
    n_qx = (_MG0 - _QX0) // MXU_COLS
    gates = []
    for i in range(n_qx + 3 * n_col):
        if i < n_qx:
            qx[i] = proj(_QX0 + i * MXU_COLS)
            if i == CROSS_W // MXU_COLS - 1:
                vec_todo[2:2] = [(cross_head, (hh,)) for hh in range(N_CROSS_HEADS)]
        else:
            gates.append(_sigmoid(proj(_MG0 + (i - n_qx) * MXU_COLS)))
        if vec_todo:
            fn, args = vec_todo.pop(0)
            fn(*args)
    while vec_todo:
        fn, args = vec_todo.pop(0)
        fn(*args)
    conv_ref[:, 0:CONV_CARRY, :] = conv_ref[:, ts:ts + CONV_CARRY, :]

    g_cross = jnp.concatenate([qx[i] for i in range(CROSS_W // MXU_COLS, n_qx)], axis=1)
    attn_act = (attn_ref[...] * _silu(g_attn)).astype(_BF16)
    conv_all = jnp.concatenate([conv_act[r0] for r0 in range(0, ts, CONV_ROWS)], axis=0)
    cross_act = (jnp.concatenate([cross_out[hh] for hh in range(N_CROSS_HEADS)], axis=1)
                 * _silu(g_cross)).astype(_BF16)
    merged = []
    for c in range(n_col):
        y_attn = _dot(attn_act, w_oa_ref[:, cols(c)])
        y_conv = _dot(conv_all, w_pw_ref[:, cols(c)]) + b_pw_ref[:, cols(c)]
        y_cross = _dot(cross_act, w_oc_ref[:, cols(c)])
        merged.append((gates[c] * y_attn + gates[n_col + c] * y_conv
                       + gates[2 * n_col + c] * y_cross).astype(_BF16))
    merged = jnp.concatenate(merged, axis=1)

    x_new = jnp.concatenate([x[:, cols(c)] + _dot(merged, w_out_ref[:, cols(c)]) for c in range(n_col)],
                            axis=1)
    out_ref[0] = _rmsnorm(x_new, fng_ref[...])


def _const_spec(shape):
    nd = len(shape)
    return pl.BlockSpec(shape, lambda b, s: (0,) * nd, pipeline_mode=pl.Buffered(1))


@jax.jit
def kernel(x, mem, positions, norm_g, w_in, attn_sinks, w_o_attn, b_glu, w_dw, b_dw, ln_g, ln_b,
           w_pw, b_pw, mem_norm_g, w_mem_kv, w_o_cross, w_out, final_norm_g):
    batch, seq, d_model = x.shape
    assert d_model == D_MODEL and norm_g.shape[0] == 1 and w_in.shape == (1, D_MODEL, IN_W)
    assert seq % SEQ_TILE == 0 and SEQ_TILE % BLOCK == 0 and mem.shape[1] == MEM_LEN
    ts = SEQ_TILE
    n_blk_tile = ts // BLOCK
    n_blk = seq // BLOCK

    km, vm = pl.pallas_call(
        _mem_kv_kernel,
        grid=(batch,),
        in_specs=[pl.BlockSpec((1, MEM_LEN, D_MODEL), lambda b: (b, 0, 0)),
                  pl.BlockSpec((1, D_MODEL), lambda b: (0, 0)),
                  pl.BlockSpec((D_MODEL, 2 * CROSS_W), lambda b: (0, 0))],
        out_specs=[pl.BlockSpec((1, MEM_LEN, CROSS_W), lambda b: (b, 0, 0)),
                   pl.BlockSpec((1, MEM_LEN, CROSS_W), lambda b: (b, 0, 0))],
        out_shape=[jax.ShapeDtypeStruct((batch, MEM_LEN, CROSS_W), _BF16)] * 2,
        name="mem_kv",
    )(mem, mem_norm_g, w_mem_kv[0].astype(_BF16))

    pad_pos = jnp.broadcast_to(positions[:, :1] - (WINDOW + 1), (batch, BLOCK))
    pos_p = jnp.concatenate([pad_pos, positions], axis=1).reshape(batch, n_blk + 1, BLOCK)
    kpos = jnp.concatenate([pos_p[:, :-1], pos_p[:, 1:]], axis=2).reshape(batch, n_blk, 1, 2 * BLOCK)
    qpos = positions.reshape(batch, n_blk, 1, BLOCK)

    in_specs = [
        pl.BlockSpec(memory_space=pltpu.SMEM),
        pl.BlockSpec((1, ts, D_MODEL), lambda b, s: (b, s, 0)),
        pl.BlockSpec((1, n_blk_tile, 1, BLOCK), lambda b, s: (b, s, 0, 0)),
        pl.BlockSpec((1, n_blk_tile, 1, 2 * BLOCK), lambda b, s: (b, s, 0, 0)),
        pl.BlockSpec((1, MEM_LEN, CROSS_W), lambda b, s: (b, 0, 0)),
        pl.BlockSpec((1, MEM_LEN, CROSS_W), lambda b, s: (b, 0, 0)),
        _const_spec((1, D_MODEL)),
        _const_spec((D_MODEL, IN_W)),
        _const_spec((ATTN_W, D_MODEL)),
        _const_spec((1, 2 * CONV_W)),
        _const_spec((CONV_K, CONV_W)),
        _const_spec((1, CONV_W)),
        _const_spec((1, CONV_W)),
        _const_spec((1, CONV_W)),
        _const_spec((CONV_W, D_MODEL)),
        _const_spec((1, D_MODEL)),
        _const_spec((CROSS_W, D_MODEL)),
        _const_spec((D_MODEL, D_MODEL)),
        _const_spec((1, D_MODEL)),
    ]
    return pl.pallas_call(
        _trunk_kernel,
        grid=(batch, seq // ts),
        in_specs=in_specs,
        out_specs=pl.BlockSpec((1, ts, D_MODEL), lambda b, s: (b, s, 0)),
        out_shape=jax.ShapeDtypeStruct((batch, seq, D_MODEL), x.dtype),
        scratch_shapes=[
            pltpu.VMEM((4, ts + BLOCK, KV_W), _BF16),
            pltpu.VMEM((4, ts + BLOCK, KV_W), _BF16),
            pltpu.VMEM((ts, ATTN_W), _F32),
            pltpu.VMEM((CONV_W // LANES, ts + CONV_CARRY, LANES), _F32),
        ],
        compiler_params=pltpu.CompilerParams(
            dimension_semantics=("arbitrary", "arbitrary"),
            vmem_limit_bytes=VMEM_LIMIT_BYTES),
        name="trunk",
    )(attn_sinks, x, qpos, kpos, km, vm, norm_g, w_in[0].astype(_BF16),
      w_o_attn[0].astype(_BF16), b_glu, w_dw.reshape(CONV_K, CONV_W), b_dw, ln_g, ln_b,
      w_pw[0].astype(_BF16), b_pw, w_o_cross[0].astype(_BF16), w_out[0].astype(_BF16),
      final_norm_g.reshape(1, D_MODEL))
```

```python
import math

import jax
import jax.numpy as jnp
from jax import lax
from jax.experimental import pallas as pl
from jax.experimental.pallas import tpu as pltpu

D_MODEL = 1024
MEM_LEN = 256
HEAD_DIM = 64
N_Q_HEADS = 8
N_KV_HEADS = 2
GROUP = N_Q_HEADS // N_KV_HEADS
WINDOW = 128
BLOCK = 128
ATTN_W = N_Q_HEADS * HEAD_DIM
KV_W = N_KV_HEADS * HEAD_DIM
CONV_W = D_MODEL // 2
CONV_K = 31
N_CROSS_HEADS = 4
CROSS_HEAD_DIM = 128
CROSS_W = N_CROSS_HEADS * CROSS_HEAD_DIM
RMS_EPS = 1e-6
LN_EPS = 1e-5
NEG_BIG = -1e30
ALIBI_SLOPES = tuple(2.0 ** (-8.0 * (i + 1) / N_Q_HEADS) for i in range(N_Q_HEADS))

_Q0 = 0
_K0 = _Q0 + ATTN_W
_V0 = _K0 + KV_W
_GA0 = _V0 + KV_W
_UC0 = _GA0 + ATTN_W
_GC0 = _UC0 + 2 * CONV_W
_QX0 = _GC0 + CONV_W
_GX0 = _QX0 + CROSS_W
_MG0 = _GX0 + CROSS_W
IN_W = _MG0 + 3 * D_MODEL

_W_MEM0 = 0
_W_OUT0 = _W_MEM0 + D_MODEL
_W_OA0 = _W_OUT0 + D_MODEL
_W_PW0 = _W_OA0 + ATTN_W
_W_OC0 = _W_PW0 + CONV_W
W_STACK_ROWS = _W_OC0 + CROSS_W

LANES = 128
MXU_COLS = 256
SEQ_TILE = 512
CONV_CARRY = 32
CONV_ROWS = 128
VMEM_LIMIT_BYTES = 56 * 1024 * 1024

_BF16 = jnp.bfloat16
_F32 = jnp.float32


def _dot(a, b):
    return jnp.dot(a, b, preferred_element_type=_F32)


def _dot_nt(a, b):
    return lax.dot_general(a, b, (((1,), (1,)), ((), ())), preferred_element_type=_F32)


def _rmsnorm(xf, g):
    y = xf * lax.rsqrt(jnp.mean(xf * xf, axis=-1, keepdims=True) + RMS_EPS)
    return y * g


def _sigmoid(x):
    return jax.nn.sigmoid(x)


def _silu(x):
    return x * jax.nn.sigmoid(x)


def _after(value, anchor):
    rows = 32 // value.dtype.itemsize
    never = pl.program_id(0) < 0
    head = jnp.where(never, anchor[:rows, :LANES].astype(value.dtype), value[:rows, :LANES])
    top = jnp.concatenate([head, value[:rows, LANES:]], axis=1)
    return jnp.concatenate([top, value[rows:]], axis=0)


def _mem_kv_kernel(mem_ref, g_ref, w_ref, km_ref, vm_ref):
    mem_n = _rmsnorm(mem_ref[0], g_ref[...]).astype(_BF16)
    kv = _dot(mem_n, w_ref[...])
    km_ref[0] = kv[:, :CROSS_W].astype(_BF16)
    vm_ref[0] = kv[:, CROSS_W:].astype(_BF16)


def _trunk_kernel(sinks_ref, x_ref, pos_ref, km_ref, vm_ref, norm_g_ref, w_in_ref,
                  w_out_ref, w_oa_ref, w_pw_ref, w_oc_ref, b_glu_ref, w_dw_ref, b_dw_ref,
                  ln_g_ref, ln_b_ref, b_pw_ref, fng_ref, out_ref,
                  kmask_ref, vmask_ref, kpos_ref, attn_ref, conv_ref):
    ts = x_ref.shape[1]
    n_blk = ts // BLOCK
    n_col = D_MODEL // MXU_COLS
    seq_step = pl.program_id(1)

    @pl.when(seq_step == 0)
    def _():
        kmask_ref[:, 0:BLOCK, :] = jnp.zeros((4, BLOCK, KV_W), _BF16)
        vmask_ref[:, 0:BLOCK, :] = jnp.zeros((4, BLOCK, KV_W), _BF16)
        conv_ref[:, 0:CONV_CARRY, :] = jnp.zeros((CONV_W // LANES, CONV_CARRY, LANES), _F32)
        kpos_ref[...] = jnp.broadcast_to(pos_ref[0, 0][:, 0:1] - (WINDOW + 1), (1, BLOCK))

    x = x_ref[0]
    h = _rmsnorm(x, norm_g_ref[...]).astype(_BF16)

    def proj(c0, width=MXU_COLS):
        return _dot(h, w_in_ref[:, c0:c0 + width])

    def cols(c):
        return slice(c * MXU_COLS, (c + 1) * MXU_COLS)

    qkvg = proj(_Q0, _UC0 - _Q0)
    q = (qkvg[:, _Q0:_K0] * (1.0 / math.sqrt(HEAD_DIM))).astype(_BF16)
    k = qkvg[:, _K0:_V0]
    v = qkvg[:, _V0:_GA0]
    g_attn = qkvg[:, _GA0:_UC0]

    lo_half = lax.broadcasted_iota(jnp.int32, (ts, KV_W), 1) < HEAD_DIM
    zero = jnp.zeros((ts, KV_W), _F32)
    for src, dst_ref in ((k, kmask_ref), (v, vmask_ref)):
        rolled = pltpu.roll(src, HEAD_DIM, 1)
        dst_ref[0, BLOCK:, :] = jnp.where(lo_half, src, zero).astype(_BF16)
        dst_ref[1, BLOCK:, :] = jnp.where(lo_half, zero, rolled).astype(_BF16)
        dst_ref[2, BLOCK:, :] = jnp.where(lo_half, rolled, zero).astype(_BF16)
        dst_ref[3, BLOCK:, :] = jnp.where(lo_half, zero, src).astype(_BF16)

    lo_out = lax.broadcasted_iota(jnp.int32, (2 * BLOCK, 2 * HEAD_DIM), 1) < HEAD_DIM
    band_cache = {}

    def band_terms(j):
        if j not in band_cache:
            qcol = jnp.broadcast_to(pos_ref[0, j], (BLOCK, BLOCK)).T
            prev_pos = kpos_ref[...] if j == 0 else pos_ref[0, j - 1]
            delta = jnp.concatenate([qcol - prev_pos, qcol - pos_ref[0, j]], axis=1)
            band_cache[j] = ((delta >= 0) & (delta < WINDOW), delta.astype(_F32))
        return band_cache[j]

    def scores(j, g):
        rows = slice(j * BLOCK, (j + 1) * BLOCK)
        band = slice(j * BLOCK, (j + 2) * BLOCK)
        qg = jnp.concatenate([q[rows, (2 * g) * LANES:(2 * g + 1) * LANES],
                              q[rows, (2 * g + 1) * LANES:(2 * g + 2) * LANES]], axis=0)
        return [_dot_nt(qg, kmask_ref[2 * g + a, band, :]) for a in range(2)]

    def attend(j, g, s_pair, anchors):
        rows = slice(j * BLOCK, (j + 1) * BLOCK)
        band = slice(j * BLOCK, (j + 2) * BLOCK)
        allowed, delta_f = band_terms(j)
        probs = []
        inv_l = []
        for a in range(2):
            s = s_pair[a]
            e_parts = []
            l_parts = []
            for r in range(2):
                head = 2 * (2 * g + r) + a
                sr = s[r * BLOCK:(r + 1) * BLOCK]
                sr = jnp.where(allowed, sr - ALIBI_SLOPES[head] * delta_f, NEG_BIG)
                sink = sinks_ref[0, head]
                m = jnp.maximum(jnp.max(sr, axis=1, keepdims=True), sink)
                e = jnp.exp(sr - m)
                l_parts.append(jnp.sum(e, axis=1, keepdims=True) + jnp.exp(sink - m))
                e_parts.append(e.astype(_BF16))
            p_a = jnp.concatenate(e_parts, axis=0)
            if a < len(anchors):
                p_a = _after(p_a, anchors[a])
            probs.append(p_a)
            inv_l.append(1.0 / jnp.concatenate(l_parts, axis=0))
        o = (_dot(probs[0], vmask_ref[2 * g, band, :])
             + _dot(probs[1], vmask_ref[2 * g + 1, band, :]))
        o = o * jnp.where(lo_out, inv_l[0], inv_l[1])
        attn_ref[rows, (2 * g) * LANES:(2 * g + 1) * LANES] = o[:BLOCK]
        attn_ref[rows, (2 * g + 1) * LANES:(2 * g + 2) * LANES] = o[BLOCK:]

    n_uc = (_QX0 - _UC0) // MXU_COLS
    attn_todo = [(j, g) for j in range(n_blk) for g in range(N_KV_HEADS)]
    s_all = [scores(j, g) for j, g in attn_todo]
    uc = [_dot(_after(h, s_all[i % len(attn_todo)][0]), w_in_ref[:, _UC0 + i * MXU_COLS:_UC0 + (i + 1) * MXU_COLS])
          for i in range(n_uc)]
    for i, (j, g) in enumerate(attn_todo):
        attend(j, g, s_all[i], uc[i::len(attn_todo)])

    kmask_ref[:, 0:BLOCK, :] = kmask_ref[:, ts:ts + BLOCK, :]
    vmask_ref[:, 0:BLOCK, :] = vmask_ref[:, ts:ts + BLOCK, :]
    kpos_ref[...] = pos_ref[0, n_blk - 1]

    n_glu = CONV_W // MXU_COLS
    for c in range(n_glu):
        a_c = ((uc[c] + b_glu_ref[:, cols(c)])
               * _sigmoid(uc[n_glu + c] + b_glu_ref[:, CONV_W + c * MXU_COLS:CONV_W + (c + 1) * MXU_COLS]))
        for i in range(MXU_COLS // LANES):
            conv_ref[c * (MXU_COLS // LANES) + i, CONV_CARRY:, :] = a_c[:, i * LANES:(i + 1) * LANES]
    g_conv = jnp.concatenate(uc[2 * n_glu:], axis=1)

    first_tap_row = CONV_CARRY - (CONV_K - 1)
    conv_out = {}

    def conv_piece(lc, r0):
        lanes = slice(lc * LANES, (lc + 1) * LANES)
        acc = jnp.broadcast_to(b_dw_ref[:, lanes], (CONV_ROWS, LANES))
        for t in range(CONV_K):
            off = r0 + first_tap_row + t
            acc = acc + w_dw_ref[t:t + 1, lanes] * conv_ref[lc, off:off + CONV_ROWS, :]
        conv_out[(lc, r0)] = acc

    conv_act = {}

    def conv_norm(r0):
        c = jnp.concatenate([conv_out.pop((lc, r0)) for lc in range(CONV_W // LANES)], axis=1)
        mu = jnp.mean(c, axis=-1, keepdims=True)
        cc = c - mu
        var = jnp.mean(cc * cc, axis=-1, keepdims=True)
        cn = cc * lax.rsqrt(var + LN_EPS) * ln_g_ref[...] + ln_b_ref[...]
        conv_act[r0] = (_silu(cn) * _silu(g_conv[r0:r0 + CONV_ROWS])).astype(_BF16)

    cross_out = {}
    qx = {}

    def cross_head(hh):
        per_chunk = MXU_COLS // CROSS_HEAD_DIM
        qh = qx[hh // per_chunk][:, (hh % per_chunk) * CROSS_HEAD_DIM:(hh % per_chunk + 1) * CROSS_HEAD_DIM]
        kv_cols = slice(hh * CROSS_HEAD_DIM, (hh + 1) * CROSS_HEAD_DIM)
        s = _dot_nt(qh.astype(_BF16), km_ref[0, :, kv_cols]) * (1.0 / math.sqrt(CROSS_HEAD_DIM))
        m = jnp.max(s, axis=1, keepdims=True)
        e = jnp.exp(s - m)
        l = jnp.sum(e, axis=1, keepdims=True)
        cross_out[hh] = _dot(e.astype(_BF16), vm_ref[0, :, kv_cols]) * (1.0 / l)

    vec_todo = []
    for r0 in range(0, ts, CONV_ROWS):
        vec_todo += [(conv_piece, (lc, r0)) for lc in range(CONV_W // LANES)]
        vec_todo.append((conv_norm, (r0,)))
    n_qx = (_MG0 - _QX0) // MXU_COLS
    gates = []
    for i in range(n_qx + 3 * n_col):
        if i < n_qx:
            qx[i] = proj(_QX0 + i * MXU_COLS)
            if i == CROSS_W // MXU_COLS - 1:
                vec_todo[2:2] = [(cross_head, (hh,)) for hh in range(N_CROSS_HEADS)]
        else:
            gates.append(_sigmoid(proj(_MG0 + (i - n_qx) * MXU_COLS)))
        if vec_todo:
            fn, args = vec_todo.pop(0)
            fn(*args)
    while vec_todo:
        fn, args = vec_todo.pop(0)
        fn(*args)
    conv_ref[:, 0:CONV_CARRY, :] = conv_ref[:, ts:ts + CONV_CARRY, :]

    g_cross = jnp.concatenate([qx[i] for i in range(CROSS_W // MXU_COLS, n_qx)], axis=1)
    attn_act = (attn_ref[...] * _silu(g_attn)).astype(_BF16)
    conv_all = jnp.concatenate([conv_act[r0] for r0 in range(0, ts, CONV_ROWS)], axis=0)
    cross_act = (jnp.concatenate([cross_out[hh] for hh in range(N_CROSS_HEADS)], axis=1)
                 * _silu(g_cross)).astype(_BF16)
    merged = []
    for c in range(n_col):
        y_attn = _dot(attn_act, w_oa_ref[:, cols(c)])
        y_conv = _dot(conv_all, w_pw_ref[:, cols(c)]) + b_pw_ref[:, cols(c)]
        y_cross = _dot(cross_act, w_oc_ref[:, cols(c)])
        merged.append((gates[c] * y_attn + gates[n_col + c] * y_conv
                       + gates[2 * n_col + c] * y_cross).astype(_BF16))
    merged = jnp.concatenate(merged, axis=1)

    x_new = jnp.concatenate([x[:, cols(c)] + _dot(merged, w_out_ref[:, cols(c)]) for c in range(n_col)],
                            axis=1)
    out_ref[0] = _rmsnorm(x_new, fng_ref[...])


def _const_spec(shape, block_index=None):
    index = tuple(block_index) if block_index is not None else (0,) * len(shape)
    return pl.BlockSpec(shape, lambda b, s: index, pipeline_mode=pl.Buffered(1))


@jax.jit
def kernel(x, mem, positions, norm_g, w_in, attn_sinks, w_o_attn, b_glu, w_dw, b_dw, ln_g, ln_b,
           w_pw, b_pw, mem_norm_g, w_mem_kv, w_o_cross, w_out, final_norm_g):
    batch, seq, d_model = x.shape
    assert d_model == D_MODEL and norm_g.shape[0] == 1 and w_in.shape == (1, D_MODEL, IN_W)
    assert seq % SEQ_TILE == 0 and SEQ_TILE % BLOCK == 0 and mem.shape[1] == MEM_LEN
    ts = SEQ_TILE
    n_blk_tile = ts // BLOCK

    w_stack = jnp.concatenate([w_mem_kv[0], w_out[0], w_o_attn[0], w_pw[0], w_o_cross[0]],
                              axis=0).astype(_BF16)

    km, vm = pl.pallas_call(
        _mem_kv_kernel,
        grid=(batch,),
        in_specs=[pl.BlockSpec((1, MEM_LEN, D_MODEL), lambda b: (b, 0, 0)),
                  pl.BlockSpec((1, D_MODEL), lambda b: (0, 0)),
                  pl.BlockSpec((D_MODEL, 2 * CROSS_W), lambda b: (_W_MEM0 // D_MODEL, 0))],
        out_specs=[pl.BlockSpec((1, MEM_LEN, CROSS_W), lambda b: (b, 0, 0)),
                   pl.BlockSpec((1, MEM_LEN, CROSS_W), lambda b: (b, 0, 0))],
        out_shape=[jax.ShapeDtypeStruct((batch, MEM_LEN, CROSS_W), _BF16)] * 2,
        name="mem_kv",
    )(mem, mem_norm_g, w_stack)

    in_specs = [
        pl.BlockSpec(memory_space=pltpu.SMEM),
        pl.BlockSpec((1, ts, D_MODEL), lambda b, s: (b, s, 0)),
        pl.BlockSpec((1, n_blk_tile, 1, BLOCK), lambda b, s: (b, s, 0, 0)),
        pl.BlockSpec((1, MEM_LEN, CROSS_W), lambda b, s: (b, 0, 0)),
        pl.BlockSpec((1, MEM_LEN, CROSS_W), lambda b, s: (b, 0, 0)),
        _const_spec((1, D_MODEL)),
        _const_spec((D_MODEL, IN_W)),
        _const_spec((D_MODEL, D_MODEL), (_W_OUT0 // D_MODEL, 0)),
        _const_spec((ATTN_W, D_MODEL), (_W_OA0 // ATTN_W, 0)),
        _const_spec((CONV_W, D_MODEL), (_W_PW0 // CONV_W, 0)),
        _const_spec((CROSS_W, D_MODEL), (_W_OC0 // CROSS_W, 0)),
        _const_spec((1, 2 * CONV_W)),
        _const_spec((CONV_K, CONV_W)),
        _const_spec((1, CONV_W)),
        _const_spec((1, CONV_W)),
        _const_spec((1, CONV_W)),
        _const_spec((1, D_MODEL)),
        _const_spec((1, D_MODEL)),
    ]
    return pl.pallas_call(
        _trunk_kernel,
        grid=(batch, seq // ts),
        in_specs=in_specs,
        out_specs=pl.BlockSpec((1, ts, D_MODEL), lambda b, s: (b, s, 0)),
        out_shape=jax.ShapeDtypeStruct((batch, seq, D_MODEL), x.dtype),
        scratch_shapes=[
            pltpu.VMEM((4, ts + BLOCK, KV_W), _BF16),
            pltpu.VMEM((4, ts + BLOCK, KV_W), _BF16),
            pltpu.VMEM((1, BLOCK), jnp.int32),
            pltpu.VMEM((ts, ATTN_W), _F32),
            pltpu.VMEM((CONV_W // LANES, ts + CONV_CARRY, LANES), _F32),
        ],
        compiler_params=pltpu.CompilerParams(
            dimension_semantics=("arbitrary", "arbitrary"),
            vmem_limit_bytes=VMEM_LIMIT_BYTES),
        name="trunk",
    )(attn_sinks, x, positions.reshape(batch, seq // BLOCK, 1, BLOCK), km, vm, norm_g,
      w_in[0].astype(_BF16), w_stack, w_stack, w_stack, w_stack, b_glu,
      w_dw.reshape(CONV_K, CONV_W), b_dw, ln_g, ln_b, b_pw, final_norm_g.reshape(1, D_MODEL))
```

```python
import math

import jax
import jax.numpy as jnp
from jax import lax
from jax.experimental import pallas as pl
from jax.experimental.pallas import tpu as pltpu

D_MODEL = 1024
MEM_LEN = 256
HEAD_DIM = 64
N_Q_HEADS = 8
N_KV_HEADS = 2
GROUP = N_Q_HEADS // N_KV_HEADS
WINDOW = 128
BLOCK = 128
ATTN_W = N_Q_HEADS * HEAD_DIM
KV_W = N_KV_HEADS * HEAD_DIM
CONV_W = D_MODEL // 2
CONV_K = 31
N_CROSS_HEADS = 4
CROSS_HEAD_DIM = 128
CROSS_W = N_CROSS_HEADS * CROSS_HEAD_DIM
RMS_EPS = 1e-6
LN_EPS = 1e-5
NEG_BIG = -1e30
ALIBI_SLOPES = tuple(2.0 ** (-8.0 * (i + 1) / N_Q_HEADS) for i in range(N_Q_HEADS))

_Q0 = 0
_K0 = _Q0 + ATTN_W
_V0 = _K0 + KV_W
_GA0 = _V0 + KV_W
_UC0 = _GA0 + ATTN_W
_GC0 = _UC0 + 2 * CONV_W
_QX0 = _GC0 + CONV_W
_GX0 = _QX0 + CROSS_W
_MG0 = _GX0 + CROSS_W
IN_W = _MG0 + 3 * D_MODEL

_W_MEM0 = 0
_W_OUT0 = _W_MEM0 + D_MODEL
_W_OA0 = _W_OUT0 + D_MODEL
_W_PW0 = _W_OA0 + ATTN_W
_W_OC0 = _W_PW0 + CONV_W
W_STACK_ROWS = _W_OC0 + CROSS_W

LANES = 128
MXU_COLS = 256
SEQ_TILE = 512
CONV_CARRY = 32
CONV_ROWS = 128
VMEM_LIMIT_BYTES = 56 * 1024 * 1024

_BF16 = jnp.bfloat16
_F32 = jnp.float32


def _dot(a, b):
    return jnp.dot(a, b, preferred_element_type=_F32)


def _dot_nt(a, b):
    return lax.dot_general(a, b, (((1,), (1,)), ((), ())), preferred_element_type=_F32)


def _rmsnorm(xf, g):
    y = xf * lax.rsqrt(jnp.mean(xf * xf, axis=-1, keepdims=True) + RMS_EPS)
    return y * g


def _sigmoid(x):
    return jax.nn.sigmoid(x)


def _silu(x):
    return x * jax.nn.sigmoid(x)


def _after(value, anchor):
    rows = 32 // value.dtype.itemsize
    never = pl.program_id(0) < 0
    head = jnp.where(never, anchor[:rows, :LANES].astype(value.dtype), value[:rows, :LANES])
    top = jnp.concatenate([head, value[:rows, LANES:]], axis=1)
    return jnp.concatenate([top, value[rows:]], axis=0)


def _mem_kv_kernel(mem_ref, g_ref, w_ref, km_ref, vm_ref):
    mem_n = _rmsnorm(mem_ref[0], g_ref[...]).astype(_BF16)
    kv = _dot(mem_n, w_ref[...])
    km_ref[0] = kv[:, :CROSS_W].astype(_BF16)
    vm_ref[0] = kv[:, CROSS_W:].astype(_BF16)


def _trunk_kernel(sinks_ref, x_ref, pos_ref, km_ref, vm_ref, norm_g_ref, w_in_ref,
                  w_out_ref, w_oa_ref, w_pw_ref, w_oc_ref, b_glu_ref, w_dw_ref, b_dw_ref,
                  ln_g_ref, ln_b_ref, b_pw_ref, fng_ref, out_ref,
                  kmask_ref, vt_ref, kpos_ref, attn_ref, conv_ref):
    ts = x_ref.shape[1]
    n_blk = ts // BLOCK
    n_col = D_MODEL // MXU_COLS
    seq_step = pl.program_id(1)

    @pl.when(seq_step == 0)
    def _():
        kmask_ref[:, 0:BLOCK, :] = jnp.zeros((4, BLOCK, KV_W), _BF16)
        vt_ref[:, 0:BLOCK] = jnp.zeros((KV_W, BLOCK), _BF16)
        conv_ref[:, 0:CONV_CARRY, :] = jnp.zeros((CONV_W // LANES, CONV_CARRY, LANES), _F32)
        kpos_ref[...] = jnp.broadcast_to(pos_ref[0, 0][:, 0:1] - (WINDOW + 1), (1, BLOCK))

    x = x_ref[0]
    h = _rmsnorm(x, norm_g_ref[...]).astype(_BF16)

    def proj(c0, width=MXU_COLS):
        return _dot(h, w_in_ref[:, c0:c0 + width])

    def cols(c):
        return slice(c * MXU_COLS, (c + 1) * MXU_COLS)

    qkvg = proj(_Q0, _UC0 - _Q0)
    q = (qkvg[:, _Q0:_K0] * (1.0 / math.sqrt(HEAD_DIM))).astype(_BF16)
    k = qkvg[:, _K0:_V0]
    v = qkvg[:, _V0:_GA0]
    g_attn = qkvg[:, _GA0:_UC0]

    lo_half = lax.broadcasted_iota(jnp.int32, (ts, KV_W), 1) < HEAD_DIM
    zero = jnp.zeros((ts, KV_W), _F32)
    k_rolled = pltpu.roll(k, HEAD_DIM, 1)
    kmask_ref[0, BLOCK:, :] = jnp.where(lo_half, k, zero).astype(_BF16)
    kmask_ref[1, BLOCK:, :] = jnp.where(lo_half, zero, k_rolled).astype(_BF16)
    kmask_ref[2, BLOCK:, :] = jnp.where(lo_half, k_rolled, zero).astype(_BF16)
    kmask_ref[3, BLOCK:, :] = jnp.where(lo_half, zero, k).astype(_BF16)
    for b in range(n_blk):
        vt_ref[:, (b + 1) * BLOCK:(b + 2) * BLOCK] = v[b * BLOCK:(b + 1) * BLOCK, :].T.astype(_BF16)

    band_cache = {}

    def key_column(pos_row):
        return jnp.broadcast_to(pos_row, (BLOCK, BLOCK)).T

    def band_terms(j):
        if j not in band_cache:
            prev_pos = kpos_ref[...] if j == 0 else pos_ref[0, j - 1]
            key_pos = jnp.concatenate([key_column(prev_pos), key_column(pos_ref[0, j])], axis=0)
            delta = pos_ref[0, j] - key_pos
            band_cache[j] = ((delta >= 0) & (delta < WINDOW), delta.astype(_F32))
        return band_cache[j]

    def scores(j, g):
        rows = slice(j * BLOCK, (j + 1) * BLOCK)
        band = slice(j * BLOCK, (j + 2) * BLOCK)
        qg = jnp.concatenate([q[rows, (2 * g) * LANES:(2 * g + 1) * LANES],
                              q[rows, (2 * g + 1) * LANES:(2 * g + 2) * LANES]], axis=0)
        return [_dot_nt(kmask_ref[2 * g + a, band, :], qg) for a in range(2)]

    def attend(j, g, s_pair, anchors):
        rows = slice(j * BLOCK, (j + 1) * BLOCK)
        band = slice(j * BLOCK, (j + 2) * BLOCK)
        allowed, delta_f = band_terms(j)
        v_t = vt_ref[g * HEAD_DIM:(g + 1) * HEAD_DIM, band]
        out_t = []
        for a in range(2):
            s = s_pair[a]
            e_parts = []
            l_parts = []
            for r in range(2):
                head = 2 * (2 * g + r) + a
                sr = s[:, r * BLOCK:(r + 1) * BLOCK]
                sr = jnp.where(allowed, sr - ALIBI_SLOPES[head] * delta_f, NEG_BIG)
                sink = sinks_ref[0, head]
                m = jnp.maximum(jnp.max(sr, axis=0, keepdims=True), sink)
                e = jnp.exp(sr - m)
                l_parts.append(jnp.sum(e, axis=0, keepdims=True) + jnp.exp(sink - m))
                e_parts.append(e.astype(_BF16))
            p_a = jnp.concatenate(e_parts, axis=1)
            if a < len(anchors):
                p_a = _after(p_a, anchors[a])
            out_t.append(_dot(v_t, p_a) * (1.0 / jnp.concatenate(l_parts, axis=1)))
        for r in range(2):
            pair_t = jnp.concatenate([out_t[0][:, r * BLOCK:(r + 1) * BLOCK],
                                      out_t[1][:, r * BLOCK:(r + 1) * BLOCK]], axis=0)
            attn_ref[rows, (2 * g + r) * LANES:(2 * g + r + 1) * LANES] = pair_t.T

    n_uc = (_QX0 - _UC0) // MXU_COLS
    attn_todo = [(j, g) for j in range(n_blk) for g in range(N_KV_HEADS)]
    s_all = [scores(j, g) for j, g in attn_todo]
    uc = [_dot(_after(h, s_all[i % len(attn_todo)][0]), w_in_ref[:, _UC0 + i * MXU_COLS:_UC0 + (i + 1) * MXU_COLS])
          for i in range(n_uc)]
    for i, (j, g) in enumerate(attn_todo):
        attend(j, g, s_all[i], uc[i::len(attn_todo)])

    kmask_ref[:, 0:BLOCK, :] = kmask_ref[:, ts:ts + BLOCK, :]
    vt_ref[:, 0:BLOCK] = vt_ref[:, ts:ts + BLOCK]
    kpos_ref[...] = pos_ref[0, n_blk - 1]

    n_glu = CONV_W // MXU_COLS
    for c in range(n_glu):
        a_c = ((uc[c] + b_glu_ref[:, cols(c)])
               * _sigmoid(uc[n_glu + c] + b_glu_ref[:, CONV_W + c * MXU_COLS:CONV_W + (c + 1) * MXU_COLS]))
        for i in range(MXU_COLS // LANES):
            conv_ref[c * (MXU_COLS // LANES) + i, CONV_CARRY:, :] = a_c[:, i * LANES:(i + 1) * LANES]
    g_conv = jnp.concatenate(uc[2 * n_glu:], axis=1)

    first_tap_row = CONV_CARRY - (CONV_K - 1)
    conv_out = {}

    def conv_piece(lc, r0):
        lanes = slice(lc * LANES, (lc + 1) * LANES)
        acc = jnp.broadcast_to(b_dw_ref[:, lanes], (CONV_ROWS, LANES))
        for t in range(CONV_K):
            off = r0 + first_tap_row + t
            acc = acc + w_dw_ref[t:t + 1, lanes] * conv_ref[lc, off:off + CONV_ROWS, :]
        conv_out[(lc, r0)] = acc

    conv_act = {}

    def conv_norm(r0):
        c = jnp.concatenate([conv_out.pop((lc, r0)) for lc in range(CONV_W // LANES)], axis=1)
        mu = jnp.mean(c, axis=-1, keepdims=True)
        cc = c - mu
        var = jnp.mean(cc * cc, axis=-1, keepdims=True)
        cn = cc * lax.rsqrt(var + LN_EPS) * ln_g_ref[...] + ln_b_ref[...]
        conv_act[r0] = (_silu(cn) * _silu(g_conv[r0:r0 + CONV_ROWS])).astype(_BF16)

    cross_out = {}
    qx = {}

    def cross_head(hh):
        per_chunk = MXU_COLS // CROSS_HEAD_DIM
        qh = qx[hh // per_chunk][:, (hh % per_chunk) * CROSS_HEAD_DIM:(hh % per_chunk + 1) * CROSS_HEAD_DIM]
        kv_cols = slice(hh * CROSS_HEAD_DIM, (hh + 1) * CROSS_HEAD_DIM)
        s = _dot_nt(qh.astype(_BF16), km_ref[0, :, kv_cols]) * (1.0 / math.sqrt(CROSS_HEAD_DIM))
        m = jnp.max(s, axis=1, keepdims=True)
        e = jnp.exp(s - m)
        l = jnp.sum(e, axis=1, keepdims=True)
        cross_out[hh] = _dot(e.astype(_BF16), vm_ref[0, :, kv_cols]) * (1.0 / l)

    vec_todo = []
    for r0 in range(0, ts, CONV_ROWS):
        vec_todo += [(conv_piece, (lc, r0)) for lc in range(CONV_W // LANES)]
        vec_todo.append((conv_norm, (r0,)))
    n_qx = (_MG0 - _QX0) // MXU_COLS
    gates = []
    for i in range(n_qx + 3 * n_col):
        if i < n_qx:
            qx[i] = proj(_QX0 + i * MXU_COLS)
            if i == CROSS_W // MXU_COLS - 1:
                vec_todo[2:2] = [(cross_head, (hh,)) for hh in range(N_CROSS_HEADS)]
        else:
            gates.append(_sigmoid(proj(_MG0 + (i - n_qx) * MXU_COLS)))
        if vec_todo:
            fn, args = vec_todo.pop(0)
            fn(*args)
    while vec_todo:
        fn, args = vec_todo.pop(0)
        fn(*args)
    conv_ref[:, 0:CONV_CARRY, :] = conv_ref[:, ts:ts + CONV_CARRY, :]

    g_cross = jnp.concatenate([qx[i] for i in range(CROSS_W // MXU_COLS, n_qx)], axis=1)
    attn_act = (attn_ref[...] * _silu(g_attn)).astype(_BF16)
    conv_all = jnp.concatenate([conv_act[r0] for r0 in range(0, ts, CONV_ROWS)], axis=0)
    cross_act = (jnp.concatenate([cross_out[hh] for hh in range(N_CROSS_HEADS)], axis=1)
                 * _silu(g_cross)).astype(_BF16)
    merged = []
    for c in range(n_col):
        y_attn = _dot(attn_act, w_oa_ref[:, cols(c)])
        y_conv = _dot(conv_all, w_pw_ref[:, cols(c)]) + b_pw_ref[:, cols(c)]
        y_cross = _dot(cross_act, w_oc_ref[:, cols(c)])
        merged.append((gates[c] * y_attn + gates[n_col + c] * y_conv
                       + gates[2 * n_col + c] * y_cross).astype(_BF16))
    merged = jnp.concatenate(merged, axis=1)

    x_new = jnp.concatenate([x[:, cols(c)] + _dot(merged, w_out_ref[:, cols(c)]) for c in range(n_col)],
                            axis=1)
    out_ref[0] = _rmsnorm(x_new, fng_ref[...])


def _const_spec(shape, block_index=None):
    index = tuple(block_index) if block_index is not None else (0,) * len(shape)
    return pl.BlockSpec(shape, lambda b, s: index, pipeline_mode=pl.Buffered(1))


@jax.jit
def kernel(x, mem, positions, norm_g, w_in, attn_sinks, w_o_attn, b_glu, w_dw, b_dw, ln_g, ln_b,
           w_pw, b_pw, mem_norm_g, w_mem_kv, w_o_cross, w_out, final_norm_g):
    batch, seq, d_model = x.shape
    assert d_model == D_MODEL and norm_g.shape[0] == 1 and w_in.shape == (1, D_MODEL, IN_W)
    assert seq % SEQ_TILE == 0 and SEQ_TILE % BLOCK == 0 and mem.shape[1] == MEM_LEN
    ts = SEQ_TILE
    n_blk_tile = ts // BLOCK

    w_stack = jnp.concatenate([w_mem_kv[0], w_out[0], w_o_attn[0], w_pw[0], w_o_cross[0]],
                              axis=0).astype(_BF16)

    km, vm = pl.pallas_call(
        _mem_kv_kernel,
        grid=(batch,),
        in_specs=[pl.BlockSpec((1, MEM_LEN, D_MODEL), lambda b: (b, 0, 0)),
                  pl.BlockSpec((1, D_MODEL), lambda b: (0, 0)),
                  pl.BlockSpec((D_MODEL, 2 * CROSS_W), lambda b: (_W_MEM0 // D_MODEL, 0))],
        out_specs=[pl.BlockSpec((1, MEM_LEN, CROSS_W), lambda b: (b, 0, 0)),
                   pl.BlockSpec((1, MEM_LEN, CROSS_W), lambda b: (b, 0, 0))],
        out_shape=[jax.ShapeDtypeStruct((batch, MEM_LEN, CROSS_W), _BF16)] * 2,
        name="mem_kv",
    )(mem, mem_norm_g, w_stack)

    in_specs = [
        pl.BlockSpec(memory_space=pltpu.SMEM),
        pl.BlockSpec((1, ts, D_MODEL), lambda b, s: (b, s, 0)),
        pl.BlockSpec((1, n_blk_tile, 1, BLOCK), lambda b, s: (b, s, 0, 0)),
        pl.BlockSpec((1, MEM_LEN, CROSS_W), lambda b, s: (b, 0, 0)),
        pl.BlockSpec((1, MEM_LEN, CROSS_W), lambda b, s: (b, 0, 0)),
        _const_spec((1, D_MODEL)),
        _const_spec((D_MODEL, IN_W)),
        _const_spec((D_MODEL, D_MODEL), (_W_OUT0 // D_MODEL, 0)),
        _const_spec((ATTN_W, D_MODEL), (_W_OA0 // ATTN_W, 0)),
        _const_spec((CONV_W, D_MODEL), (_W_PW0 // CONV_W, 0)),
        _const_spec((CROSS_W, D_MODEL), (_W_OC0 // CROSS_W, 0)),
        _const_spec((1, 2 * CONV_W)),
        _const_spec((CONV_K, CONV_W)),
        _const_spec((1, CONV_W)),
        _const_spec((1, CONV_W)),
        _const_spec((1, CONV_W)),
        _const_spec((1, D_MODEL)),
        _const_spec((1, D_MODEL)),
    ]
    return pl.pallas_call(
        _trunk_kernel,
        grid=(batch, seq // ts),
        in_specs=in_specs,
        out_specs=pl.BlockSpec((1, ts, D_MODEL), lambda b, s: (b, s, 0)),
        out_shape=jax.ShapeDtypeStruct((batch, seq, D_MODEL), x.dtype),
        scratch_shapes=[
            pltpu.VMEM((4, ts + BLOCK, KV_W), _BF16),
            pltpu.VMEM((KV_W, ts + BLOCK), _BF16),
            pltpu.VMEM((1, BLOCK), jnp.int32),
            pltpu.VMEM((ts, ATTN_W), _F32),
            pltpu.VMEM((CONV_W // LANES, ts + CONV_CARRY, LANES), _F32),
        ],
        compiler_params=pltpu.CompilerParams(
            dimension_semantics=("arbitrary", "arbitrary"),
            vmem_limit_bytes=VMEM_LIMIT_BYTES),
        name="trunk",
    )(attn_sinks, x, positions.reshape(batch, seq // BLOCK, 1, BLOCK), km, vm, norm_g,
      w_in[0].astype(_BF16), w_stack, w_stack, w_stack, w_stack, b_glu,
      w_dw.reshape(CONV_K, CONV_W), b_dw, ln_g, ln_b, b_pw, final_norm_g.reshape(1, D_MODEL))
```

```python
import math

import jax
import jax.numpy as jnp
from jax import lax
from jax.experimental import pallas as pl
from jax.experimental.pallas import tpu as pltpu

D_MODEL = 1024
MEM_LEN = 256
HEAD_DIM = 64
N_Q_HEADS = 8
N_KV_HEADS = 2
GROUP = N_Q_HEADS // N_KV_HEADS
WINDOW = 128
BLOCK = 128
ATTN_W = N_Q_HEADS * HEAD_DIM
KV_W = N_KV_HEADS * HEAD_DIM
CONV_W = D_MODEL // 2
CONV_K = 31
N_CROSS_HEADS = 4
CROSS_HEAD_DIM = 128
CROSS_W = N_CROSS_HEADS * CROSS_HEAD_DIM
RMS_EPS = 1e-6
LN_EPS = 1e-5
NEG_BIG = -1e30
ALIBI_SLOPES = tuple(2.0 ** (-8.0 * (i + 1) / N_Q_HEADS) for i in range(N_Q_HEADS))

_Q0 = 0
_K0 = _Q0 + ATTN_W
_V0 = _K0 + KV_W
_GA0 = _V0 + KV_W
_UC0 = _GA0 + ATTN_W
_GC0 = _UC0 + 2 * CONV_W
_QX0 = _GC0 + CONV_W
_GX0 = _QX0 + CROSS_W
_MG0 = _GX0 + CROSS_W
IN_W = _MG0 + 3 * D_MODEL

LANES = 128
MXU_COLS = 256
SEQ_TILE = 512
CONV_CARRY = 32
CONV_ROWS = 128
LOAD_ROWS = 32
VMEM_LIMIT_BYTES = 56 * 1024 * 1024

_BF16 = jnp.bfloat16
_F32 = jnp.float32


def _dot(a, b):
    return jnp.dot(a, b, preferred_element_type=_F32)


def _dot_nt(a, b):
    return lax.dot_general(a, b, (((1,), (1,)), ((), ())), preferred_element_type=_F32)


def _rmsnorm(xf, g):
    y = xf * lax.rsqrt(jnp.mean(xf * xf, axis=-1, keepdims=True) + RMS_EPS)
    return y * g


def _sigmoid(x):
    return jax.nn.sigmoid(x)


def _silu(x):
    return x * jax.nn.sigmoid(x)


def _after(value, anchor):
    rows = 32 // value.dtype.itemsize
    never = pl.program_id(0) < 0
    head = jnp.where(never, anchor[:rows, :LANES].astype(value.dtype), value[:rows, :LANES])
    top = jnp.concatenate([head, value[:rows, LANES:]], axis=1)
    return jnp.concatenate([top, value[rows:]], axis=0)


def _mem_kv_kernel(mem_ref, g_ref, w_ref, km_ref, vm_ref):
    mem_n = _rmsnorm(mem_ref[0], g_ref[...]).astype(_BF16)
    kv = _dot(mem_n, w_ref[0].astype(_BF16))
    km_ref[0] = kv[:, :CROSS_W].astype(_BF16)
    vm_ref[0] = kv[:, CROSS_W:].astype(_BF16)


def _load_as_bf16(src_hbm, dst_ref, stage_ref, sem_ref):
    rows = stage_ref.shape[1]
    n_chunks = src_hbm.shape[0] // rows

    def chunk_copy(i, slot):
        return pltpu.make_async_copy(src_hbm.at[pl.ds(i * rows, rows)], stage_ref.at[slot], sem_ref.at[slot])

    chunk_copy(0, 0).start()

    def body(i, carry):
        slot = i % 2
        chunk_copy(i, slot).wait()

        @pl.when(i + 1 < n_chunks)
        def _():
            chunk_copy(i + 1, 1 - slot).start()

        dst_ref[pl.ds(pl.multiple_of(i * rows, rows), rows), :] = stage_ref[slot].astype(_BF16)
        return carry

    lax.fori_loop(0, n_chunks, body, 0)


def _trunk_kernel(sinks_ref, x_ref, pos_ref, km_ref, vm_ref, norm_g_ref, w_in_hbm,
                  w_out_hbm, w_oa_hbm, w_pw_hbm, w_oc_hbm, b_glu_ref, w_dw_ref, b_dw_ref,
                  ln_g_ref, ln_b_ref, b_pw_ref, fng_ref, out_ref,
                  kmask_ref, vmask_ref, kpos_ref, attn_ref, conv_ref,
                  w_in_ref, w_out_ref, w_oa_ref, w_pw_ref, w_oc_ref, stage_in_ref, stage_ref, load_sem):
    ts = x_ref.shape[1]
    n_blk = ts // BLOCK
    n_col = D_MODEL // MXU_COLS
    seq_step = pl.program_id(1)

    @pl.when((pl.program_id(0) == 0) & (seq_step == 0))
    def _():
        _load_as_bf16(w_in_hbm, w_in_ref, stage_in_ref, load_sem)
        for src, dst in ((w_out_hbm, w_out_ref), (w_oa_hbm, w_oa_ref),
                         (w_pw_hbm, w_pw_ref), (w_oc_hbm, w_oc_ref)):
            _load_as_bf16(src, dst, stage_ref, load_sem)

    @pl.when(seq_step == 0)
    def _():
        kmask_ref[:, 0:BLOCK, :] = jnp.zeros((4, BLOCK, KV_W), _BF16)
        vmask_ref[:, 0:BLOCK, :] = jnp.zeros((4, BLOCK, KV_W), _BF16)
        conv_ref[:, 0:CONV_CARRY, :] = jnp.zeros((CONV_W // LANES, CONV_CARRY, LANES), _F32)
        kpos_ref[...] = jnp.broadcast_to(pos_ref[0, 0][:, 0:1] - (WINDOW + 1), (1, BLOCK))

    x = x_ref[0]
    h = _rmsnorm(x, norm_g_ref[...]).astype(_BF16)

    def proj(c0, width=MXU_COLS):
        return _dot(h, w_in_ref[:, c0:c0 + width])

    def cols(c):
        return slice(c * MXU_COLS, (c + 1) * MXU_COLS)

    qkvg = proj(_Q0, _UC0 - _Q0)
    q = (qkvg[:, _Q0:_K0] * (1.0 / math.sqrt(HEAD_DIM))).astype(_BF16)
    k = qkvg[:, _K0:_V0]
    v = qkvg[:, _V0:_GA0]
    g_attn = qkvg[:, _GA0:_UC0]

    lo_half = lax.broadcasted_iota(jnp.int32, (ts, KV_W), 1) < HEAD_DIM
    zero = jnp.zeros((ts, KV_W), _F32)
    for src, dst_ref in ((k, kmask_ref), (v, vmask_ref)):
        rolled = pltpu.roll(src, HEAD_DIM, 1)
        dst_ref[0, BLOCK:, :] = jnp.where(lo_half, src, zero).astype(_BF16)
        dst_ref[1, BLOCK:, :] = jnp.where(lo_half, zero, rolled).astype(_BF16)
        dst_ref[2, BLOCK:, :] = jnp.where(lo_half, rolled, zero).astype(_BF16)
        dst_ref[3, BLOCK:, :] = jnp.where(lo_half, zero, src).astype(_BF16)

    lo_out = lax.broadcasted_iota(jnp.int32, (2 * BLOCK, 2 * HEAD_DIM), 1) < HEAD_DIM
    band_cache = {}

    def band_terms(j):
        if j not in band_cache:
            qcol = jnp.broadcast_to(pos_ref[0, j], (BLOCK, BLOCK)).T
            prev_pos = kpos_ref[...] if j == 0 else pos_ref[0, j - 1]
            delta = jnp.concatenate([qcol - prev_pos, qcol - pos_ref[0, j]], axis=1)
            band_cache[j] = ((delta >= 0) & (delta < WINDOW), delta.astype(_F32))
        return band_cache[j]

    def scores(j, g):
        rows = slice(j * BLOCK, (j + 1) * BLOCK)
        band = slice(j * BLOCK, (j + 2) * BLOCK)
        qg = jnp.concatenate([q[rows, (2 * g) * LANES:(2 * g + 1) * LANES],
                              q[rows, (2 * g + 1) * LANES:(2 * g + 2) * LANES]], axis=0)
        return [_dot_nt(qg, kmask_ref[2 * g + a, band, :]) for a in range(2)]

    def attend(j, g, s_pair, anchors):
        rows = slice(j * BLOCK, (j + 1) * BLOCK)
        band = slice(j * BLOCK, (j + 2) * BLOCK)
        allowed, delta_f = band_terms(j)
        probs = []
        inv_l = []
        for a in range(2):
            s = s_pair[a]
            e_parts = []
            l_parts = []
            for r in range(2):
                head = 2 * (2 * g + r) + a
                sr = s[r * BLOCK:(r + 1) * BLOCK]
                sr = jnp.where(allowed, sr - ALIBI_SLOPES[head] * delta_f, NEG_BIG)
                sink = sinks_ref[0, head]
                m = jnp.maximum(jnp.max(sr, axis=1, keepdims=True), sink)
                e = jnp.exp(sr - m)
                l_parts.append(jnp.sum(e, axis=1, keepdims=True) + jnp.exp(sink - m))
                e_parts.append(e.astype(_BF16))
            p_a = jnp.concatenate(e_parts, axis=0)
            if a < len(anchors):
                p_a = _after(p_a, anchors[a])
            probs.append(p_a)
            inv_l.append(1.0 / jnp.concatenate(l_parts, axis=0))
        o = (_dot(probs[0], vmask_ref[2 * g, band, :])
             + _dot(probs[1], vmask_ref[2 * g + 1, band, :]))
        o = o * jnp.where(lo_out, inv_l[0], inv_l[1])
        attn_ref[rows, (2 * g) * LANES:(2 * g + 1) * LANES] = o[:BLOCK]
        attn_ref[rows, (2 * g + 1) * LANES:(2 * g + 2) * LANES] = o[BLOCK:]

    n_uc = (_QX0 - _UC0) // MXU_COLS
    attn_todo = [(j, g) for j in range(n_blk) for g in range(N_KV_HEADS)]
    s_all = [scores(j, g) for j, g in attn_todo]
    uc = [_dot(_after(h, s_all[i % len(attn_todo)][0]), w_in_ref[:, _UC0 + i * MXU_COLS:_UC0 + (i + 1) * MXU_COLS])
          for i in range(n_uc)]
    for i, (j, g) in enumerate(attn_todo):
        attend(j, g, s_all[i], uc[i::len(attn_todo)])

    kmask_ref[:, 0:BLOCK, :] = kmask_ref[:, ts:ts + BLOCK, :]
    vmask_ref[:, 0:BLOCK, :] = vmask_ref[:, ts:ts + BLOCK, :]
    kpos_ref[...] = pos_ref[0, n_blk - 1]

    n_glu = CONV_W // MXU_COLS
    for c in range(n_glu):
        a_c = ((uc[c] + b_glu_ref[:, cols(c)])
               * _sigmoid(uc[n_glu + c] + b_glu_ref[:, CONV_W + c * MXU_COLS:CONV_W + (c + 1) * MXU_COLS]))
        for i in range(MXU_COLS // LANES):
            conv_ref[c * (MXU_COLS // LANES) + i, CONV_CARRY:, :] = a_c[:, i * LANES:(i + 1) * LANES]
    g_conv = jnp.concatenate(uc[2 * n_glu:], axis=1)

    first_tap_row = CONV_CARRY - (CONV_K - 1)
    conv_out = {}

    def conv_piece(lc, r0):
        lanes = slice(lc * LANES, (lc + 1) * LANES)
        acc = jnp.broadcast_to(b_dw_ref[:, lanes], (CONV_ROWS, LANES))
        for t in range(CONV_K):
            off = r0 + first_tap_row + t
            acc = acc + w_dw_ref[t:t + 1, lanes] * conv_ref[lc, off:off + CONV_ROWS, :]
        conv_out[(lc, r0)] = acc

    conv_act = {}

    def conv_norm(r0):
        c = jnp.concatenate([conv_out.pop((lc, r0)) for lc in range(CONV_W // LANES)], axis=1)
        mu = jnp.mean(c, axis=-1, keepdims=True)
        cc = c - mu
        var = jnp.mean(cc * cc, axis=-1, keepdims=True)
        cn = cc * lax.rsqrt(var + LN_EPS) * ln_g_ref[...] + ln_b_ref[...]
        conv_act[r0] = (_silu(cn) * _silu(g_conv[r0:r0 + CONV_ROWS])).astype(_BF16)

    cross_out = {}
    qx = {}

    def cross_head(hh):
        per_chunk = MXU_COLS // CROSS_HEAD_DIM
        qh = qx[hh // per_chunk][:, (hh % per_chunk) * CROSS_HEAD_DIM:(hh % per_chunk + 1) * CROSS_HEAD_DIM]
        kv_cols = slice(hh * CROSS_HEAD_DIM, (hh + 1) * CROSS_HEAD_DIM)
        s = _dot_nt(qh.astype(_BF16), km_ref[0, :, kv_cols]) * (1.0 / math.sqrt(CROSS_HEAD_DIM))
        m = jnp.max(s, axis=1, keepdims=True)
        e = jnp.exp(s - m)
        l = jnp.sum(e, axis=1, keepdims=True)
        cross_out[hh] = _dot(e.astype(_BF16), vm_ref[0, :, kv_cols]) * (1.0 / l)

    vec_todo = []
    for r0 in range(0, ts, CONV_ROWS):
        vec_todo += [(conv_piece, (lc, r0)) for lc in range(CONV_W // LANES)]
        vec_todo.append((conv_norm, (r0,)))
    n_qx = (_MG0 - _QX0) // MXU_COLS
    gates = []
    for i in range(n_qx + 3 * n_col):
        if i < n_qx:
            qx[i] = proj(_QX0 + i * MXU_COLS)
            if i == CROSS_W // MXU_COLS - 1:
                vec_todo[2:2] = [(cross_head, (hh,)) for hh in range(N_CROSS_HEADS)]
        else:
            gates.append(_sigmoid(proj(_MG0 + (i - n_qx) * MXU_COLS)))
        if vec_todo:
            fn, args = vec_todo.pop(0)
            fn(*args)
    while vec_todo:
        fn, args = vec_todo.pop(0)
        fn(*args)
    conv_ref[:, 0:CONV_CARRY, :] = conv_ref[:, ts:ts + CONV_CARRY, :]

    g_cross = jnp.concatenate([qx[i] for i in range(CROSS_W // MXU_COLS, n_qx)], axis=1)
    attn_act = (attn_ref[...] * _silu(g_attn)).astype(_BF16)
    conv_all = jnp.concatenate([conv_act[r0] for r0 in range(0, ts, CONV_ROWS)], axis=0)
    cross_act = (jnp.concatenate([cross_out[hh] for hh in range(N_CROSS_HEADS)], axis=1)
                 * _silu(g_cross)).astype(_BF16)
    merged = []
    for c in range(n_col):
        y_attn = _dot(attn_act, w_oa_ref[:, cols(c)])
        y_conv = _dot(conv_all, w_pw_ref[:, cols(c)]) + b_pw_ref[:, cols(c)]
        y_cross = _dot(cross_act, w_oc_ref[:, cols(c)])
        merged.append((gates[c] * y_attn + gates[n_col + c] * y_conv
                       + gates[2 * n_col + c] * y_cross).astype(_BF16))
    merged = jnp.concatenate(merged, axis=1)

    x_new = jnp.concatenate([x[:, cols(c)] + _dot(merged, w_out_ref[:, cols(c)]) for c in range(n_col)],
                            axis=1)
    out_ref[0] = _rmsnorm(x_new, fng_ref[...])


def _const_spec(shape):
    nd = len(shape)
    return pl.BlockSpec(shape, lambda b, s: (0,) * nd, pipeline_mode=pl.Buffered(1))


@jax.jit
def kernel(x, mem, positions, norm_g, w_in, attn_sinks, w_o_attn, b_glu, w_dw, b_dw, ln_g, ln_b,
           w_pw, b_pw, mem_norm_g, w_mem_kv, w_o_cross, w_out, final_norm_g):
    batch, seq, d_model = x.shape
    assert d_model == D_MODEL and norm_g.shape[0] == 1 and w_in.shape == (1, D_MODEL, IN_W)
    assert seq % SEQ_TILE == 0 and SEQ_TILE % BLOCK == 0 and mem.shape[1] == MEM_LEN
    ts = SEQ_TILE
    n_blk_tile = ts // BLOCK

    km, vm = pl.pallas_call(
        _mem_kv_kernel,
        grid=(batch,),
        in_specs=[pl.BlockSpec((1, MEM_LEN, D_MODEL), lambda b: (b, 0, 0)),
                  pl.BlockSpec((1, D_MODEL), lambda b: (0, 0)),
                  pl.BlockSpec((1, D_MODEL, 2 * CROSS_W), lambda b: (0, 0, 0))],
        out_specs=[pl.BlockSpec((1, MEM_LEN, CROSS_W), lambda b: (b, 0, 0)),
                   pl.BlockSpec((1, MEM_LEN, CROSS_W), lambda b: (b, 0, 0))],
        out_shape=[jax.ShapeDtypeStruct((batch, MEM_LEN, CROSS_W), _BF16)] * 2,
        name="mem_kv",
    )(mem, mem_norm_g, w_mem_kv)

    hbm = pl.BlockSpec(memory_space=pl.ANY)
    in_specs = [
        pl.BlockSpec(memory_space=pltpu.SMEM),
        pl.BlockSpec((1, ts, D_MODEL), lambda b, s: (b, s, 0)),
        pl.BlockSpec((1, n_blk_tile, 1, BLOCK), lambda b, s: (b, s, 0, 0)),
        pl.BlockSpec((1, MEM_LEN, CROSS_W), lambda b, s: (b, 0, 0)),
        pl.BlockSpec((1, MEM_LEN, CROSS_W), lambda b, s: (b, 0, 0)),
        _const_spec((1, D_MODEL)),
        hbm, hbm, hbm, hbm, hbm,
        _const_spec((1, 2 * CONV_W)),
        _const_spec((CONV_K, CONV_W)),
        _const_spec((1, CONV_W)),
        _const_spec((1, CONV_W)),
        _const_spec((1, CONV_W)),
        _const_spec((1, D_MODEL)),
        _const_spec((1, D_MODEL)),
    ]
    return pl.pallas_call(
        _trunk_kernel,
        grid=(batch, seq // ts),
        in_specs=in_specs,
        out_specs=pl.BlockSpec((1, ts, D_MODEL), lambda b, s: (b, s, 0)),
        out_shape=jax.ShapeDtypeStruct((batch, seq, D_MODEL), x.dtype),
        scratch_shapes=[
            pltpu.VMEM((4, ts + BLOCK, KV_W), _BF16),
            pltpu.VMEM((4, ts + BLOCK, KV_W), _BF16),
            pltpu.VMEM((1, BLOCK), jnp.int32),
            pltpu.VMEM((ts, ATTN_W), _F32),
            pltpu.VMEM((CONV_W // LANES, ts + CONV_CARRY, LANES), _F32),
            pltpu.VMEM((D_MODEL, IN_W), _BF16),
            pltpu.VMEM((D_MODEL, D_MODEL), _BF16),
            pltpu.VMEM((ATTN_W, D_MODEL), _BF16),
            pltpu.VMEM((CONV_W, D_MODEL), _BF16),
            pltpu.VMEM((CROSS_W, D_MODEL), _BF16),
            pltpu.VMEM((2, LOAD_ROWS, IN_W), _F32),
            pltpu.VMEM((2, LOAD_ROWS, D_MODEL), _F32),
            pltpu.SemaphoreType.DMA((2,)),
        ],
        compiler_params=pltpu.CompilerParams(
            dimension_semantics=("arbitrary", "arbitrary"),
            vmem_limit_bytes=VMEM_LIMIT_BYTES),
        name="trunk",
    )(attn_sinks, x, positions.reshape(batch, seq // BLOCK, 1, BLOCK), km, vm, norm_g,
      w_in[0], w_out[0], w_o_attn[0], w_pw[0], w_o_cross[0], b_glu,
      w_dw.reshape(CONV_K, CONV_W), b_dw, ln_g, ln_b, b_pw, final_norm_g.reshape(1, D_MODEL))
```

```python
import math

import jax
import jax.numpy as jnp
from jax import lax
from jax.experimental import pallas as pl
from jax.experimental.pallas import tpu as pltpu

D_MODEL = 1024
MEM_LEN = 256
HEAD_DIM = 64
N_Q_HEADS = 8
N_KV_HEADS = 2
GROUP = N_Q_HEADS // N_KV_HEADS
WINDOW = 128
BLOCK = 128
ATTN_W = N_Q_HEADS * HEAD_DIM
KV_W = N_KV_HEADS * HEAD_DIM
CONV_W = D_MODEL // 2
CONV_K = 31
N_CROSS_HEADS = 4
CROSS_HEAD_DIM = 128
CROSS_W = N_CROSS_HEADS * CROSS_HEAD_DIM
RMS_EPS = 1e-6
LN_EPS = 1e-5
NEG_BIG = -1e30
ALIBI_SLOPES = tuple(2.0 ** (-8.0 * (i + 1) / N_Q_HEADS) for i in range(N_Q_HEADS))

_Q0 = 0
_K0 = _Q0 + ATTN_W
_V0 = _K0 + KV_W
_GA0 = _V0 + KV_W
_UC0 = _GA0 + ATTN_W
_GC0 = _UC0 + 2 * CONV_W
_QX0 = _GC0 + CONV_W
_GX0 = _QX0 + CROSS_W
_MG0 = _GX0 + CROSS_W
IN_W = _MG0 + 3 * D_MODEL

LANES = 128
MXU_COLS = 256
SEQ_TILE = 512
CONV_CARRY = 32
CONV_ROWS = 128
LOAD_SLOTS = 6
LOAD_ROWS = 32
LOAD_ROWS_SMALL = 128
VMEM_LIMIT_BYTES = 56 * 1024 * 1024

_BF16 = jnp.bfloat16
_F32 = jnp.float32


def _dot(a, b):
    return jnp.dot(a, b, preferred_element_type=_F32)


def _dot_nt(a, b):
    return lax.dot_general(a, b, (((1,), (1,)), ((), ())), preferred_element_type=_F32)


def _rmsnorm(xf, g):
    y = xf * lax.rsqrt(jnp.mean(xf * xf, axis=-1, keepdims=True) + RMS_EPS)
    return y * g


def _sigmoid(x):
    return jax.nn.sigmoid(x)


def _silu(x):
    return x * jax.nn.sigmoid(x)


def _after(value, anchor):
    rows = 32 // value.dtype.itemsize
    never = pl.program_id(0) < 0
    head = jnp.where(never, anchor[:rows, :LANES].astype(value.dtype), value[:rows, :LANES])
    top = jnp.concatenate([head, value[:rows, LANES:]], axis=1)
    return jnp.concatenate([top, value[rows:]], axis=0)


def _mem_kv_kernel(mem_ref, g_ref, w_ref, km_ref, vm_ref):
    mem_n = _rmsnorm(mem_ref[0], g_ref[...]).astype(_BF16)
    kv = _dot(mem_n, w_ref[0].astype(_BF16))
    km_ref[0] = kv[:, :CROSS_W].astype(_BF16)
    vm_ref[0] = kv[:, CROSS_W:].astype(_BF16)


def _load_as_bf16(src_hbm, dst_ref, stage_ref, sem_ref):
    n_slots, rows = stage_ref.shape[0], stage_ref.shape[1]
    n_chunks = src_hbm.shape[0] // rows

    def chunk_copy(i, slot):
        return pltpu.make_async_copy(src_hbm.at[pl.ds(i * rows, rows)], stage_ref.at[slot], sem_ref.at[slot])

    for i in range(min(n_slots, n_chunks)):
        chunk_copy(i, i).start()

    def body(i, carry):
        slot = i % n_slots
        chunk_copy(i, slot).wait()
        dst_ref[pl.ds(pl.multiple_of(i * rows, rows), rows), :] = stage_ref[slot].astype(_BF16)

        @pl.when(i + n_slots < n_chunks)
        def _():
            chunk_copy(i + n_slots, slot).start()

        return carry

    lax.fori_loop(0, n_chunks, body, 0)


def _trunk_kernel(sinks_ref, x_ref, pos_ref, km_ref, vm_ref, norm_g_ref, w_in_hbm,
                  w_out_hbm, w_oa_hbm, w_pw_hbm, w_oc_hbm, b_glu_ref, w_dw_ref, b_dw_ref,
                  ln_g_ref, ln_b_ref, b_pw_ref, fng_ref, out_ref,
                  kmask_ref, vmask_ref, kpos_ref, attn_ref, conv_ref,
                  w_in_ref, w_out_ref, w_oa_ref, w_pw_ref, w_oc_ref, stage_in_ref, stage_ref, load_sem):
    ts = x_ref.shape[1]
    n_blk = ts // BLOCK
    n_col = D_MODEL // MXU_COLS
    seq_step = pl.program_id(1)

    @pl.when((pl.program_id(0) == 0) & (seq_step == 0))
    def _():
        _load_as_bf16(w_in_hbm, w_in_ref, stage_in_ref, load_sem)
        for src, dst in ((w_out_hbm, w_out_ref), (w_oa_hbm, w_oa_ref),
                         (w_pw_hbm, w_pw_ref), (w_oc_hbm, w_oc_ref)):
            _load_as_bf16(src, dst, stage_ref, load_sem)

    @pl.when(seq_step == 0)
    def _():
        kmask_ref[:, 0:BLOCK, :] = jnp.zeros((4, BLOCK, KV_W), _BF16)
        vmask_ref[:, 0:BLOCK, :] = jnp.zeros((4, BLOCK, KV_W), _BF16)
        conv_ref[:, 0:CONV_CARRY, :] = jnp.zeros((CONV_W // LANES, CONV_CARRY, LANES), _F32)
        kpos_ref[...] = jnp.broadcast_to(pos_ref[0, 0][:, 0:1] - (WINDOW + 1), (1, BLOCK))

    x = x_ref[0]
    h = _rmsnorm(x, norm_g_ref[...]).astype(_BF16)

    def proj(c0, width=MXU_COLS):
        return _dot(h, w_in_ref[:, c0:c0 + width])

    def cols(c):
        return slice(c * MXU_COLS, (c + 1) * MXU_COLS)

    qkvg = proj(_Q0, _UC0 - _Q0)
    q = (qkvg[:, _Q0:_K0] * (1.0 / math.sqrt(HEAD_DIM))).astype(_BF16)
    k = qkvg[:, _K0:_V0]
    v = qkvg[:, _V0:_GA0]
    g_attn = qkvg[:, _GA0:_UC0]

    lo_half = lax.broadcasted_iota(jnp.int32, (ts, KV_W), 1) < HEAD_DIM
    zero = jnp.zeros((ts, KV_W), _F32)
    for src, dst_ref in ((k, kmask_ref), (v, vmask_ref)):
        rolled = pltpu.roll(src, HEAD_DIM, 1)
        dst_ref[0, BLOCK:, :] = jnp.where(lo_half, src, zero).astype(_BF16)
        dst_ref[1, BLOCK:, :] = jnp.where(lo_half, zero, rolled).astype(_BF16)
        dst_ref[2, BLOCK:, :] = jnp.where(lo_half, rolled, zero).astype(_BF16)
        dst_ref[3, BLOCK:, :] = jnp.where(lo_half, zero, src).astype(_BF16)

    lo_out = lax.broadcasted_iota(jnp.int32, (2 * BLOCK, 2 * HEAD_DIM), 1) < HEAD_DIM
    band_cache = {}

    def band_terms(j):
        if j not in band_cache:
            qcol = jnp.broadcast_to(pos_ref[0, j], (BLOCK, BLOCK)).T
            prev_pos = kpos_ref[...] if j == 0 else pos_ref[0, j - 1]
            delta = jnp.concatenate([qcol - prev_pos, qcol - pos_ref[0, j]], axis=1)
            band_cache[j] = ((delta >= 0) & (delta < WINDOW), delta.astype(_F32))
        return band_cache[j]

    def scores(j, g):
        rows = slice(j * BLOCK, (j + 1) * BLOCK)
        band = slice(j * BLOCK, (j + 2) * BLOCK)
        qg = jnp.concatenate([q[rows, (2 * g) * LANES:(2 * g + 1) * LANES],
                              q[rows, (2 * g + 1) * LANES:(2 * g + 2) * LANES]], axis=0)
        return [_dot_nt(qg, kmask_ref[2 * g + a, band, :]) for a in range(2)]

    def attend(j, g, s_pair, anchors):
        rows = slice(j * BLOCK, (j + 1) * BLOCK)
        band = slice(j * BLOCK, (j + 2) * BLOCK)
        allowed, delta_f = band_terms(j)
        probs = []
        inv_l = []
        for a in range(2):
            s = s_pair[a]
            e_parts = []
            l_parts = []
            for r in range(2):
                head = 2 * (2 * g + r) + a
                sr = s[r * BLOCK:(r + 1) * BLOCK]
                sr = jnp.where(allowed, sr - ALIBI_SLOPES[head] * delta_f, NEG_BIG)
                sink = sinks_ref[0, head]
                m = jnp.maximum(jnp.max(sr, axis=1, keepdims=True), sink)
                e = jnp.exp(sr - m)
                l_parts.append(jnp.sum(e, axis=1, keepdims=True) + jnp.exp(sink - m))
                e_parts.append(e.astype(_BF16))
            p_a = jnp.concatenate(e_parts, axis=0)
            if a < len(anchors):
                p_a = _after(p_a, anchors[a])
            probs.append(p_a)
            inv_l.append(1.0 / jnp.concatenate(l_parts, axis=0))
        o = (_dot(probs[0], vmask_ref[2 * g, band, :])
             + _dot(probs[1], vmask_ref[2 * g + 1, band, :]))
        o = o * jnp.where(lo_out, inv_l[0], inv_l[1])
        attn_ref[rows, (2 * g) * LANES:(2 * g + 1) * LANES] = o[:BLOCK]
        attn_ref[rows, (2 * g + 1) * LANES:(2 * g + 2) * LANES] = o[BLOCK:]

    n_uc = (_QX0 - _UC0) // MXU_COLS
    attn_todo = [(j, g) for j in range(n_blk) for g in range(N_KV_HEADS)]
    s_all = [scores(j, g) for j, g in attn_todo]
    uc = [_dot(_after(h, s_all[i % len(attn_todo)][0]), w_in_ref[:, _UC0 + i * MXU_COLS:_UC0 + (i + 1) * MXU_COLS])
          for i in range(n_uc)]
    for i, (j, g) in enumerate(attn_todo):
        attend(j, g, s_all[i], uc[i::len(attn_todo)])

    kmask_ref[:, 0:BLOCK, :] = kmask_ref[:, ts:ts + BLOCK, :]
    vmask_ref[:, 0:BLOCK, :] = vmask_ref[:, ts:ts + BLOCK, :]
    kpos_ref[...] = pos_ref[0, n_blk - 1]

    n_glu = CONV_W // MXU_COLS
    for c in range(n_glu):
        a_c = ((uc[c] + b_glu_ref[:, cols(c)])
               * _sigmoid(uc[n_glu + c] + b_glu_ref[:, CONV_W + c * MXU_COLS:CONV_W + (c + 1) * MXU_COLS]))
        for i in range(MXU_COLS // LANES):
            conv_ref[c * (MXU_COLS // LANES) + i, CONV_CARRY:, :] = a_c[:, i * LANES:(i + 1) * LANES]
    g_conv = jnp.concatenate(uc[2 * n_glu:], axis=1)

    first_tap_row = CONV_CARRY - (CONV_K - 1)
    conv_out = {}

    def conv_piece(lc, r0):
        lanes = slice(lc * LANES, (lc + 1) * LANES)
        acc = jnp.broadcast_to(b_dw_ref[:, lanes], (CONV_ROWS, LANES))
        for t in range(CONV_K):
            off = r0 + first_tap_row + t
            acc = acc + w_dw_ref[t:t + 1, lanes] * conv_ref[lc, off:off + CONV_ROWS, :]
        conv_out[(lc, r0)] = acc

    conv_act = {}

    def conv_norm(r0):
        c = jnp.concatenate([conv_out.pop((lc, r0)) for lc in range(CONV_W // LANES)], axis=1)
        mu = jnp.mean(c, axis=-1, keepdims=True)
        cc = c - mu
        var = jnp.mean(cc * cc, axis=-1, keepdims=True)
        cn = cc * lax.rsqrt(var + LN_EPS) * ln_g_ref[...] + ln_b_ref[...]
        conv_act[r0] = (_silu(cn) * _silu(g_conv[r0:r0 + CONV_ROWS])).astype(_BF16)

    cross_out = {}
    qx = {}

    def cross_head(hh):
        per_chunk = MXU_COLS // CROSS_HEAD_DIM
        qh = qx[hh // per_chunk][:, (hh % per_chunk) * CROSS_HEAD_DIM:(hh % per_chunk + 1) * CROSS_HEAD_DIM]
        kv_cols = slice(hh * CROSS_HEAD_DIM, (hh + 1) * CROSS_HEAD_DIM)
        s = _dot_nt(qh.astype(_BF16), km_ref[0, :, kv_cols]) * (1.0 / math.sqrt(CROSS_HEAD_DIM))
        m = jnp.max(s, axis=1, keepdims=True)
        e = jnp.exp(s - m)
        l = jnp.sum(e, axis=1, keepdims=True)
        cross_out[hh] = _dot(e.astype(_BF16), vm_ref[0, :, kv_cols]) * (1.0 / l)

    vec_todo = []
    for r0 in range(0, ts, CONV_ROWS):
        vec_todo += [(conv_piece, (lc, r0)) for lc in range(CONV_W // LANES)]
        vec_todo.append((conv_norm, (r0,)))
    n_qx = (_MG0 - _QX0) // MXU_COLS
    gates = []
    for i in range(n_qx + 3 * n_col):
        if i < n_qx:
            qx[i] = proj(_QX0 + i * MXU_COLS)
            if i == CROSS_W // MXU_COLS - 1:
                vec_todo[2:2] = [(cross_head, (hh,)) for hh in range(N_CROSS_HEADS)]
        else:
            gates.append(_sigmoid(proj(_MG0 + (i - n_qx) * MXU_COLS)))
        if vec_todo:
            fn, args = vec_todo.pop(0)
            fn(*args)
    while vec_todo:
        fn, args = vec_todo.pop(0)
        fn(*args)
    conv_ref[:, 0:CONV_CARRY, :] = conv_ref[:, ts:ts + CONV_CARRY, :]

    g_cross = jnp.concatenate([qx[i] for i in range(CROSS_W // MXU_COLS, n_qx)], axis=1)
    attn_act = (attn_ref[...] * _silu(g_attn)).astype(_BF16)
    conv_all = jnp.concatenate([conv_act[r0] for r0 in range(0, ts, CONV_ROWS)], axis=0)
    cross_act = (jnp.concatenate([cross_out[hh] for hh in range(N_CROSS_HEADS)], axis=1)
                 * _silu(g_cross)).astype(_BF16)
    merged = []
    for c in range(n_col):
        y_attn = _dot(attn_act, w_oa_ref[:, cols(c)])
        y_conv = _dot(conv_all, w_pw_ref[:, cols(c)]) + b_pw_ref[:, cols(c)]
        y_cross = _dot(cross_act, w_oc_ref[:, cols(c)])
        merged.append((gates[c] * y_attn + gates[n_col + c] * y_conv
                       + gates[2 * n_col + c] * y_cross).astype(_BF16))
    merged = jnp.concatenate(merged, axis=1)

    x_new = jnp.concatenate([x[:, cols(c)] + _dot(merged, w_out_ref[:, cols(c)]) for c in range(n_col)],
                            axis=1)
    out_ref[0] = _rmsnorm(x_new, fng_ref[...])


def _const_spec(shape):
    nd = len(shape)
    return pl.BlockSpec(shape, lambda b, s: (0,) * nd, pipeline_mode=pl.Buffered(1))


@jax.jit
def kernel(x, mem, positions, norm_g, w_in, attn_sinks, w_o_attn, b_glu, w_dw, b_dw, ln_g, ln_b,
           w_pw, b_pw, mem_norm_g, w_mem_kv, w_o_cross, w_out, final_norm_g):
    batch, seq, d_model = x.shape
    assert d_model == D_MODEL and norm_g.shape[0] == 1 and w_in.shape == (1, D_MODEL, IN_W)
    assert seq % SEQ_TILE == 0 and SEQ_TILE % BLOCK == 0 and mem.shape[1] == MEM_LEN
    ts = SEQ_TILE
    n_blk_tile = ts // BLOCK

    km, vm = pl.pallas_call(
        _mem_kv_kernel,
        grid=(batch,),
        in_specs=[pl.BlockSpec((1, MEM_LEN, D_MODEL), lambda b: (b, 0, 0)),
                  pl.BlockSpec((1, D_MODEL), lambda b: (0, 0)),
                  pl.BlockSpec((1, D_MODEL, 2 * CROSS_W), lambda b: (0, 0, 0))],
        out_specs=[pl.BlockSpec((1, MEM_LEN, CROSS_W), lambda b: (b, 0, 0)),
                   pl.BlockSpec((1, MEM_LEN, CROSS_W), lambda b: (b, 0, 0))],
        out_shape=[jax.ShapeDtypeStruct((batch, MEM_LEN, CROSS_W), _BF16)] * 2,
        name="mem_kv",
    )(mem, mem_norm_g, w_mem_kv)

    hbm = pl.BlockSpec(memory_space=pl.ANY)
    in_specs = [
        pl.BlockSpec(memory_space=pltpu.SMEM),
        pl.BlockSpec((1, ts, D_MODEL), lambda b, s: (b, s, 0)),
        pl.BlockSpec((1, n_blk_tile, 1, BLOCK), lambda b, s: (b, s, 0, 0)),
        pl.BlockSpec((1, MEM_LEN, CROSS_W), lambda b, s: (b, 0, 0)),
        pl.BlockSpec((1, MEM_LEN, CROSS_W), lambda b, s: (b, 0, 0)),
        _const_spec((1, D_MODEL)),
        hbm, hbm, hbm, hbm, hbm,
        _const_spec((1, 2 * CONV_W)),
        _const_spec((CONV_K, CONV_W)),
        _const_spec((1, CONV_W)),
        _const_spec((1, CONV_W)),
        _const_spec((1, CONV_W)),
        _const_spec((1, D_MODEL)),
        _const_spec((1, D_MODEL)),
    ]
    return pl.pallas_call(
        _trunk_kernel,
        grid=(batch, seq // ts),
        in_specs=in_specs,
        out_specs=pl.BlockSpec((1, ts, D_MODEL), lambda b, s: (b, s, 0)),
        out_shape=jax.ShapeDtypeStruct((batch, seq, D_MODEL), x.dtype),
        scratch_shapes=[
            pltpu.VMEM((4, ts + BLOCK, KV_W), _BF16),
            pltpu.VMEM((4, ts + BLOCK, KV_W), _BF16),
            pltpu.VMEM((1, BLOCK), jnp.int32),
            pltpu.VMEM((ts, ATTN_W), _F32),
            pltpu.VMEM((CONV_W // LANES, ts + CONV_CARRY, LANES), _F32),
            pltpu.VMEM((D_MODEL, IN_W), _BF16),
            pltpu.VMEM((D_MODEL, D_MODEL), _BF16),
            pltpu.VMEM((ATTN_W, D_MODEL), _BF16),
            pltpu.VMEM((CONV_W, D_MODEL), _BF16),
            pltpu.VMEM((CROSS_W, D_MODEL), _BF16),
            pltpu.VMEM((LOAD_SLOTS, LOAD_ROWS, IN_W), _F32),
            pltpu.VMEM((LOAD_SLOTS, LOAD_ROWS_SMALL, D_MODEL), _F32),
            pltpu.SemaphoreType.DMA((LOAD_SLOTS,)),
        ],
        compiler_params=pltpu.CompilerParams(
            dimension_semantics=("arbitrary", "arbitrary"),
            vmem_limit_bytes=VMEM_LIMIT_BYTES),
        name="trunk",
    )(attn_sinks, x, positions.reshape(batch, seq // BLOCK, 1, BLOCK), km, vm, norm_g,
      w_in[0], w_out[0], w_o_attn[0], w_pw[0], w_o_cross[0], b_glu,
      w_dw.reshape(CONV_K, CONV_W), b_dw, ln_g, ln_b, b_pw, final_norm_g.reshape(1, D_MODEL))
```

```python
import math

import jax
import jax.numpy as jnp
from jax import lax
from jax.experimental import pallas as pl
from jax.experimental.pallas import tpu as pltpu

D_MODEL = 1024
MEM_LEN = 256
HEAD_DIM = 64
N_Q_HEADS = 8
N_KV_HEADS = 2
GROUP = N_Q_HEADS // N_KV_HEADS
WINDOW = 128
BLOCK = 128
ATTN_W = N_Q_HEADS * HEAD_DIM
KV_W = N_KV_HEADS * HEAD_DIM
CONV_W = D_MODEL // 2
CONV_K = 31
N_CROSS_HEADS = 4
CROSS_HEAD_DIM = 128
CROSS_W = N_CROSS_HEADS * CROSS_HEAD_DIM
RMS_EPS = 1e-6
LN_EPS = 1e-5
NEG_BIG = -1e30
ALIBI_SLOPES = tuple(2.0 ** (-8.0 * (i + 1) / N_Q_HEADS) for i in range(N_Q_HEADS))

_Q0 = 0
_K0 = _Q0 + ATTN_W
_V0 = _K0 + KV_W
_GA0 = _V0 + KV_W
_UC0 = _GA0 + ATTN_W
_GC0 = _UC0 + 2 * CONV_W
_QX0 = _GC0 + CONV_W
_GX0 = _QX0 + CROSS_W
_MG0 = _GX0 + CROSS_W
IN_W = _MG0 + 3 * D_MODEL

LANES = 128
MXU_COLS = 256
SEQ_TILE = 512
CONV_CARRY = 32
CONV_ROWS = 128
LOAD_SLOTS = 6
LOAD_ROWS = 32
LOAD_ROWS_SMALL = 128
VMEM_LIMIT_BYTES = 56 * 1024 * 1024

_BF16 = jnp.bfloat16
_F32 = jnp.float32


def _dot(a, b):
    return jnp.dot(a, b, preferred_element_type=_F32)


def _dot_nt(a, b):
    return lax.dot_general(a, b, (((1,), (1,)), ((), ())), preferred_element_type=_F32)


def _rmsnorm(xf, g):
    y = xf * lax.rsqrt(jnp.mean(xf * xf, axis=-1, keepdims=True) + RMS_EPS)
    return y * g


def _sigmoid(x):
    return jax.nn.sigmoid(x)


def _silu(x):
    return x * jax.nn.sigmoid(x)


def _after(value, anchor):
    rows = 32 // value.dtype.itemsize
    never = pl.program_id(0) < 0
    head = jnp.where(never, anchor[:rows, :LANES].astype(value.dtype), value[:rows, :LANES])
    top = jnp.concatenate([head, value[:rows, LANES:]], axis=1)
    return jnp.concatenate([top, value[rows:]], axis=0)


def _load_as_bf16(src_hbm, dst_ref, rows):
    n_chunks = src_hbm.shape[0] // rows

    def load(stage_ref, sem_ref):
        def chunk_copy(i, slot):
            return pltpu.make_async_copy(src_hbm.at[pl.ds(i * rows, rows)], stage_ref.at[slot],
                                         sem_ref.at[slot])

        for i in range(min(LOAD_SLOTS, n_chunks)):
            chunk_copy(i, i).start()

        def body(i, carry):
            slot = i % LOAD_SLOTS
            chunk_copy(i, slot).wait()
            dst_ref[pl.ds(pl.multiple_of(i * rows, rows), rows), :] = stage_ref[slot].astype(_BF16)

            @pl.when(i + LOAD_SLOTS < n_chunks)
            def _():
                chunk_copy(i + LOAD_SLOTS, slot).start()

            return carry

        lax.fori_loop(0, n_chunks, body, 0)

    pl.run_scoped(load, pltpu.VMEM((LOAD_SLOTS, rows, src_hbm.shape[1]), _F32),
                  pltpu.SemaphoreType.DMA((LOAD_SLOTS,)))


def _trunk_kernel(sinks_ref, x_ref, pos_ref, mem_ref, mem_g_ref, norm_g_ref, w_in_hbm,
                  w_out_hbm, w_oa_hbm, w_pw_hbm, w_oc_hbm, w_mkv_hbm, b_glu_ref, w_dw_ref, b_dw_ref,
                  ln_g_ref, ln_b_ref, b_pw_ref, fng_ref, out_ref,
                  kmask_ref, vmask_ref, kpos_ref, attn_ref, conv_ref,
                  w_in_ref, w_out_ref, w_oa_ref, w_pw_ref, w_oc_ref, w_mkv_ref, km_ref, vm_ref):
    ts = x_ref.shape[1]
    n_blk = ts // BLOCK
    n_col = D_MODEL // MXU_COLS
    seq_step = pl.program_id(1)

    @pl.when((pl.program_id(0) == 0) & (seq_step == 0))
    def _():
        _load_as_bf16(w_in_hbm, w_in_ref, LOAD_ROWS)
        for src, dst in ((w_mkv_hbm, w_mkv_ref), (w_out_hbm, w_out_ref), (w_oa_hbm, w_oa_ref),
                         (w_pw_hbm, w_pw_ref), (w_oc_hbm, w_oc_ref)):
            _load_as_bf16(src, dst, LOAD_ROWS_SMALL)

    @pl.when(seq_step == 0)
    def _():
        kmask_ref[:, 0:BLOCK, :] = jnp.zeros((4, BLOCK, KV_W), _BF16)
        vmask_ref[:, 0:BLOCK, :] = jnp.zeros((4, BLOCK, KV_W), _BF16)
        conv_ref[:, 0:CONV_CARRY, :] = jnp.zeros((CONV_W // LANES, CONV_CARRY, LANES), _F32)
        kpos_ref[...] = jnp.broadcast_to(pos_ref[0, 0][:, 0:1] - (WINDOW + 1), (1, BLOCK))
        mem_n = _rmsnorm(mem_ref[0], mem_g_ref[...]).astype(_BF16)
        mem_kv = _dot(mem_n, w_mkv_ref[...])
        km_ref[...] = mem_kv[:, :CROSS_W].astype(_BF16)
        vm_ref[...] = mem_kv[:, CROSS_W:].astype(_BF16)

    x = x_ref[0]
    h = _rmsnorm(x, norm_g_ref[...]).astype(_BF16)

    def proj(c0, width=MXU_COLS):
        return _dot(h, w_in_ref[:, c0:c0 + width])

    def cols(c):
        return slice(c * MXU_COLS, (c + 1) * MXU_COLS)

    qkvg = proj(_Q0, _UC0 - _Q0)
    q = (qkvg[:, _Q0:_K0] * (1.0 / math.sqrt(HEAD_DIM))).astype(_BF16)
    k = qkvg[:, _K0:_V0]
    v = qkvg[:, _V0:_GA0]
    g_attn = qkvg[:, _GA0:_UC0]

    lo_half = lax.broadcasted_iota(jnp.int32, (ts, KV_W), 1) < HEAD_DIM
    zero = jnp.zeros((ts, KV_W), _F32)
    for src, dst_ref in ((k, kmask_ref), (v, vmask_ref)):
        rolled = pltpu.roll(src, HEAD_DIM, 1)
        dst_ref[0, BLOCK:, :] = jnp.where(lo_half, src, zero).astype(_BF16)
        dst_ref[1, BLOCK:, :] = jnp.where(lo_half, zero, rolled).astype(_BF16)
        dst_ref[2, BLOCK:, :] = jnp.where(lo_half, rolled, zero).astype(_BF16)
        dst_ref[3, BLOCK:, :] = jnp.where(lo_half, zero, src).astype(_BF16)

    lo_out = lax.broadcasted_iota(jnp.int32, (2 * BLOCK, 2 * HEAD_DIM), 1) < HEAD_DIM
    band_cache = {}

    def band_terms(j):
        if j not in band_cache:
            qcol = jnp.broadcast_to(pos_ref[0, j], (BLOCK, BLOCK)).T
            prev_pos = kpos_ref[...] if j == 0 else pos_ref[0, j - 1]
            delta = jnp.concatenate([qcol - prev_pos, qcol - pos_ref[0, j]], axis=1)
            band_cache[j] = ((delta >= 0) & (delta < WINDOW), delta.astype(_F32))
        return band_cache[j]

    def scores(j, g):
        rows = slice(j * BLOCK, (j + 1) * BLOCK)
        band = slice(j * BLOCK, (j + 2) * BLOCK)
        qg = jnp.concatenate([q[rows, (2 * g) * LANES:(2 * g + 1) * LANES],
                              q[rows, (2 * g + 1) * LANES:(2 * g + 2) * LANES]], axis=0)
        return [_dot_nt(qg, kmask_ref[2 * g + a, band, :]) for a in range(2)]

    def attend(j, g, s_pair, anchors):
        rows = slice(j * BLOCK, (j + 1) * BLOCK)
        band = slice(j * BLOCK, (j + 2) * BLOCK)
        allowed, delta_f = band_terms(j)
        probs = []
        inv_l = []
        for a in range(2):
            s = s_pair[a]
            e_parts = []
            l_parts = []
            for r in range(2):
                head = 2 * (2 * g + r) + a
                sr = s[r * BLOCK:(r + 1) * BLOCK]
                sr = jnp.where(allowed, sr - ALIBI_SLOPES[head] * delta_f, NEG_BIG)
                sink = sinks_ref[0, head]
                m = jnp.maximum(jnp.max(sr, axis=1, keepdims=True), sink)
                e = jnp.exp(sr - m)
                l_parts.append(jnp.sum(e, axis=1, keepdims=True) + jnp.exp(sink - m))
                e_parts.append(e.astype(_BF16))
            p_a = jnp.concatenate(e_parts, axis=0)
            if a < len(anchors):
                p_a = _after(p_a, anchors[a])
            probs.append(p_a)
            inv_l.append(1.0 / jnp.concatenate(l_parts, axis=0))
        o = (_dot(probs[0], vmask_ref[2 * g, band, :])
             + _dot(probs[1], vmask_ref[2 * g + 1, band, :]))
        o = o * jnp.where(lo_out, inv_l[0], inv_l[1])
        attn_ref[rows, (2 * g) * LANES:(2 * g + 1) * LANES] = o[:BLOCK]
        attn_ref[rows, (2 * g + 1) * LANES:(2 * g + 2) * LANES] = o[BLOCK:]

    n_uc = (_QX0 - _UC0) // MXU_COLS
    attn_todo = [(j, g) for j in range(n_blk) for g in range(N_KV_HEADS)]
    s_all = [scores(j, g) for j, g in attn_todo]
    uc = [_dot(_after(h, s_all[i % len(attn_todo)][0]), w_in_ref[:, _UC0 + i * MXU_COLS:_UC0 + (i + 1) * MXU_COLS])
          for i in range(n_uc)]
    for i, (j, g) in enumerate(attn_todo):
        attend(j, g, s_all[i], uc[i::len(attn_todo)])

    kmask_ref[:, 0:BLOCK, :] = kmask_ref[:, ts:ts + BLOCK, :]
    vmask_ref[:, 0:BLOCK, :] = vmask_ref[:, ts:ts + BLOCK, :]
    kpos_ref[...] = pos_ref[0, n_blk - 1]

    n_glu = CONV_W // MXU_COLS
    for c in range(n_glu):
        a_c = ((uc[c] + b_glu_ref[:, cols(c)])
               * _sigmoid(uc[n_glu + c] + b_glu_ref[:, CONV_W + c * MXU_COLS:CONV_W + (c + 1) * MXU_COLS]))
        for i in range(MXU_COLS // LANES):
            conv_ref[c * (MXU_COLS // LANES) + i, CONV_CARRY:, :] = a_c[:, i * LANES:(i + 1) * LANES]
    g_conv = jnp.concatenate(uc[2 * n_glu:], axis=1)

    first_tap_row = CONV_CARRY - (CONV_K - 1)
    conv_out = {}

    def conv_piece(lc, r0):
        lanes = slice(lc * LANES, (lc + 1) * LANES)
        acc = jnp.broadcast_to(b_dw_ref[:, lanes], (CONV_ROWS, LANES))
        for t in range(CONV_K):
            off = r0 + first_tap_row + t
            acc = acc + w_dw_ref[t:t + 1, lanes] * conv_ref[lc, off:off + CONV_ROWS, :]
        conv_out[(lc, r0)] = acc

    conv_act = {}

    def conv_norm(r0):
        c = jnp.concatenate([conv_out.pop((lc, r0)) for lc in range(CONV_W // LANES)], axis=1)
        mu = jnp.mean(c, axis=-1, keepdims=True)
        cc = c - mu
        var = jnp.mean(cc * cc, axis=-1, keepdims=True)
        cn = cc * lax.rsqrt(var + LN_EPS) * ln_g_ref[...] + ln_b_ref[...]
        conv_act[r0] = (_silu(cn) * _silu(g_conv[r0:r0 + CONV_ROWS])).astype(_BF16)

    cross_out = {}
    qx = {}

    def cross_head(hh):
        per_chunk = MXU_COLS // CROSS_HEAD_DIM
        qh = qx[hh // per_chunk][:, (hh % per_chunk) * CROSS_HEAD_DIM:(hh % per_chunk + 1) * CROSS_HEAD_DIM]
        kv_cols = slice(hh * CROSS_HEAD_DIM, (hh + 1) * CROSS_HEAD_DIM)
        s = _dot_nt(qh.astype(_BF16), km_ref[:, kv_cols]) * (1.0 / math.sqrt(CROSS_HEAD_DIM))
        m = jnp.max(s, axis=1, keepdims=True)
        e = jnp.exp(s - m)
        l = jnp.sum(e, axis=1, keepdims=True)
        cross_out[hh] = _dot(e.astype(_BF16), vm_ref[:, kv_cols]) * (1.0 / l)

    vec_todo = []
    for r0 in range(0, ts, CONV_ROWS):
        vec_todo += [(conv_piece, (lc, r0)) for lc in range(CONV_W // LANES)]
        vec_todo.append((conv_norm, (r0,)))
    n_qx = (_MG0 - _QX0) // MXU_COLS
    gates = []
    for i in range(n_qx + 3 * n_col):
        if i < n_qx:
            qx[i] = proj(_QX0 + i * MXU_COLS)
            if i == CROSS_W // MXU_COLS - 1:
                vec_todo[2:2] = [(cross_head, (hh,)) for hh in range(N_CROSS_HEADS)]
        else:
            gates.append(_sigmoid(proj(_MG0 + (i - n_qx) * MXU_COLS)))
        if vec_todo:
            fn, args = vec_todo.pop(0)
            fn(*args)
    while vec_todo:
        fn, args = vec_todo.pop(0)
        fn(*args)
    conv_ref[:, 0:CONV_CARRY, :] = conv_ref[:, ts:ts + CONV_CARRY, :]

    g_cross = jnp.concatenate([qx[i] for i in range(CROSS_W // MXU_COLS, n_qx)], axis=1)
    attn_act = (attn_ref[...] * _silu(g_attn)).astype(_BF16)
    conv_all = jnp.concatenate([conv_act[r0] for r0 in range(0, ts, CONV_ROWS)], axis=0)
    cross_act = (jnp.concatenate([cross_out[hh] for hh in range(N_CROSS_HEADS)], axis=1)
                 * _silu(g_cross)).astype(_BF16)
    merged = []
    for c in range(n_col):
        y_attn = _dot(attn_act, w_oa_ref[:, cols(c)])
        y_conv = _dot(conv_all, w_pw_ref[:, cols(c)]) + b_pw_ref[:, cols(c)]
        y_cross = _dot(cross_act, w_oc_ref[:, cols(c)])
        merged.append((gates[c] * y_attn + gates[n_col + c] * y_conv
                       + gates[2 * n_col + c] * y_cross).astype(_BF16))
    merged = jnp.concatenate(merged, axis=1)

    x_new = jnp.concatenate([x[:, cols(c)] + _dot(merged, w_out_ref[:, cols(c)]) for c in range(n_col)],
                            axis=1)
    out_ref[0] = _rmsnorm(x_new, fng_ref[...])


def _const_spec(shape):
    nd = len(shape)
    return pl.BlockSpec(shape, lambda b, s: (0,) * nd, pipeline_mode=pl.Buffered(1))


@jax.jit
def kernel(x, mem, positions, norm_g, w_in, attn_sinks, w_o_attn, b_glu, w_dw, b_dw, ln_g, ln_b,
           w_pw, b_pw, mem_norm_g, w_mem_kv, w_o_cross, w_out, final_norm_g):
    batch, seq, d_model = x.shape
    assert d_model == D_MODEL and norm_g.shape[0] == 1 and w_in.shape == (1, D_MODEL, IN_W)
    assert seq % SEQ_TILE == 0 and SEQ_TILE % BLOCK == 0 and mem.shape[1] == MEM_LEN
    ts = SEQ_TILE
    n_blk_tile = ts // BLOCK

    hbm = pl.BlockSpec(memory_space=pl.ANY)
    in_specs = [
        pl.BlockSpec(memory_space=pltpu.SMEM),
        pl.BlockSpec((1, ts, D_MODEL), lambda b, s: (b, s, 0)),
        pl.BlockSpec((1, n_blk_tile, 1, BLOCK), lambda b, s: (b, s, 0, 0)),
        pl.BlockSpec((1, MEM_LEN, D_MODEL), lambda b, s: (b, 0, 0)),
        _const_spec((1, D_MODEL)),
        _const_spec((1, D_MODEL)),
        hbm, hbm, hbm, hbm, hbm, hbm,
        _const_spec((1, 2 * CONV_W)),
        _const_spec((CONV_K, CONV_W)),
        _const_spec((1, CONV_W)),
        _const_spec((1, CONV_W)),
        _const_spec((1, CONV_W)),
        _const_spec((1, D_MODEL)),
        _const_spec((1, D_MODEL)),
    ]
    return pl.pallas_call(
        _trunk_kernel,
        grid=(batch, seq // ts),
        in_specs=in_specs,
        out_specs=pl.BlockSpec((1, ts, D_MODEL), lambda b, s: (b, s, 0)),
        out_shape=jax.ShapeDtypeStruct((batch, seq, D_MODEL), x.dtype),
        scratch_shapes=[
            pltpu.VMEM((4, ts + BLOCK, KV_W), _BF16),
            pltpu.VMEM((4, ts + BLOCK, KV_W), _BF16),
            pltpu.VMEM((1, BLOCK), jnp.int32),
            pltpu.VMEM((ts, ATTN_W), _F32),
            pltpu.VMEM((CONV_W // LANES, ts + CONV_CARRY, LANES), _F32),
            pltpu.VMEM((D_MODEL, IN_W), _BF16),
            pltpu.VMEM((D_MODEL, D_MODEL), _BF16),
            pltpu.VMEM((ATTN_W, D_MODEL), _BF16),
            pltpu.VMEM((CONV_W, D_MODEL), _BF16),
            pltpu.VMEM((CROSS_W, D_MODEL), _BF16),
            pltpu.VMEM((D_MODEL, 2 * CROSS_W), _BF16),
            pltpu.VMEM((MEM_LEN, CROSS_W), _BF16),
            pltpu.VMEM((MEM_LEN, CROSS_W), _BF16),
        ],
        compiler_params=pltpu.CompilerParams(
            dimension_semantics=("arbitrary", "arbitrary"),
            vmem_limit_bytes=VMEM_LIMIT_BYTES),
        name="trunk",
    )(attn_sinks, x, positions.reshape(batch, seq // BLOCK, 1, BLOCK), mem, mem_norm_g, norm_g,
      w_in[0], w_out[0], w_o_attn[0], w_pw[0], w_o_cross[0], w_mem_kv[0], b_glu,
      w_dw.reshape(CONV_K, CONV_W), b_dw, ln_g, ln_b, b_pw, final_norm_g.reshape(1, D_MODEL))
```

```python
import math

import jax
import jax.numpy as jnp
from jax import lax
from jax.experimental import pallas as pl
from jax.experimental.pallas import tpu as pltpu

D_MODEL = 1024
MEM_LEN = 256
HEAD_DIM = 64
N_Q_HEADS = 8
N_KV_HEADS = 2
WINDOW = 128
BLOCK = 128
ATTN_W = N_Q_HEADS * HEAD_DIM
KV_W = N_KV_HEADS * HEAD_DIM
CONV_W = D_MODEL // 2
CONV_K = 31
N_CROSS_HEADS = 4
CROSS_HEAD_DIM = 128
CROSS_W = N_CROSS_HEADS * CROSS_HEAD_DIM
RMS_EPS = 1e-6
LN_EPS = 1e-5
NEG_BIG = -1e30
ALIBI_SLOPES = tuple(2.0 ** (-8.0 * (i + 1) / N_Q_HEADS) for i in range(N_Q_HEADS))

_Q0 = 0
_K0 = _Q0 + ATTN_W
_V0 = _K0 + KV_W
_GA0 = _V0 + KV_W
_UC0 = _GA0 + ATTN_W
_GC0 = _UC0 + 2 * CONV_W
_QX0 = _GC0 + CONV_W
_GX0 = _QX0 + CROSS_W
_MG0 = _GX0 + CROSS_W
IN_W = _MG0 + 3 * D_MODEL

LANES = 128
VREG_ROW_BYTES = 32
MXU_COLS = 256
SEQ_TILE = 512
CONV_CARRY = 32
CONV_ROWS = 128
LOAD_SLOTS = 6
LOAD_ROWS = 32
LOAD_ROWS_SMALL = 128
VMEM_LIMIT_BYTES = 56 * 1024 * 1024

_BF16 = jnp.bfloat16
_F32 = jnp.float32


def _dot(a, b):
    return jnp.dot(a, b, preferred_element_type=_F32)


def _dot_nt(a, b):
    return lax.dot_general(a, b, (((1,), (1,)), ((), ())), preferred_element_type=_F32)


def _rmsnorm(xf, g):
    y = xf * lax.rsqrt(jnp.mean(xf * xf, axis=-1, keepdims=True) + RMS_EPS)
    return y * g


def _sigmoid(x):
    return jax.nn.sigmoid(x)


def _silu(x):
    return x * jax.nn.sigmoid(x)


def _after(value, anchor):
    rows = VREG_ROW_BYTES // value.dtype.itemsize
    never = pl.program_id(0) < 0
    head = jnp.where(never, anchor[:rows, :LANES].astype(value.dtype), value[:rows, :LANES])
    top = jnp.concatenate([head, value[:rows, LANES:]], axis=1)
    return jnp.concatenate([top, value[rows:]], axis=0)


def _load_as_bf16(src_hbm, dst_ref, rows):
    n_chunks = src_hbm.shape[0] // rows

    def load(stage_ref, sem_ref):
        def chunk_copy(i, slot):
            return pltpu.make_async_copy(src_hbm.at[pl.ds(i * rows, rows)], stage_ref.at[slot],
                                         sem_ref.at[slot])

        for i in range(min(LOAD_SLOTS, n_chunks)):
            chunk_copy(i, i).start()

        def body(i, carry):
            slot = i % LOAD_SLOTS
            chunk_copy(i, slot).wait()
            dst_ref[pl.ds(pl.multiple_of(i * rows, rows), rows), :] = stage_ref[slot].astype(_BF16)

            @pl.when(i + LOAD_SLOTS < n_chunks)
            def _():
                chunk_copy(i + LOAD_SLOTS, slot).start()

            return carry

        lax.fori_loop(0, n_chunks, body, 0)

    pl.run_scoped(load, pltpu.VMEM((LOAD_SLOTS, rows, src_hbm.shape[1]), _F32),
                  pltpu.SemaphoreType.DMA((LOAD_SLOTS,)))


def _load_many_as_bf16(pairs, rows):
    chunks = [(src, dst, r0) for src, dst in pairs for r0 in range(0, src.shape[0], rows)]
    width = pairs[0][0].shape[1]

    def load(stage_ref, sem_ref):
        def chunk_copy(k):
            src, _, r0 = chunks[k]
            slot = k % LOAD_SLOTS
            return pltpu.make_async_copy(src.at[pl.ds(r0, rows)], stage_ref.at[slot], sem_ref.at[slot])

        for k in range(min(LOAD_SLOTS, len(chunks))):
            chunk_copy(k).start()
        for k, (_, dst, r0) in enumerate(chunks):
            chunk_copy(k).wait()
            dst[pl.ds(r0, rows), :] = stage_ref[k % LOAD_SLOTS].astype(_BF16)
            if k + LOAD_SLOTS < len(chunks):
                chunk_copy(k + LOAD_SLOTS).start()

    pl.run_scoped(load, pltpu.VMEM((LOAD_SLOTS, rows, width), _F32),
                  pltpu.SemaphoreType.DMA((LOAD_SLOTS,)))


def _trunk_kernel(sinks_ref, x_ref, pos_ref, mem_ref, mem_g_ref, norm_g_ref, w_in_hbm,
                  w_out_hbm, w_oa_hbm, w_pw_hbm, w_oc_hbm, w_mkv_hbm, b_glu_ref, w_dw_ref, b_dw_ref,
                  ln_g_ref, ln_b_ref, b_pw_ref, fng_ref, out_ref,
                  kmask_ref, vmask_ref, kpos_ref, attn_ref, conv_ref,
                  w_in_ref, w_out_ref, w_oa_ref, w_pw_ref, w_oc_ref, w_mkv_ref, km_ref, vm_ref):
    ts = x_ref.shape[1]
    n_blk = ts // BLOCK
    n_col = D_MODEL // MXU_COLS
    seq_step = pl.program_id(1)

    @pl.when((pl.program_id(0) == 0) & (seq_step == 0))
    def _():
        _load_as_bf16(w_in_hbm, w_in_ref, LOAD_ROWS)
        _load_many_as_bf16([(w_mkv_hbm, w_mkv_ref), (w_out_hbm, w_out_ref), (w_oa_hbm, w_oa_ref),
                            (w_pw_hbm, w_pw_ref), (w_oc_hbm, w_oc_ref)], LOAD_ROWS_SMALL)

    @pl.when(seq_step == 0)
    def _():
        kmask_ref[:, 0:BLOCK, :] = jnp.zeros((4, BLOCK, KV_W), _BF16)
        vmask_ref[:, 0:BLOCK, :] = jnp.zeros((4, BLOCK, KV_W), _BF16)
        conv_ref[:, 0:CONV_CARRY, :] = jnp.zeros((CONV_W // LANES, CONV_CARRY, LANES), _F32)
        kpos_ref[...] = jnp.broadcast_to(pos_ref[0, 0][:, 0:1] - (WINDOW + 1), (1, BLOCK))
        mem_n = _rmsnorm(mem_ref[0], mem_g_ref[...]).astype(_BF16)
        mem_kv = _dot(mem_n, w_mkv_ref[...])
        km_ref[...] = mem_kv[:, :CROSS_W].astype(_BF16)
        vm_ref[...] = mem_kv[:, CROSS_W:].astype(_BF16)

    x = x_ref[0]
    h = _rmsnorm(x, norm_g_ref[...]).astype(_BF16)

    def proj(c0, width=MXU_COLS):
        return _dot(h, w_in_ref[:, c0:c0 + width])

    def cols(c):
        return slice(c * MXU_COLS, (c + 1) * MXU_COLS)

    qkvg = proj(_Q0, _UC0 - _Q0)
    q = (qkvg[:, _Q0:_K0] * (1.0 / math.sqrt(HEAD_DIM))).astype(_BF16)
    k = qkvg[:, _K0:_V0]
    v = qkvg[:, _V0:_GA0]
    g_attn = qkvg[:, _GA0:_UC0]

    lo_half = lax.broadcasted_iota(jnp.int32, (ts, KV_W), 1) < HEAD_DIM
    zero = jnp.zeros((ts, KV_W), _F32)
    for src, dst_ref in ((k, kmask_ref), (v, vmask_ref)):
        rolled = pltpu.roll(src, HEAD_DIM, 1)
        dst_ref[0, BLOCK:, :] = jnp.where(lo_half, src, zero).astype(_BF16)
        dst_ref[1, BLOCK:, :] = jnp.where(lo_half, zero, rolled).astype(_BF16)
        dst_ref[2, BLOCK:, :] = jnp.where(lo_half, rolled, zero).astype(_BF16)
        dst_ref[3, BLOCK:, :] = jnp.where(lo_half, zero, src).astype(_BF16)

    lo_out = lax.broadcasted_iota(jnp.int32, (2 * BLOCK, 2 * HEAD_DIM), 1) < HEAD_DIM
    band_cache = {}

    def band_terms(j):
        if j not in band_cache:
            qcol = jnp.broadcast_to(pos_ref[0, j], (BLOCK, BLOCK)).T
            prev_pos = kpos_ref[...] if j == 0 else pos_ref[0, j - 1]
            delta = jnp.concatenate([qcol - prev_pos, qcol - pos_ref[0, j]], axis=1)
            band_cache[j] = ((delta >= 0) & (delta < WINDOW), delta.astype(_F32))
        return band_cache[j]

    def scores(j, g):
        rows = slice(j * BLOCK, (j + 1) * BLOCK)
        band = slice(j * BLOCK, (j + 2) * BLOCK)
        qg = jnp.concatenate([q[rows, (2 * g) * LANES:(2 * g + 1) * LANES],
                              q[rows, (2 * g + 1) * LANES:(2 * g + 2) * LANES]], axis=0)
        return [_dot_nt(qg, kmask_ref[2 * g + a, band, :]) for a in range(2)]

    def attend(j, g, s_pair, anchors):
        rows = slice(j * BLOCK, (j + 1) * BLOCK)
        band = slice(j * BLOCK, (j + 2) * BLOCK)
        allowed, delta_f = band_terms(j)
        probs = []
        inv_l = []
        for a in range(2):
            s = s_pair[a]
            e_parts = []
            l_parts = []
            for r in range(2):
                head = 2 * (2 * g + r) + a
                sr = s[r * BLOCK:(r + 1) * BLOCK]
                sr = jnp.where(allowed, sr - ALIBI_SLOPES[head] * delta_f, NEG_BIG)
                sink = sinks_ref[0, head]
                m = jnp.maximum(jnp.max(sr, axis=1, keepdims=True), sink)
                e = jnp.exp(sr - m)
                l_parts.append(jnp.sum(e, axis=1, keepdims=True) + jnp.exp(sink - m))
                e_parts.append(e.astype(_BF16))
            p_a = jnp.concatenate(e_parts, axis=0)
            if a < len(anchors):
                p_a = _after(p_a, anchors[a])
            probs.append(p_a)
            inv_l.append(1.0 / jnp.concatenate(l_parts, axis=0))
        o = (_dot(probs[0], vmask_ref[2 * g, band, :])
             + _dot(probs[1], vmask_ref[2 * g + 1, band, :]))
        o = o * jnp.where(lo_out, inv_l[0], inv_l[1])
        attn_ref[rows, (2 * g) * LANES:(2 * g + 1) * LANES] = o[:BLOCK]
        attn_ref[rows, (2 * g + 1) * LANES:(2 * g + 2) * LANES] = o[BLOCK:]

    n_uc = (_QX0 - _UC0) // MXU_COLS
    attn_todo = [(j, g) for j in range(n_blk) for g in range(N_KV_HEADS)]
    s_all = [scores(j, g) for j, g in attn_todo]
    uc = [_dot(_after(h, s_all[i % len(attn_todo)][0]), w_in_ref[:, _UC0 + i * MXU_COLS:_UC0 + (i + 1) * MXU_COLS])
          for i in range(n_uc)]
    for i, (j, g) in enumerate(attn_todo):
        attend(j, g, s_all[i], uc[i::len(attn_todo)])

    kmask_ref[:, 0:BLOCK, :] = kmask_ref[:, ts:ts + BLOCK, :]
    vmask_ref[:, 0:BLOCK, :] = vmask_ref[:, ts:ts + BLOCK, :]
    kpos_ref[...] = pos_ref[0, n_blk - 1]

    n_glu = CONV_W // MXU_COLS
    for c in range(n_glu):
        a_c = ((uc[c] + b_glu_ref[:, cols(c)])
               * _sigmoid(uc[n_glu + c] + b_glu_ref[:, CONV_W + c * MXU_COLS:CONV_W + (c + 1) * MXU_COLS]))
        for i in range(MXU_COLS // LANES):
            conv_ref[c * (MXU_COLS // LANES) + i, CONV_CARRY:, :] = a_c[:, i * LANES:(i + 1) * LANES]
    g_conv = jnp.concatenate(uc[2 * n_glu:], axis=1)

    first_tap_row = CONV_CARRY - (CONV_K - 1)
    conv_out = {}

    def conv_piece(lc, r0):
        lanes = slice(lc * LANES, (lc + 1) * LANES)
        acc = jnp.broadcast_to(b_dw_ref[:, lanes], (CONV_ROWS, LANES))
        for t in range(CONV_K):
            off = r0 + first_tap_row + t
            acc = acc + w_dw_ref[t:t + 1, lanes] * conv_ref[lc, off:off + CONV_ROWS, :]
        conv_out[(lc, r0)] = acc

    conv_act = {}

    def conv_norm(r0):
        c = jnp.concatenate([conv_out.pop((lc, r0)) for lc in range(CONV_W // LANES)], axis=1)
        mu = jnp.mean(c, axis=-1, keepdims=True)
        cc = c - mu
        var = jnp.mean(cc * cc, axis=-1, keepdims=True)
        cn = cc * lax.rsqrt(var + LN_EPS) * ln_g_ref[...] + ln_b_ref[...]
        conv_act[r0] = (_silu(cn) * _silu(g_conv[r0:r0 + CONV_ROWS])).astype(_BF16)

    cross_out = {}
    qx = {}

    def cross_head(hh):
        per_chunk = MXU_COLS // CROSS_HEAD_DIM
        qh = qx[hh // per_chunk][:, (hh % per_chunk) * CROSS_HEAD_DIM:(hh % per_chunk + 1) * CROSS_HEAD_DIM]
        kv_cols = slice(hh * CROSS_HEAD_DIM, (hh + 1) * CROSS_HEAD_DIM)
        s = _dot_nt(qh.astype(_BF16), km_ref[:, kv_cols]) * (1.0 / math.sqrt(CROSS_HEAD_DIM))
        m = jnp.max(s, axis=1, keepdims=True)
        e = jnp.exp(s - m)
        l = jnp.sum(e, axis=1, keepdims=True)
        cross_out[hh] = _dot(e.astype(_BF16), vm_ref[:, kv_cols]) * (1.0 / l)

    vec_todo = []
    for r0 in range(0, ts, CONV_ROWS):
        vec_todo += [(conv_piece, (lc, r0)) for lc in range(CONV_W // LANES)]
        vec_todo.append((conv_norm, (r0,)))
    n_qx = (_MG0 - _QX0) // MXU_COLS
    gates = []
    for i in range(n_qx + 3 * n_col):
        if i < n_qx:
            qx[i] = proj(_QX0 + i * MXU_COLS)
            if i == CROSS_W // MXU_COLS - 1:
                vec_todo[2:2] = [(cross_head, (hh,)) for hh in range(N_CROSS_HEADS)]
        else:
            gates.append(_sigmoid(proj(_MG0 + (i - n_qx) * MXU_COLS)))
        if vec_todo:
            fn, args = vec_todo.pop(0)
            fn(*args)
    while vec_todo:
        fn, args = vec_todo.pop(0)
        fn(*args)
    conv_ref[:, 0:CONV_CARRY, :] = conv_ref[:, ts:ts + CONV_CARRY, :]

    g_cross = jnp.concatenate([qx[i] for i in range(CROSS_W // MXU_COLS, n_qx)], axis=1)
    attn_act = (attn_ref[...] * _silu(g_attn)).astype(_BF16)
    conv_all = jnp.concatenate([conv_act[r0] for r0 in range(0, ts, CONV_ROWS)], axis=0)
    cross_act = (jnp.concatenate([cross_out[hh] for hh in range(N_CROSS_HEADS)], axis=1)
                 * _silu(g_cross)).astype(_BF16)
    merged = []
    for c in range(n_col):
        y_attn = _dot(attn_act, w_oa_ref[:, cols(c)])
        y_conv = _dot(conv_all, w_pw_ref[:, cols(c)]) + b_pw_ref[:, cols(c)]
        y_cross = _dot(cross_act, w_oc_ref[:, cols(c)])
        merged.append((gates[c] * y_attn + gates[n_col + c] * y_conv
                       + gates[2 * n_col + c] * y_cross).astype(_BF16))
    merged = jnp.concatenate(merged, axis=1)

    x_new = jnp.concatenate([x[:, cols(c)] + _dot(merged, w_out_ref[:, cols(c)]) for c in range(n_col)],
                            axis=1)
    out_ref[0] = _rmsnorm(x_new, fng_ref[...])


def _const_spec(shape):
    nd = len(shape)
    return pl.BlockSpec(shape, lambda b, s: (0,) * nd, pipeline_mode=pl.Buffered(1))


@jax.jit
def kernel(x, mem, positions, norm_g, w_in, attn_sinks, w_o_attn, b_glu, w_dw, b_dw, ln_g, ln_b,
           w_pw, b_pw, mem_norm_g, w_mem_kv, w_o_cross, w_out, final_norm_g):
    batch, seq, d_model = x.shape
    assert d_model == D_MODEL and norm_g.shape[0] == 1 and w_in.shape == (1, D_MODEL, IN_W)
    assert seq % SEQ_TILE == 0 and SEQ_TILE % BLOCK == 0 and mem.shape[1] == MEM_LEN
    ts = SEQ_TILE
    n_blk_tile = ts // BLOCK

    hbm = pl.BlockSpec(memory_space=pl.ANY)
    in_specs = [
        pl.BlockSpec(memory_space=pltpu.SMEM),
        pl.BlockSpec((1, ts, D_MODEL), lambda b, s: (b, s, 0)),
        pl.BlockSpec((1, n_blk_tile, 1, BLOCK), lambda b, s: (b, s, 0, 0)),
        pl.BlockSpec((1, MEM_LEN, D_MODEL), lambda b, s: (b, 0, 0)),
        _const_spec((1, D_MODEL)),
        _const_spec((1, D_MODEL)),
        hbm, hbm, hbm, hbm, hbm, hbm,
        _const_spec((1, 2 * CONV_W)),
        _const_spec((CONV_K, CONV_W)),
        _const_spec((1, CONV_W)),
        _const_spec((1, CONV_W)),
        _const_spec((1, CONV_W)),
        _const_spec((1, D_MODEL)),
        _const_spec((1, D_MODEL)),
    ]
    return pl.pallas_call(
        _trunk_kernel,
        grid=(batch, seq // ts),
        in_specs=in_specs,
        out_specs=pl.BlockSpec((1, ts, D_MODEL), lambda b, s: (b, s, 0)),
        out_shape=jax.ShapeDtypeStruct((batch, seq, D_MODEL), x.dtype),
        scratch_shapes=[
            pltpu.VMEM((4, ts + BLOCK, KV_W), _BF16),
            pltpu.VMEM((4, ts + BLOCK, KV_W), _BF16),
            pltpu.VMEM((1, BLOCK), jnp.int32),
            pltpu.VMEM((ts, ATTN_W), _F32),
            pltpu.VMEM((CONV_W // LANES, ts + CONV_CARRY, LANES), _F32),
            pltpu.VMEM((D_MODEL, IN_W), _BF16),
            pltpu.VMEM((D_MODEL, D_MODEL), _BF16),
            pltpu.VMEM((ATTN_W, D_MODEL), _BF16),
            pltpu.VMEM((CONV_W, D_MODEL), _BF16),
            pltpu.VMEM((CROSS_W, D_MODEL), _BF16),
            pltpu.VMEM((D_MODEL, 2 * CROSS_W), _BF16),
            pltpu.VMEM((MEM_LEN, CROSS_W), _BF16),
            pltpu.VMEM((MEM_LEN, CROSS_W), _BF16),
        ],
        compiler_params=pltpu.CompilerParams(
            dimension_semantics=("arbitrary", "arbitrary"),
            vmem_limit_bytes=VMEM_LIMIT_BYTES),
        name="trunk",
    )(attn_sinks, x, positions.reshape(batch, seq // BLOCK, 1, BLOCK), mem, mem_norm_g, norm_g,
      w_in[0], w_out[0], w_o_attn[0], w_pw[0], w_o_cross[0], w_mem_kv[0], b_glu,
      w_dw.reshape(CONV_K, CONV_W), b_dw, ln_g, ln_b, b_pw, final_norm_g.reshape(1, D_MODEL))
```

```python
import math

import jax
import jax.numpy as jnp
from jax import lax
from jax.experimental import pallas as pl
from jax.experimental.pallas import tpu as pltpu

D_MODEL = 1024
MEM_LEN = 256
HEAD_DIM = 64
N_Q_HEADS = 8
N_KV_HEADS = 2
WINDOW = 128
BLOCK = 128
ATTN_W = N_Q_HEADS * HEAD_DIM
KV_W = N_KV_HEADS * HEAD_DIM
CONV_W = D_MODEL // 2
CONV_K = 31
N_CROSS_HEADS = 4
CROSS_HEAD_DIM = 128
CROSS_W = N_CROSS_HEADS * CROSS_HEAD_DIM
RMS_EPS = 1e-6
LN_EPS = 1e-5
NEG_BIG = -1e30
ALIBI_SLOPES = tuple(2.0 ** (-8.0 * (i + 1) / N_Q_HEADS) for i in range(N_Q_HEADS))

_Q0 = 0
_K0 = _Q0 + ATTN_W
_V0 = _K0 + KV_W
_GA0 = _V0 + KV_W
_UC0 = _GA0 + ATTN_W
_GC0 = _UC0 + 2 * CONV_W
_QX0 = _GC0 + CONV_W
_GX0 = _QX0 + CROSS_W
_MG0 = _GX0 + CROSS_W
IN_W = _MG0 + 3 * D_MODEL

LANES = 128
VREG_ROW_BYTES = 32
MXU_COLS = 256
SEQ_TILE = 512
CONV_CARRY = 32
CONV_ROWS = 128
LOAD_SLOTS = 8
LOAD_ROWS = 32
LOAD_ROWS_SMALL = 128
VMEM_LIMIT_BYTES = 56 * 1024 * 1024

_BF16 = jnp.bfloat16
_F32 = jnp.float32


def _dot(a, b):
    return jnp.dot(a, b, preferred_element_type=_F32)


def _dot_nt(a, b):
    return lax.dot_general(a, b, (((1,), (1,)), ((), ())), preferred_element_type=_F32)


def _rmsnorm(xf, g):
    y = xf * lax.rsqrt(jnp.mean(xf * xf, axis=-1, keepdims=True) + RMS_EPS)
    return y * g


def _sigmoid(x):
    return jax.nn.sigmoid(x)


def _silu(x):
    return x * jax.nn.sigmoid(x)


def _after(value, anchor):
    rows = VREG_ROW_BYTES // value.dtype.itemsize
    never = pl.program_id(0) < 0
    head = jnp.where(never, anchor[:rows, :LANES].astype(value.dtype), value[:rows, :LANES])
    top = jnp.concatenate([head, value[:rows, LANES:]], axis=1)
    return jnp.concatenate([top, value[rows:]], axis=0)


def _load_as_bf16(src_hbm, dst_ref, rows):
    n_chunks = src_hbm.shape[0] // rows

    def load(stage_ref, sem_ref):
        def chunk_copy(i, slot):
            return pltpu.make_async_copy(src_hbm.at[pl.ds(i * rows, rows)], stage_ref.at[slot],
                                         sem_ref.at[slot])

        for i in range(min(LOAD_SLOTS, n_chunks)):
            chunk_copy(i, i).start()

        def body(i, carry):
            slot = i % LOAD_SLOTS
            chunk_copy(i, slot).wait()
            dst_ref[pl.ds(pl.multiple_of(i * rows, rows), rows), :] = stage_ref[slot].astype(_BF16)

            @pl.when(i + LOAD_SLOTS < n_chunks)
            def _():
                chunk_copy(i + LOAD_SLOTS, slot).start()

            return carry

        lax.fori_loop(0, n_chunks, body, 0)

    pl.run_scoped(load, pltpu.VMEM((LOAD_SLOTS, rows, src_hbm.shape[1]), _F32),
                  pltpu.SemaphoreType.DMA((LOAD_SLOTS,)))


def _load_many_as_bf16(pairs, rows):
    chunks = [(src, dst, r0) for src, dst in pairs for r0 in range(0, src.shape[0], rows)]
    width = pairs[0][0].shape[1]

    def load(stage_ref, sem_ref):
        def chunk_copy(k):
            src, _, r0 = chunks[k]
            slot = k % LOAD_SLOTS
            return pltpu.make_async_copy(src.at[pl.ds(r0, rows)], stage_ref.at[slot], sem_ref.at[slot])

        for k in range(min(LOAD_SLOTS, len(chunks))):
            chunk_copy(k).start()
        for k, (_, dst, r0) in enumerate(chunks):
            chunk_copy(k).wait()
            dst[pl.ds(r0, rows), :] = stage_ref[k % LOAD_SLOTS].astype(_BF16)
            if k + LOAD_SLOTS < len(chunks):
                chunk_copy(k + LOAD_SLOTS).start()

    pl.run_scoped(load, pltpu.VMEM((LOAD_SLOTS, rows, width), _F32),
                  pltpu.SemaphoreType.DMA((LOAD_SLOTS,)))


def _trunk_kernel(sinks_ref, x_ref, pos_ref, mem_ref, mem_g_ref, norm_g_ref, w_in_hbm,
                  w_out_hbm, w_oa_hbm, w_pw_hbm, w_oc_hbm, w_mkv_hbm, b_glu_ref, w_dw_ref, b_dw_ref,
                  ln_g_ref, ln_b_ref, b_pw_ref, fng_ref, out_ref,
                  kmask_ref, vmask_ref, kpos_ref, attn_ref, conv_ref,
                  w_in_ref, w_out_ref, w_oa_ref, w_pw_ref, w_oc_ref, w_mkv_ref, km_ref, vm_ref):
    ts = x_ref.shape[1]
    n_blk = ts // BLOCK
    n_col = D_MODEL // MXU_COLS
    seq_step = pl.program_id(1)

    @pl.when((pl.program_id(0) == 0) & (seq_step == 0))
    def _():
        _load_as_bf16(w_in_hbm, w_in_ref, LOAD_ROWS)
        _load_many_as_bf16([(w_mkv_hbm, w_mkv_ref), (w_out_hbm, w_out_ref), (w_oa_hbm, w_oa_ref),
                            (w_pw_hbm, w_pw_ref), (w_oc_hbm, w_oc_ref)], LOAD_ROWS_SMALL)

    @pl.when(seq_step == 0)
    def _():
        kmask_ref[:, 0:BLOCK, :] = jnp.zeros((4, BLOCK, KV_W), _BF16)
        vmask_ref[:, 0:BLOCK, :] = jnp.zeros((4, BLOCK, KV_W), _BF16)
        conv_ref[:, 0:CONV_CARRY, :] = jnp.zeros((CONV_W // LANES, CONV_CARRY, LANES), _F32)
        kpos_ref[...] = jnp.broadcast_to(pos_ref[0, 0][:, 0:1] - (WINDOW + 1), (1, BLOCK))
        mem_n = _rmsnorm(mem_ref[0], mem_g_ref[...]).astype(_BF16)
        mem_kv = _dot(mem_n, w_mkv_ref[...])
        km_ref[...] = mem_kv[:, :CROSS_W].astype(_BF16)
        vm_ref[...] = mem_kv[:, CROSS_W:].astype(_BF16)

    x = x_ref[0]
    h = _rmsnorm(x, norm_g_ref[...]).astype(_BF16)

    def proj(c0, width=MXU_COLS):
        return _dot(h, w_in_ref[:, c0:c0 + width])

    def cols(c):
        return slice(c * MXU_COLS, (c + 1) * MXU_COLS)

    qkvg = proj(_Q0, _UC0 - _Q0)
    q = (qkvg[:, _Q0:_K0] * (1.0 / math.sqrt(HEAD_DIM))).astype(_BF16)
    k = qkvg[:, _K0:_V0]
    v = qkvg[:, _V0:_GA0]
    g_attn = qkvg[:, _GA0:_UC0]

    lo_half = lax.broadcasted_iota(jnp.int32, (ts, KV_W), 1) < HEAD_DIM
    zero = jnp.zeros((ts, KV_W), _F32)
    for src, dst_ref in ((k, kmask_ref), (v, vmask_ref)):
        rolled = pltpu.roll(src, HEAD_DIM, 1)
        dst_ref[0, BLOCK:, :] = jnp.where(lo_half, src, zero).astype(_BF16)
        dst_ref[1, BLOCK:, :] = jnp.where(lo_half, zero, rolled).astype(_BF16)
        dst_ref[2, BLOCK:, :] = jnp.where(lo_half, rolled, zero).astype(_BF16)
        dst_ref[3, BLOCK:, :] = jnp.where(lo_half, zero, src).astype(_BF16)

    lo_out = lax.broadcasted_iota(jnp.int32, (2 * BLOCK, 2 * HEAD_DIM), 1) < HEAD_DIM
    band_cache = {}

    def band_terms(j):
        if j not in band_cache:
            qcol = jnp.broadcast_to(pos_ref[0, j], (BLOCK, BLOCK)).T
            prev_pos = kpos_ref[...] if j == 0 else pos_ref[0, j - 1]
            delta = jnp.concatenate([qcol - prev_pos, qcol - pos_ref[0, j]], axis=1)
            band_cache[j] = ((delta >= 0) & (delta < WINDOW), delta.astype(_F32))
        return band_cache[j]

    def scores(j, g):
        rows = slice(j * BLOCK, (j + 1) * BLOCK)
        band = slice(j * BLOCK, (j + 2) * BLOCK)
        qg = jnp.concatenate([q[rows, (2 * g) * LANES:(2 * g + 1) * LANES],
                              q[rows, (2 * g + 1) * LANES:(2 * g + 2) * LANES]], axis=0)
        return [_dot_nt(qg, kmask_ref[2 * g + a, band, :]) for a in range(2)]

    def attend(j, g, s_pair, anchors):
        rows = slice(j * BLOCK, (j + 1) * BLOCK)
        band = slice(j * BLOCK, (j + 2) * BLOCK)
        allowed, delta_f = band_terms(j)
        probs = []
        inv_l = []
        for a in range(2):
            s = s_pair[a]
            e_parts = []
            l_parts = []
            for r in range(2):
                head = 2 * (2 * g + r) + a
                sr = s[r * BLOCK:(r + 1) * BLOCK]
                sr = jnp.where(allowed, sr - ALIBI_SLOPES[head] * delta_f, NEG_BIG)
                sink = sinks_ref[0, head]
                m = jnp.maximum(jnp.max(sr, axis=1, keepdims=True), sink)
                e = jnp.exp(sr - m)
                l_parts.append(jnp.sum(e, axis=1, keepdims=True) + jnp.exp(sink - m))
                e_parts.append(e.astype(_BF16))
            p_a = jnp.concatenate(e_parts, axis=0)
            if a < len(anchors):
                p_a = _after(p_a, anchors[a])
            probs.append(p_a)
            inv_l.append(1.0 / jnp.concatenate(l_parts, axis=0))
        o = (_dot(probs[0], vmask_ref[2 * g, band, :])
             + _dot(probs[1], vmask_ref[2 * g + 1, band, :]))
        o = o * jnp.where(lo_out, inv_l[0], inv_l[1])
        attn_ref[rows, (2 * g) * LANES:(2 * g + 1) * LANES] = o[:BLOCK]
        attn_ref[rows, (2 * g + 1) * LANES:(2 * g + 2) * LANES] = o[BLOCK:]

    n_uc = (_QX0 - _UC0) // MXU_COLS
    attn_todo = [(j, g) for j in range(n_blk) for g in range(N_KV_HEADS)]
    s_all = [scores(j, g) for j, g in attn_todo]
    uc = [_dot(_after(h, s_all[i % len(attn_todo)][0]), w_in_ref[:, _UC0 + i * MXU_COLS:_UC0 + (i + 1) * MXU_COLS])
          for i in range(n_uc)]
    for i, (j, g) in enumerate(attn_todo):
        attend(j, g, s_all[i], uc[i::len(attn_todo)])

    kmask_ref[:, 0:BLOCK, :] = kmask_ref[:, ts:ts + BLOCK, :]
    vmask_ref[:, 0:BLOCK, :] = vmask_ref[:, ts:ts + BLOCK, :]
    kpos_ref[...] = pos_ref[0, n_blk - 1]

    n_glu = CONV_W // MXU_COLS
    for c in range(n_glu):
        a_c = ((uc[c] + b_glu_ref[:, cols(c)])
               * _sigmoid(uc[n_glu + c] + b_glu_ref[:, CONV_W + c * MXU_COLS:CONV_W + (c + 1) * MXU_COLS]))
        for i in range(MXU_COLS // LANES):
            conv_ref[c * (MXU_COLS // LANES) + i, CONV_CARRY:, :] = a_c[:, i * LANES:(i + 1) * LANES]
    g_conv = jnp.concatenate(uc[2 * n_glu:], axis=1)

    first_tap_row = CONV_CARRY - (CONV_K - 1)
    conv_out = {}

    def conv_piece(lc, r0):
        lanes = slice(lc * LANES, (lc + 1) * LANES)
        acc = jnp.broadcast_to(b_dw_ref[:, lanes], (CONV_ROWS, LANES))
        for t in range(CONV_K):
            off = r0 + first_tap_row + t
            acc = acc + w_dw_ref[t:t + 1, lanes] * conv_ref[lc, off:off + CONV_ROWS, :]
        conv_out[(lc, r0)] = acc

    conv_act = {}

    def conv_norm(r0):
        c = jnp.concatenate([conv_out.pop((lc, r0)) for lc in range(CONV_W // LANES)], axis=1)
        mu = jnp.mean(c, axis=-1, keepdims=True)
        cc = c - mu
        var = jnp.mean(cc * cc, axis=-1, keepdims=True)
        cn = cc * lax.rsqrt(var + LN_EPS) * ln_g_ref[...] + ln_b_ref[...]
        conv_act[r0] = (_silu(cn) * _silu(g_conv[r0:r0 + CONV_ROWS])).astype(_BF16)

    cross_out = {}
    qx = {}

    def cross_head(hh):
        per_chunk = MXU_COLS // CROSS_HEAD_DIM
        qh = qx[hh // per_chunk][:, (hh % per_chunk) * CROSS_HEAD_DIM:(hh % per_chunk + 1) * CROSS_HEAD_DIM]
        kv_cols = slice(hh * CROSS_HEAD_DIM, (hh + 1) * CROSS_HEAD_DIM)
        s = _dot_nt(qh.astype(_BF16), km_ref[:, kv_cols]) * (1.0 / math.sqrt(CROSS_HEAD_DIM))
        m = jnp.max(s, axis=1, keepdims=True)
        e = jnp.exp(s - m)
        l = jnp.sum(e, axis=1, keepdims=True)
        cross_out[hh] = _dot(e.astype(_BF16), vm_ref[:, kv_cols]) * (1.0 / l)

    vec_todo = []
    for r0 in range(0, ts, CONV_ROWS):
        vec_todo += [(conv_piece, (lc, r0)) for lc in range(CONV_W // LANES)]
        vec_todo.append((conv_norm, (r0,)))
    n_qx = (_MG0 - _QX0) // MXU_COLS
    gates = []
    for i in range(n_qx + 3 * n_col):
        if i < n_qx:
            qx[i] = proj(_QX0 + i * MXU_COLS)
            if i == CROSS_W // MXU_COLS - 1:
                vec_todo[2:2] = [(cross_head, (hh,)) for hh in range(N_CROSS_HEADS)]
        else:
            gates.append(_sigmoid(proj(_MG0 + (i - n_qx) * MXU_COLS)))
        if vec_todo:
            fn, args = vec_todo.pop(0)
            fn(*args)
    while vec_todo:
        fn, args = vec_todo.pop(0)
        fn(*args)
    conv_ref[:, 0:CONV_CARRY, :] = conv_ref[:, ts:ts + CONV_CARRY, :]

    g_cross = jnp.concatenate([qx[i] for i in range(CROSS_W // MXU_COLS, n_qx)], axis=1)
    attn_act = (attn_ref[...] * _silu(g_attn)).astype(_BF16)
    conv_all = jnp.concatenate([conv_act[r0] for r0 in range(0, ts, CONV_ROWS)], axis=0)
    cross_act = (jnp.concatenate([cross_out[hh] for hh in range(N_CROSS_HEADS)], axis=1)
                 * _silu(g_cross)).astype(_BF16)
    merged = []
    for c in range(n_col):
        y_attn = _dot(attn_act, w_oa_ref[:, cols(c)])
        y_conv = _dot(conv_all, w_pw_ref[:, cols(c)]) + b_pw_ref[:, cols(c)]
        y_cross = _dot(cross_act, w_oc_ref[:, cols(c)])
        merged.append((gates[c] * y_attn + gates[n_col + c] * y_conv
                       + gates[2 * n_col + c] * y_cross).astype(_BF16))
    merged = jnp.concatenate(merged, axis=1)

    x_new = jnp.concatenate([x[:, cols(c)] + _dot(merged, w_out_ref[:, cols(c)]) for c in range(n_col)],
                            axis=1)
    out_ref[0] = _rmsnorm(x_new, fng_ref[...])


def _const_spec(shape):
    nd = len(shape)
    return pl.BlockSpec(shape, lambda b, s: (0,) * nd, pipeline_mode=pl.Buffered(1))


@jax.jit
def kernel(x, mem, positions, norm_g, w_in, attn_sinks, w_o_attn, b_glu, w_dw, b_dw, ln_g, ln_b,
           w_pw, b_pw, mem_norm_g, w_mem_kv, w_o_cross, w_out, final_norm_g):
    batch, seq, d_model = x.shape
    assert d_model == D_MODEL and norm_g.shape[0] == 1 and w_in.shape == (1, D_MODEL, IN_W)
    assert seq % SEQ_TILE == 0 and SEQ_TILE % BLOCK == 0 and mem.shape[1] == MEM_LEN
    ts = SEQ_TILE
    n_blk_tile = ts // BLOCK

    hbm = pl.BlockSpec(memory_space=pl.ANY)
    in_specs = [
        pl.BlockSpec(memory_space=pltpu.SMEM),
        pl.BlockSpec((1, ts, D_MODEL), lambda b, s: (b, s, 0)),
        pl.BlockSpec((1, n_blk_tile, 1, BLOCK), lambda b, s: (b, s, 0, 0)),
        pl.BlockSpec((1, MEM_LEN, D_MODEL), lambda b, s: (b, 0, 0)),
        _const_spec((1, D_MODEL)),
        _const_spec((1, D_MODEL)),
        hbm, hbm, hbm, hbm, hbm, hbm,
        _const_spec((1, 2 * CONV_W)),
        _const_spec((CONV_K, CONV_W)),
        _const_spec((1, CONV_W)),
        _const_spec((1, CONV_W)),
        _const_spec((1, CONV_W)),
        _const_spec((1, D_MODEL)),
        _const_spec((1, D_MODEL)),
    ]
    return pl.pallas_call(
        _trunk_kernel,
        grid=(batch, seq // ts),
        in_specs=in_specs,
        out_specs=pl.BlockSpec((1, ts, D_MODEL), lambda b, s: (b, s, 0)),
        out_shape=jax.ShapeDtypeStruct((batch, seq, D_MODEL), x.dtype),
        scratch_shapes=[
            pltpu.VMEM((4, ts + BLOCK, KV_W), _BF16),
            pltpu.VMEM((4, ts + BLOCK, KV_W), _BF16),
            pltpu.VMEM((1, BLOCK), jnp.int32),
            pltpu.VMEM((ts, ATTN_W), _F32),
            pltpu.VMEM((CONV_W // LANES, ts + CONV_CARRY, LANES), _F32),
            pltpu.VMEM((D_MODEL, IN_W), _BF16),
            pltpu.VMEM((D_MODEL, D_MODEL), _BF16),
            pltpu.VMEM((ATTN_W, D_MODEL), _BF16),
            pltpu.VMEM((CONV_W, D_MODEL), _BF16),
            pltpu.VMEM((CROSS_W, D_MODEL), _BF16),
            pltpu.VMEM((D_MODEL, 2 * CROSS_W), _BF16),
            pltpu.VMEM((MEM_LEN, CROSS_W), _BF16),
            pltpu.VMEM((MEM_LEN, CROSS_W), _BF16),
        ],
        compiler_params=pltpu.CompilerParams(
            dimension_semantics=("arbitrary", "arbitrary"),
            vmem_limit_bytes=VMEM_LIMIT_BYTES),
        name="trunk",
    )(attn_sinks, x, positions.reshape(batch, seq // BLOCK, 1, BLOCK), mem, mem_norm_g, norm_g,
      w_in[0], w_out[0], w_o_attn[0], w_pw[0], w_o_cross[0], w_mem_kv[0], b_glu,
      w_dw.reshape(CONV_K, CONV_W), b_dw, ln_g, ln_b, b_pw, final_norm_g.reshape(1, D_MODEL))
```

```python
import math

import jax
import jax.numpy as jnp
from jax import lax
from jax.experimental import pallas as pl
from jax.experimental.pallas import tpu as pltpu

D_MODEL = 1024
MEM_LEN = 256
HEAD_DIM = 64
N_Q_HEADS = 8
N_KV_HEADS = 2
WINDOW = 128
BLOCK = 128
ATTN_W = N_Q_HEADS * HEAD_DIM
KV_W = N_KV_HEADS * HEAD_DIM
CONV_W = D_MODEL // 2
CONV_K = 31
N_CROSS_HEADS = 4
CROSS_HEAD_DIM = 128
CROSS_W = N_CROSS_HEADS * CROSS_HEAD_DIM
RMS_EPS = 1e-6
LN_EPS = 1e-5
NEG_BIG = -1e30
ALIBI_SLOPES = tuple(2.0 ** (-8.0 * (i + 1) / N_Q_HEADS) for i in range(N_Q_HEADS))

_Q0 = 0
_K0 = _Q0 + ATTN_W
_V0 = _K0 + KV_W
_GA0 = _V0 + KV_W
_UC0 = _GA0 + ATTN_W
_GC0 = _UC0 + 2 * CONV_W
_QX0 = _GC0 + CONV_W
_GX0 = _QX0 + CROSS_W
_MG0 = _GX0 + CROSS_W
IN_W = _MG0 + 3 * D_MODEL

LANES = 128
VREG_ROW_BYTES = 32
MXU_COLS = 256
SEQ_TILE = 512
CONV_CARRY = 32
CONV_ROWS = 128
LOAD_SLOTS = 8
LOAD_ROWS = 32
LOAD_ROWS_SMALL = 128
VMEM_LIMIT_BYTES = 56 * 1024 * 1024

_BF16 = jnp.bfloat16
_F32 = jnp.float32


def _dot(a, b):
    return jnp.dot(a, b, preferred_element_type=_F32)


def _dot_nt(a, b):
    return lax.dot_general(a, b, (((1,), (1,)), ((), ())), preferred_element_type=_F32)


def _rmsnorm(xf, g):
    y = xf * lax.rsqrt(jnp.mean(xf * xf, axis=-1, keepdims=True) + RMS_EPS)
    return y * g


def _sigmoid(x):
    return jax.nn.sigmoid(x)


def _silu(x):
    return x * jax.nn.sigmoid(x)


def _after(value, anchor):
    rows = VREG_ROW_BYTES // value.dtype.itemsize
    never = pl.program_id(0) < 0
    head = jnp.where(never, anchor[:rows, :LANES].astype(value.dtype), value[:rows, :LANES])
    top = jnp.concatenate([head, value[:rows, LANES:]], axis=1)
    return jnp.concatenate([top, value[rows:]], axis=0)


def _load_as_bf16(src_hbm, dst_ref, rows):
    n_chunks = src_hbm.shape[0] // rows

    def load(stage_ref, sem_ref):
        def chunk_copy(i, slot):
            return pltpu.make_async_copy(src_hbm.at[pl.ds(i * rows, rows)], stage_ref.at[slot],
                                         sem_ref.at[slot])

        for i in range(min(LOAD_SLOTS, n_chunks)):
            chunk_copy(i, i).start()

        def body(i, carry):
            slot = i % LOAD_SLOTS
            chunk_copy(i, slot).wait()
            dst_ref[pl.ds(pl.multiple_of(i * rows, rows), rows), :] = stage_ref[slot].astype(_BF16)

            @pl.when(i + LOAD_SLOTS < n_chunks)
            def _():
                chunk_copy(i + LOAD_SLOTS, slot).start()

            return carry

        lax.fori_loop(0, n_chunks, body, 0)

    pl.run_scoped(load, pltpu.VMEM((LOAD_SLOTS, rows, src_hbm.shape[1]), _F32),
                  pltpu.SemaphoreType.DMA((LOAD_SLOTS,)))


def _load_many_as_bf16(pairs, rows):
    chunks = [(src, dst, r0) for src, dst in pairs for r0 in range(0, src.shape[0], rows)]
    width = pairs[0][0].shape[1]

    def load(stage_ref, sem_ref):
        def chunk_copy(k):
            src, _, r0 = chunks[k]
            slot = k % LOAD_SLOTS
            return pltpu.make_async_copy(src.at[pl.ds(r0, rows)], stage_ref.at[slot], sem_ref.at[slot])

        for k in range(min(LOAD_SLOTS, len(chunks))):
            chunk_copy(k).start()
        for k, (_, dst, r0) in enumerate(chunks):
            chunk_copy(k).wait()
            dst[pl.ds(r0, rows), :] = stage_ref[k % LOAD_SLOTS].astype(_BF16)
            if k + LOAD_SLOTS < len(chunks):
                chunk_copy(k + LOAD_SLOTS).start()

    pl.run_scoped(load, pltpu.VMEM((LOAD_SLOTS, rows, width), _F32),
                  pltpu.SemaphoreType.DMA((LOAD_SLOTS,)))


def _trunk_kernel(sinks_ref, x_ref, pos_ref, mem_ref, mem_g_ref, norm_g_ref, w_in_hbm,
                  w_out_hbm, w_oa_hbm, w_pw_hbm, w_oc_hbm, w_mkv_hbm, b_glu_ref, w_dw_ref, b_dw_ref,
                  ln_g_ref, ln_b_ref, b_pw_ref, fng_ref, out_ref,
                  kmask_ref, vmask_ref, kpos_ref, attn_ref, conv_ref,
                  w_in_ref, w_out_ref, w_oa_ref, w_pw_ref, w_oc_ref, w_mkv_ref, km_ref, vm_ref):
    ts = x_ref.shape[1]
    n_blk = ts // BLOCK
    n_col = D_MODEL // MXU_COLS
    seq_step = pl.program_id(1)

    @pl.when((pl.program_id(0) == 0) & (seq_step == 0))
    def _():
        _load_as_bf16(w_in_hbm, w_in_ref, LOAD_ROWS)
        _load_many_as_bf16([(w_mkv_hbm, w_mkv_ref), (w_out_hbm, w_out_ref), (w_oa_hbm, w_oa_ref),
                            (w_pw_hbm, w_pw_ref), (w_oc_hbm, w_oc_ref)], LOAD_ROWS_SMALL)

    @pl.when(seq_step == 0)
    def _():
        kmask_ref[:, 0:BLOCK, :] = jnp.zeros((4, BLOCK, KV_W), _BF16)
        vmask_ref[:, 0:BLOCK, :] = jnp.zeros((4, BLOCK, KV_W), _BF16)
        conv_ref[:, 0:CONV_CARRY, :] = jnp.zeros((CONV_W // LANES, CONV_CARRY, LANES), _F32)
        kpos_ref[...] = jnp.broadcast_to(pos_ref[0, 0][:, 0:1] - (WINDOW + 1), (1, BLOCK))
        mem_n = _rmsnorm(mem_ref[0], mem_g_ref[...]).astype(_BF16)
        mem_kv = _dot(mem_n, w_mkv_ref[...])
        km_ref[...] = mem_kv[:, :CROSS_W].astype(_BF16)
        vm_ref[...] = mem_kv[:, CROSS_W:].astype(_BF16)

    x = x_ref[0]
    h = _rmsnorm(x, norm_g_ref[...]).astype(_BF16)

    def proj(c0, width=MXU_COLS):
        return _dot(h, w_in_ref[:, c0:c0 + width])

    def cols(c):
        return slice(c * MXU_COLS, (c + 1) * MXU_COLS)

    qkvg = proj(_Q0, _UC0 - _Q0)
    q = (qkvg[:, _Q0:_K0] * (1.0 / math.sqrt(HEAD_DIM))).astype(_BF16)
    k = qkvg[:, _K0:_V0]
    v = qkvg[:, _V0:_GA0]
    g_attn = qkvg[:, _GA0:_UC0]

    lo_half = lax.broadcasted_iota(jnp.int32, (ts, KV_W), 1) < HEAD_DIM
    zero = jnp.zeros((ts, KV_W), _F32)
    for src, dst_ref in ((k, kmask_ref), (v, vmask_ref)):
        rolled = pltpu.roll(src, HEAD_DIM, 1)
        dst_ref[0, BLOCK:, :] = jnp.where(lo_half, src, zero).astype(_BF16)
        dst_ref[1, BLOCK:, :] = jnp.where(lo_half, zero, rolled).astype(_BF16)
        dst_ref[2, BLOCK:, :] = jnp.where(lo_half, rolled, zero).astype(_BF16)
        dst_ref[3, BLOCK:, :] = jnp.where(lo_half, zero, src).astype(_BF16)

    lo_out = lax.broadcasted_iota(jnp.int32, (2 * BLOCK, 2 * HEAD_DIM), 1) < HEAD_DIM
    band_cache = {}

    def band_terms(j):
        if j not in band_cache:
            qcol = jnp.broadcast_to(pos_ref[0, j], (BLOCK, BLOCK)).T
            prev_pos = kpos_ref[...] if j == 0 else pos_ref[0, j - 1]
            delta = jnp.concatenate([qcol - prev_pos, qcol - pos_ref[0, j]], axis=1)
            band_cache[j] = ((delta >= 0) & (delta < WINDOW), delta.astype(_F32))
        return band_cache[j]

    def scores(j, g):
        rows = slice(j * BLOCK, (j + 1) * BLOCK)
        band = slice(j * BLOCK, (j + 2) * BLOCK)
        qg = jnp.concatenate([q[rows, (2 * g) * LANES:(2 * g + 1) * LANES],
                              q[rows, (2 * g + 1) * LANES:(2 * g + 2) * LANES]], axis=0)
        return [_dot_nt(qg, kmask_ref[2 * g + a, band, :]) for a in range(2)]

    def attend(j, g, s_pair, anchors):
        rows = slice(j * BLOCK, (j + 1) * BLOCK)
        band = slice(j * BLOCK, (j + 2) * BLOCK)
        allowed, delta_f = band_terms(j)
        probs = []
        inv_l = []
        for a in range(2):
            s = s_pair[a]
            e_parts = []
            l_parts = []
            for r in range(2):
                head = 2 * (2 * g + r) + a
                sr = s[r * BLOCK:(r + 1) * BLOCK]
                sr = jnp.where(allowed, sr - ALIBI_SLOPES[head] * delta_f, NEG_BIG)
                sink = sinks_ref[0, head]
                m = jnp.maximum(jnp.max(sr, axis=1, keepdims=True), sink)
                e = jnp.exp(sr - m)
                l_parts.append(jnp.sum(e, axis=1, keepdims=True) + jnp.exp(sink - m))
                e_parts.append(e.astype(_BF16))
            p_a = jnp.concatenate(e_parts, axis=0)
            if a < len(anchors):
                p_a = _after(p_a, anchors[a])
            probs.append(p_a)
            inv_l.append(1.0 / jnp.concatenate(l_parts, axis=0))
        o = (_dot(probs[0], vmask_ref[2 * g, band, :])
             + _dot(probs[1], vmask_ref[2 * g + 1, band, :]))
        o = o * jnp.where(lo_out, inv_l[0], inv_l[1])
        attn_ref[rows, (2 * g) * LANES:(2 * g + 1) * LANES] = o[:BLOCK]
        attn_ref[rows, (2 * g + 1) * LANES:(2 * g + 2) * LANES] = o[BLOCK:]

    n_uc = (_QX0 - _UC0) // MXU_COLS
    attn_todo = [(j, g) for j in range(n_blk) for g in range(N_KV_HEADS)]
    s_all = [scores(j, g) for j, g in attn_todo]
    uc = [_dot(_after(h, s_all[i % len(attn_todo)][0]), w_in_ref[:, _UC0 + i * MXU_COLS:_UC0 + (i + 1) * MXU_COLS])
          for i in range(n_uc)]
    for i, (j, g) in enumerate(attn_todo):
        attend(j, g, s_all[i], uc[i::len(attn_todo)])

    kmask_ref[:, 0:BLOCK, :] = kmask_ref[:, ts:ts + BLOCK, :]
    vmask_ref[:, 0:BLOCK, :] = vmask_ref[:, ts:ts + BLOCK, :]
    kpos_ref[...] = pos_ref[0, n_blk - 1]

    n_glu = CONV_W // MXU_COLS
    for c in range(n_glu):
        a_c = ((uc[c] + b_glu_ref[:, cols(c)])
               * _sigmoid(uc[n_glu + c] + b_glu_ref[:, CONV_W + c * MXU_COLS:CONV_W + (c + 1) * MXU_COLS]))
        for i in range(MXU_COLS // LANES):
            conv_ref[c * (MXU_COLS // LANES) + i, CONV_CARRY:, :] = a_c[:, i * LANES:(i + 1) * LANES]
    g_conv = jnp.concatenate(uc[2 * n_glu:], axis=1)

    first_tap_row = CONV_CARRY - (CONV_K - 1)
    conv_out = {}

    def conv_piece(lc, r0):
        lanes = slice(lc * LANES, (lc + 1) * LANES)
        acc = jnp.broadcast_to(b_dw_ref[:, lanes], (CONV_ROWS, LANES))
        for t in range(CONV_K):
            off = r0 + first_tap_row + t
            acc = acc + w_dw_ref[t:t + 1, lanes] * conv_ref[lc, off:off + CONV_ROWS, :]
        conv_out[(lc, r0)] = acc

    conv_act = {}

    def conv_norm(r0):
        c = jnp.concatenate([conv_out.pop((lc, r0)) for lc in range(CONV_W // LANES)], axis=1)
        mu = jnp.mean(c, axis=-1, keepdims=True)
        cc = c - mu
        var = jnp.mean(cc * cc, axis=-1, keepdims=True)
        cn = cc * lax.rsqrt(var + LN_EPS) * ln_g_ref[...] + ln_b_ref[...]
        conv_act[r0] = (_silu(cn) * _silu(g_conv[r0:r0 + CONV_ROWS])).astype(_BF16)

    cross_out = {}
    qx = {}

    def cross_head(hh):
        per_chunk = MXU_COLS // CROSS_HEAD_DIM
        qh = qx[hh // per_chunk][:, (hh % per_chunk) * CROSS_HEAD_DIM:(hh % per_chunk + 1) * CROSS_HEAD_DIM]
        kv_cols = slice(hh * CROSS_HEAD_DIM, (hh + 1) * CROSS_HEAD_DIM)
        s = _dot_nt(qh.astype(_BF16), km_ref[:, kv_cols]) * (1.0 / math.sqrt(CROSS_HEAD_DIM))
        m = jnp.max(s, axis=1, keepdims=True)
        e = jnp.exp(s - m)
        l = jnp.sum(e, axis=1, keepdims=True)
        cross_out[hh] = _dot(e.astype(_BF16), vm_ref[:, kv_cols]) * (1.0 / l)

    vec_todo = []
    for r0 in range(0, ts, CONV_ROWS):
        vec_todo += [(conv_piece, (lc, r0)) for lc in range(CONV_W // LANES)]
        vec_todo.append((conv_norm, (r0,)))
    n_qx = (_MG0 - _QX0) // MXU_COLS
    gates = []
    for i in range(n_qx + 3 * n_col):
        if i < n_qx:
            qx[i] = proj(_QX0 + i * MXU_COLS)
            if i == CROSS_W // MXU_COLS - 1:
                vec_todo[2:2] = [(cross_head, (hh,)) for hh in range(N_CROSS_HEADS)]
        else:
            gates.append(_sigmoid(proj(_MG0 + (i - n_qx) * MXU_COLS)))
        if vec_todo:
            fn, args = vec_todo.pop(0)
            fn(*args)
    while vec_todo:
        fn, args = vec_todo.pop(0)
        fn(*args)
    conv_ref[:, 0:CONV_CARRY, :] = conv_ref[:, ts:ts + CONV_CARRY, :]

    g_cross = jnp.concatenate([qx[i] for i in range(CROSS_W // MXU_COLS, n_qx)], axis=1)
    attn_act = (attn_ref[...] * _silu(g_attn)).astype(_BF16)
    conv_all = jnp.concatenate([conv_act[r0] for r0 in range(0, ts, CONV_ROWS)], axis=0)
    cross_act = (jnp.concatenate([cross_out[hh] for hh in range(N_CROSS_HEADS)], axis=1)
                 * _silu(g_cross)).astype(_BF16)
    merged = []
    wide = 2 * MXU_COLS
    for c in range(D_MODEL // wide):
        wcols = slice(c * wide, (c + 1) * wide)
        gate = [jnp.concatenate(gates[br * n_col + 2 * c:br * n_col + 2 * c + 2], axis=1) for br in range(3)]
        y_attn = _dot(attn_act, w_oa_ref[:, wcols])
        y_conv = _dot(conv_all, w_pw_ref[:, wcols]) + b_pw_ref[:, wcols]
        y_cross = _dot(cross_act, w_oc_ref[:, wcols])
        merged.append((gate[0] * y_attn + gate[1] * y_conv + gate[2] * y_cross).astype(_BF16))
    merged = jnp.concatenate(merged, axis=1)

    x_new = jnp.concatenate([x[:, c * wide:(c + 1) * wide] + _dot(merged, w_out_ref[:, c * wide:(c + 1) * wide])
                             for c in range(D_MODEL // wide)], axis=1)
    out_ref[0] = _rmsnorm(x_new, fng_ref[...])


def _const_spec(shape):
    nd = len(shape)
    return pl.BlockSpec(shape, lambda b, s: (0,) * nd, pipeline_mode=pl.Buffered(1))


@jax.jit
def kernel(x, mem, positions, norm_g, w_in, attn_sinks, w_o_attn, b_glu, w_dw, b_dw, ln_g, ln_b,
           w_pw, b_pw, mem_norm_g, w_mem_kv, w_o_cross, w_out, final_norm_g):
    batch, seq, d_model = x.shape
    assert d_model == D_MODEL and norm_g.shape[0] == 1 and w_in.shape == (1, D_MODEL, IN_W)
    assert seq % SEQ_TILE == 0 and SEQ_TILE % BLOCK == 0 and mem.shape[1] == MEM_LEN
    ts = SEQ_TILE
    n_blk_tile = ts // BLOCK

    hbm = pl.BlockSpec(memory_space=pl.ANY)
    in_specs = [
        pl.BlockSpec(memory_space=pltpu.SMEM),
        pl.BlockSpec((1, ts, D_MODEL), lambda b, s: (b, s, 0)),
        pl.BlockSpec((1, n_blk_tile, 1, BLOCK), lambda b, s: (b, s, 0, 0)),
        pl.BlockSpec((1, MEM_LEN, D_MODEL), lambda b, s: (b, 0, 0)),
        _const_spec((1, D_MODEL)),
        _const_spec((1, D_MODEL)),
        hbm, hbm, hbm, hbm, hbm, hbm,
        _const_spec((1, 2 * CONV_W)),
        _const_spec((CONV_K, CONV_W)),
        _const_spec((1, CONV_W)),
        _const_spec((1, CONV_W)),
        _const_spec((1, CONV_W)),
        _const_spec((1, D_MODEL)),
        _const_spec((1, D_MODEL)),
    ]
    return pl.pallas_call(
        _trunk_kernel,
        grid=(batch, seq // ts),
        in_specs=in_specs,
        out_specs=pl.BlockSpec((1, ts, D_MODEL), lambda b, s: (b, s, 0)),
        out_shape=jax.ShapeDtypeStruct((batch, seq, D_MODEL), x.dtype),
        scratch_shapes=[
            pltpu.VMEM((4, ts + BLOCK, KV_W), _BF16),
            pltpu.VMEM((4, ts + BLOCK, KV_W), _BF16),
            pltpu.VMEM((1, BLOCK), jnp.int32),
            pltpu.VMEM((ts, ATTN_W), _F32),
            pltpu.VMEM((CONV_W // LANES, ts + CONV_CARRY, LANES), _F32),
            pltpu.VMEM((D_MODEL, IN_W), _BF16),
            pltpu.VMEM((D_MODEL, D_MODEL), _BF16),
            pltpu.VMEM((ATTN_W, D_MODEL), _BF16),
            pltpu.VMEM((CONV_W, D_MODEL), _BF16),
            pltpu.VMEM((CROSS_W, D_MODEL), _BF16),
            pltpu.VMEM((D_MODEL, 2 * CROSS_W), _BF16),
            pltpu.VMEM((MEM_LEN, CROSS_W), _BF16),
            pltpu.VMEM((MEM_LEN, CROSS_W), _BF16),
        ],
        compiler_params=pltpu.CompilerParams(
            dimension_semantics=("arbitrary", "arbitrary"),
            vmem_limit_bytes=VMEM_LIMIT_BYTES),
        name="trunk",
    )(attn_sinks, x, positions.reshape(batch, seq // BLOCK, 1, BLOCK), mem, mem_norm_g, norm_g,
      w_in[0], w_out[0], w_o_attn[0], w_pw[0], w_o_cross[0], w_mem_kv[0], b_glu,
      w_dw.reshape(CONV_K, CONV_W), b_dw, ln_g, ln_b, b_pw, final_norm_g.reshape(1, D_MODEL))
```

```python
import math

import jax
import jax.numpy as jnp
from jax import lax
from jax.experimental import pallas as pl
from jax.experimental.pallas import tpu as pltpu

D_MODEL = 1024
MEM_LEN = 256
HEAD_DIM = 64
N_Q_HEADS = 8
N_KV_HEADS = 2
WINDOW = 128
BLOCK = 128
ATTN_W = N_Q_HEADS * HEAD_DIM
KV_W = N_KV_HEADS * HEAD_DIM
CONV_W = D_MODEL // 2
CONV_K = 31
N_CROSS_HEADS = 4
CROSS_HEAD_DIM = 128
CROSS_W = N_CROSS_HEADS * CROSS_HEAD_DIM
RMS_EPS = 1e-6
LN_EPS = 1e-5
NEG_BIG = -1e30
ALIBI_SLOPES = tuple(2.0 ** (-8.0 * (i + 1) / N_Q_HEADS) for i in range(N_Q_HEADS))

_Q0 = 0
_K0 = _Q0 + ATTN_W
_V0 = _K0 + KV_W
_GA0 = _V0 + KV_W
_UC0 = _GA0 + ATTN_W
_GC0 = _UC0 + 2 * CONV_W
_QX0 = _GC0 + CONV_W
_GX0 = _QX0 + CROSS_W
_MG0 = _GX0 + CROSS_W
IN_W = _MG0 + 3 * D_MODEL

LANES = 128
VREG_ROW_BYTES = 32
MXU_COLS = 256
SEQ_TILE = 512
CONV_CARRY = 32
CONV_ROWS = 128
OUT_CHUNKS = 4
LOAD_SLOTS = 8
LOAD_ROWS = 32
LOAD_ROWS_SMALL = 128
VMEM_LIMIT_BYTES = 56 * 1024 * 1024

_BF16 = jnp.bfloat16
_F32 = jnp.float32


def _dot(a, b):
    return jnp.dot(a, b, preferred_element_type=_F32)


def _dot_nt(a, b):
    return lax.dot_general(a, b, (((1,), (1,)), ((), ())), preferred_element_type=_F32)


def _rmsnorm(xf, g):
    y = xf * lax.rsqrt(jnp.mean(xf * xf, axis=-1, keepdims=True) + RMS_EPS)
    return y * g


def _sigmoid(x):
    return jax.nn.sigmoid(x)


def _silu(x):
    return x * jax.nn.sigmoid(x)


def _after(value, anchor):
    rows = VREG_ROW_BYTES // value.dtype.itemsize
    never = pl.program_id(0) < 0
    head = jnp.where(never, anchor[:rows, :LANES].astype(value.dtype), value[:rows, :LANES])
    top = jnp.concatenate([head, value[:rows, LANES:]], axis=1)
    return jnp.concatenate([top, value[rows:]], axis=0)


def _load_as_bf16(src_hbm, dst_ref, rows):
    n_chunks = src_hbm.shape[0] // rows

    def load(stage_ref, sem_ref):
        def chunk_copy(i, slot):
            return pltpu.make_async_copy(src_hbm.at[pl.ds(i * rows, rows)], stage_ref.at[slot],
                                         sem_ref.at[slot])

        for i in range(min(LOAD_SLOTS, n_chunks)):
            chunk_copy(i, i).start()

        def body(i, carry):
            slot = i % LOAD_SLOTS
            chunk_copy(i, slot).wait()
            dst_ref[pl.ds(pl.multiple_of(i * rows, rows), rows), :] = stage_ref[slot].astype(_BF16)

            @pl.when(i + LOAD_SLOTS < n_chunks)
            def _():
                chunk_copy(i + LOAD_SLOTS, slot).start()

            return carry

        lax.fori_loop(0, n_chunks, body, 0)

    pl.run_scoped(load, pltpu.VMEM((LOAD_SLOTS, rows, src_hbm.shape[1]), _F32),
                  pltpu.SemaphoreType.DMA((LOAD_SLOTS,)))


def _load_many_as_bf16(pairs, rows):
    chunks = [(src, dst, r0) for src, dst in pairs for r0 in range(0, src.shape[0], rows)]
    width = pairs[0][0].shape[1]

    def load(stage_ref, sem_ref):
        def chunk_copy(k):
            src, _, r0 = chunks[k]
            slot = k % LOAD_SLOTS
            return pltpu.make_async_copy(src.at[pl.ds(r0, rows)], stage_ref.at[slot], sem_ref.at[slot])

        for k in range(min(LOAD_SLOTS, len(chunks))):
            chunk_copy(k).start()
        for k, (_, dst, r0) in enumerate(chunks):
            chunk_copy(k).wait()
            dst[pl.ds(r0, rows), :] = stage_ref[k % LOAD_SLOTS].astype(_BF16)
            if k + LOAD_SLOTS < len(chunks):
                chunk_copy(k + LOAD_SLOTS).start()

    pl.run_scoped(load, pltpu.VMEM((LOAD_SLOTS, rows, width), _F32),
                  pltpu.SemaphoreType.DMA((LOAD_SLOTS,)))


def _trunk_kernel(sinks_ref, x_ref, pos_ref, mem_ref, mem_g_ref, norm_g_ref, w_in_hbm,
                  w_out_hbm, w_oa_hbm, w_pw_hbm, w_oc_hbm, w_mkv_hbm, b_glu_ref, w_dw_ref, b_dw_ref,
                  ln_g_ref, ln_b_ref, b_pw_ref, fng_ref, out_ref,
                  kmask_ref, vmask_ref, kpos_ref, attn_ref, conv_ref,
                  w_in_ref, w_out_ref, w_oa_ref, w_pw_ref, w_oc_ref, w_mkv_ref, km_ref, vm_ref):
    ts = x_ref.shape[1]
    n_blk = ts // BLOCK
    n_col = D_MODEL // MXU_COLS
    seq_step = pl.program_id(1)

    @pl.when((pl.program_id(0) == 0) & (seq_step == 0))
    def _():
        _load_as_bf16(w_in_hbm, w_in_ref, LOAD_ROWS)
        _load_many_as_bf16([(w_mkv_hbm, w_mkv_ref), (w_out_hbm, w_out_ref), (w_oa_hbm, w_oa_ref),
                            (w_pw_hbm, w_pw_ref), (w_oc_hbm, w_oc_ref)], LOAD_ROWS_SMALL)

    @pl.when(seq_step == 0)
    def _():
        kmask_ref[:, 0:BLOCK, :] = jnp.zeros((4, BLOCK, KV_W), _BF16)
        vmask_ref[:, 0:BLOCK, :] = jnp.zeros((4, BLOCK, KV_W), _BF16)
        conv_ref[:, 0:CONV_CARRY, :] = jnp.zeros((CONV_W // LANES, CONV_CARRY, LANES), _F32)
        kpos_ref[...] = jnp.broadcast_to(pos_ref[0, 0][:, 0:1] - (WINDOW + 1), (1, BLOCK))
        mem_n = _rmsnorm(mem_ref[0], mem_g_ref[...]).astype(_BF16)
        mem_kv = _dot(mem_n, w_mkv_ref[...])
        km_ref[...] = mem_kv[:, :CROSS_W].astype(_BF16)
        vm_ref[...] = mem_kv[:, CROSS_W:].astype(_BF16)

    x = x_ref[0]
    h = _rmsnorm(x, norm_g_ref[...]).astype(_BF16)

    def proj(c0, width=MXU_COLS):
        return _dot(h, w_in_ref[:, c0:c0 + width])

    def cols(c):
        return slice(c * MXU_COLS, (c + 1) * MXU_COLS)

    qkvg = proj(_Q0, _UC0 - _Q0)
    q = (qkvg[:, _Q0:_K0] * (1.0 / math.sqrt(HEAD_DIM))).astype(_BF16)
    k = qkvg[:, _K0:_V0]
    v = qkvg[:, _V0:_GA0]
    g_attn = qkvg[:, _GA0:_UC0]

    lo_half = lax.broadcasted_iota(jnp.int32, (ts, KV_W), 1) < HEAD_DIM
    zero = jnp.zeros((ts, KV_W), _F32)
    for src, dst_ref in ((k, kmask_ref), (v, vmask_ref)):
        rolled = pltpu.roll(src, HEAD_DIM, 1)
        dst_ref[0, BLOCK:, :] = jnp.where(lo_half, src, zero).astype(_BF16)
        dst_ref[1, BLOCK:, :] = jnp.where(lo_half, zero, rolled).astype(_BF16)
        dst_ref[2, BLOCK:, :] = jnp.where(lo_half, rolled, zero).astype(_BF16)
        dst_ref[3, BLOCK:, :] = jnp.where(lo_half, zero, src).astype(_BF16)

    lo_out = lax.broadcasted_iota(jnp.int32, (2 * BLOCK, 2 * HEAD_DIM), 1) < HEAD_DIM
    band_cache = {}

    def band_terms(j):
        if j not in band_cache:
            qcol = jnp.broadcast_to(pos_ref[0, j], (BLOCK, BLOCK)).T
            prev_pos = kpos_ref[...] if j == 0 else pos_ref[0, j - 1]
            delta = jnp.concatenate([qcol - prev_pos, qcol - pos_ref[0, j]], axis=1)
            band_cache[j] = ((delta >= 0) & (delta < WINDOW), delta.astype(_F32))
        return band_cache[j]

    def scores(j, g):
        rows = slice(j * BLOCK, (j + 1) * BLOCK)
        band = slice(j * BLOCK, (j + 2) * BLOCK)
        qg = jnp.concatenate([q[rows, (2 * g) * LANES:(2 * g + 1) * LANES],
                              q[rows, (2 * g + 1) * LANES:(2 * g + 2) * LANES]], axis=0)
        return [_dot_nt(qg, kmask_ref[2 * g + a, band, :]) for a in range(2)]

    def attend(j, g, s_pair, anchors):
        rows = slice(j * BLOCK, (j + 1) * BLOCK)
        band = slice(j * BLOCK, (j + 2) * BLOCK)
        allowed, delta_f = band_terms(j)
        probs = []
        inv_l = []
        for a in range(2):
            s = s_pair[a]
            e_parts = []
            l_parts = []
            for r in range(2):
                head = 2 * (2 * g + r) + a
                sr = s[r * BLOCK:(r + 1) * BLOCK]
                sr = jnp.where(allowed, sr - ALIBI_SLOPES[head] * delta_f, NEG_BIG)
                sink = sinks_ref[0, head]
                m = jnp.maximum(jnp.max(sr, axis=1, keepdims=True), sink)
                e = jnp.exp(sr - m)
                l_parts.append(jnp.sum(e, axis=1, keepdims=True) + jnp.exp(sink - m))
                e_parts.append(e.astype(_BF16))
            p_a = jnp.concatenate(e_parts, axis=0)
            if a < len(anchors):
                p_a = _after(p_a, anchors[a])
            probs.append(p_a)
            inv_l.append(1.0 / jnp.concatenate(l_parts, axis=0))
        o = (_dot(probs[0], vmask_ref[2 * g, band, :])
             + _dot(probs[1], vmask_ref[2 * g + 1, band, :]))
        o = o * jnp.where(lo_out, inv_l[0], inv_l[1])
        attn_ref[rows, (2 * g) * LANES:(2 * g + 1) * LANES] = o[:BLOCK]
        attn_ref[rows, (2 * g + 1) * LANES:(2 * g + 2) * LANES] = o[BLOCK:]

    n_uc = (_QX0 - _UC0) // MXU_COLS
    attn_todo = [(j, g) for j in range(n_blk) for g in range(N_KV_HEADS)]
    s_all = [scores(j, g) for j, g in attn_todo]
    uc = [_dot(_after(h, s_all[i % len(attn_todo)][0]), w_in_ref[:, _UC0 + i * MXU_COLS:_UC0 + (i + 1) * MXU_COLS])
          for i in range(n_uc)]
    for i, (j, g) in enumerate(attn_todo):
        attend(j, g, s_all[i], uc[i::len(attn_todo)])

    kmask_ref[:, 0:BLOCK, :] = kmask_ref[:, ts:ts + BLOCK, :]
    vmask_ref[:, 0:BLOCK, :] = vmask_ref[:, ts:ts + BLOCK, :]
    kpos_ref[...] = pos_ref[0, n_blk - 1]

    n_glu = CONV_W // MXU_COLS
    for c in range(n_glu):
        a_c = ((uc[c] + b_glu_ref[:, cols(c)])
               * _sigmoid(uc[n_glu + c] + b_glu_ref[:, CONV_W + c * MXU_COLS:CONV_W + (c + 1) * MXU_COLS]))
        for i in range(MXU_COLS // LANES):
            conv_ref[c * (MXU_COLS // LANES) + i, CONV_CARRY:, :] = a_c[:, i * LANES:(i + 1) * LANES]
    g_conv = jnp.concatenate(uc[2 * n_glu:], axis=1)

    first_tap_row = CONV_CARRY - (CONV_K - 1)
    conv_out = {}

    def conv_piece(lc, r0):
        lanes = slice(lc * LANES, (lc + 1) * LANES)
        acc = jnp.broadcast_to(b_dw_ref[:, lanes], (CONV_ROWS, LANES))
        for t in range(CONV_K):
            off = r0 + first_tap_row + t
            acc = acc + w_dw_ref[t:t + 1, lanes] * conv_ref[lc, off:off + CONV_ROWS, :]
        conv_out[(lc, r0)] = acc

    conv_act = {}

    def conv_norm(r0):
        c = jnp.concatenate([conv_out.pop((lc, r0)) for lc in range(CONV_W // LANES)], axis=1)
        mu = jnp.mean(c, axis=-1, keepdims=True)
        cc = c - mu
        var = jnp.mean(cc * cc, axis=-1, keepdims=True)
        cn = cc * lax.rsqrt(var + LN_EPS) * ln_g_ref[...] + ln_b_ref[...]
        conv_act[r0] = (_silu(cn) * _silu(g_conv[r0:r0 + CONV_ROWS])).astype(_BF16)

    cross_out = {}
    qx = {}

    def cross_head(hh):
        per_chunk = MXU_COLS // CROSS_HEAD_DIM
        qh = qx[hh // per_chunk][:, (hh % per_chunk) * CROSS_HEAD_DIM:(hh % per_chunk + 1) * CROSS_HEAD_DIM]
        kv_cols = slice(hh * CROSS_HEAD_DIM, (hh + 1) * CROSS_HEAD_DIM)
        s = _dot_nt(qh.astype(_BF16), km_ref[:, kv_cols]) * (1.0 / math.sqrt(CROSS_HEAD_DIM))
        m = jnp.max(s, axis=1, keepdims=True)
        e = jnp.exp(s - m)
        l = jnp.sum(e, axis=1, keepdims=True)
        cross_out[hh] = _dot(e.astype(_BF16), vm_ref[:, kv_cols]) * (1.0 / l)

    vec_todo = []
    for r0 in range(0, ts, CONV_ROWS):
        vec_todo += [(conv_piece, (lc, r0)) for lc in range(CONV_W // LANES)]
        vec_todo.append((conv_norm, (r0,)))
    n_qx = (_MG0 - _QX0) // MXU_COLS
    gates = []
    for i in range(n_qx + 3 * n_col):
        if i < n_qx:
            qx[i] = proj(_QX0 + i * MXU_COLS)
            if i == CROSS_W // MXU_COLS - 1:
                vec_todo[2:2] = [(cross_head, (hh,)) for hh in range(N_CROSS_HEADS)]
        else:
            gates.append(_sigmoid(proj(_MG0 + (i - n_qx) * MXU_COLS)))
        if vec_todo:
            fn, args = vec_todo.pop(0)
            fn(*args)
    while vec_todo:
        fn, args = vec_todo.pop(0)
        fn(*args)
    conv_ref[:, 0:CONV_CARRY, :] = conv_ref[:, ts:ts + CONV_CARRY, :]

    g_cross = jnp.concatenate([qx[i] for i in range(CROSS_W // MXU_COLS, n_qx)], axis=1)
    attn_act = (attn_ref[...] * _silu(g_attn)).astype(_BF16)
    conv_all = jnp.concatenate([conv_act[r0] for r0 in range(0, ts, CONV_ROWS)], axis=0)
    cross_act = (jnp.concatenate([cross_out[hh] for hh in range(N_CROSS_HEADS)], axis=1)
                 * _silu(g_cross)).astype(_BF16)
    merged = []
    wide = OUT_CHUNKS * MXU_COLS
    for c in range(D_MODEL // wide):
        wcols = slice(c * wide, (c + 1) * wide)
        gate = [jnp.concatenate(gates[br * n_col + OUT_CHUNKS * c:br * n_col + OUT_CHUNKS * (c + 1)], axis=1)
                for br in range(3)]
        y_attn = _dot(attn_act, w_oa_ref[:, wcols])
        y_conv = _dot(conv_all, w_pw_ref[:, wcols]) + b_pw_ref[:, wcols]
        y_cross = _dot(cross_act, w_oc_ref[:, wcols])
        merged.append((gate[0] * y_attn + gate[1] * y_conv + gate[2] * y_cross).astype(_BF16))
    merged = jnp.concatenate(merged, axis=1)

    x_new = jnp.concatenate([x[:, c * wide:(c + 1) * wide] + _dot(merged, w_out_ref[:, c * wide:(c + 1) * wide])
                             for c in range(D_MODEL // wide)], axis=1)
    out_ref[0] = _rmsnorm(x_new, fng_ref[...])


def _const_spec(shape):
    nd = len(shape)
    return pl.BlockSpec(shape, lambda b, s: (0,) * nd, pipeline_mode=pl.Buffered(1))


@jax.jit
def kernel(x, mem, positions, norm_g, w_in, attn_sinks, w_o_attn, b_glu, w_dw, b_dw, ln_g, ln_b,
           w_pw, b_pw, mem_norm_g, w_mem_kv, w_o_cross, w_out, final_norm_g):
    batch, seq, d_model = x.shape
    assert d_model == D_MODEL and norm_g.shape[0] == 1 and w_in.shape == (1, D_MODEL, IN_W)
    assert seq % SEQ_TILE == 0 and SEQ_TILE % BLOCK == 0 and mem.shape[1] == MEM_LEN
    ts = SEQ_TILE
    n_blk_tile = ts // BLOCK

    hbm = pl.BlockSpec(memory_space=pl.ANY)
    in_specs = [
        pl.BlockSpec(memory_space=pltpu.SMEM),
        pl.BlockSpec((1, ts, D_MODEL), lambda b, s: (b, s, 0)),
        pl.BlockSpec((1, n_blk_tile, 1, BLOCK), lambda b, s: (b, s, 0, 0)),
        pl.BlockSpec((1, MEM_LEN, D_MODEL), lambda b, s: (b, 0, 0)),
        _const_spec((1, D_MODEL)),
        _const_spec((1, D_MODEL)),
        hbm, hbm, hbm, hbm, hbm, hbm,
        _const_spec((1, 2 * CONV_W)),
        _const_spec((CONV_K, CONV_W)),
        _const_spec((1, CONV_W)),
        _const_spec((1, CONV_W)),
        _const_spec((1, CONV_W)),
        _const_spec((1, D_MODEL)),
        _const_spec((1, D_MODEL)),
    ]
    return pl.pallas_call(
        _trunk_kernel,
        grid=(batch, seq // ts),
        in_specs=in_specs,
        out_specs=pl.BlockSpec((1, ts, D_MODEL), lambda b, s: (b, s, 0)),
        out_shape=jax.ShapeDtypeStruct((batch, seq, D_MODEL), x.dtype),
        scratch_shapes=[
            pltpu.VMEM((4, ts + BLOCK, KV_W), _BF16),
            pltpu.VMEM((4, ts + BLOCK, KV_W), _BF16),
            pltpu.VMEM((1, BLOCK), jnp.int32),
            pltpu.VMEM((ts, ATTN_W), _F32),
            pltpu.VMEM((CONV_W // LANES, ts + CONV_CARRY, LANES), _F32),
            pltpu.VMEM((D_MODEL, IN_W), _BF16),
            pltpu.VMEM((D_MODEL, D_MODEL), _BF16),
            pltpu.VMEM((ATTN_W, D_MODEL), _BF16),
            pltpu.VMEM((CONV_W, D_MODEL), _BF16),
            pltpu.VMEM((CROSS_W, D_MODEL), _BF16),
            pltpu.VMEM((D_MODEL, 2 * CROSS_W), _BF16),
            pltpu.VMEM((MEM_LEN, CROSS_W), _BF16),
            pltpu.VMEM((MEM_LEN, CROSS_W), _BF16),
        ],
        compiler_params=pltpu.CompilerParams(
            dimension_semantics=("arbitrary", "arbitrary"),
            vmem_limit_bytes=VMEM_LIMIT_BYTES),
        name="trunk",
    )(attn_sinks, x, positions.reshape(batch, seq // BLOCK, 1, BLOCK), mem, mem_norm_g, norm_g,
      w_in[0], w_out[0], w_o_attn[0], w_pw[0], w_o_cross[0], w_mem_kv[0], b_glu,
      w_dw.reshape(CONV_K, CONV_W), b_dw, ln_g, ln_b, b_pw, final_norm_g.reshape(1, D_MODEL))
```

```python
import math

import jax
import jax.numpy as jnp
from jax import lax
from jax.experimental import pallas as pl
from jax.experimental.pallas import tpu as pltpu

D_MODEL = 1024
MEM_LEN = 256
HEAD_DIM = 64
N_Q_HEADS = 8
N_KV_HEADS = 2
WINDOW = 128
BLOCK = 128
ATTN_W = N_Q_HEADS * HEAD_DIM
KV_W = N_KV_HEADS * HEAD_DIM
CONV_W = D_MODEL // 2
CONV_K = 31
N_CROSS_HEADS = 4
CROSS_HEAD_DIM = 128
CROSS_W = N_CROSS_HEADS * CROSS_HEAD_DIM
RMS_EPS = 1e-6
LN_EPS = 1e-5
NEG_BIG = -1e30
ALIBI_SLOPES = tuple(2.0 ** (-8.0 * (i + 1) / N_Q_HEADS) for i in range(N_Q_HEADS))

_Q0 = 0
_K0 = _Q0 + ATTN_W
_V0 = _K0 + KV_W
_GA0 = _V0 + KV_W
_UC0 = _GA0 + ATTN_W
_GC0 = _UC0 + 2 * CONV_W
_QX0 = _GC0 + CONV_W
_GX0 = _QX0 + CROSS_W
_MG0 = _GX0 + CROSS_W
IN_W = _MG0 + 3 * D_MODEL

LANES = 128
VREG_ROW_BYTES = 32
MXU_COLS = 256
SEQ_TILE = 512
CONV_CARRY = 32
CONV_ROWS = 128
OUT_CHUNKS = 4
LOAD_SLOTS = 8
LOAD_ROWS = 32
LOAD_ROWS_SMALL = 128
VMEM_LIMIT_BYTES = 56 * 1024 * 1024

_BF16 = jnp.bfloat16
_F32 = jnp.float32


def _dot(a, b):
    return jnp.dot(a, b, preferred_element_type=_F32)


def _dot_nt(a, b):
    return lax.dot_general(a, b, (((1,), (1,)), ((), ())), preferred_element_type=_F32)


def _rmsnorm(xf, g):
    y = xf * lax.rsqrt(jnp.mean(xf * xf, axis=-1, keepdims=True) + RMS_EPS)
    return y * g


def _sigmoid(x):
    return jax.nn.sigmoid(x)


def _silu(x):
    return x * jax.nn.sigmoid(x)


def _after(value, anchor):
    rows = VREG_ROW_BYTES // value.dtype.itemsize
    never = pl.program_id(0) < 0
    head = jnp.where(never, anchor[:rows, :LANES].astype(value.dtype), value[:rows, :LANES])
    top = jnp.concatenate([head, value[:rows, LANES:]], axis=1)
    return jnp.concatenate([top, value[rows:]], axis=0)


def _load_as_bf16(src_hbm, dst_ref, rows):
    n_chunks = src_hbm.shape[0] // rows

    def load(stage_ref, sem_ref):
        def chunk_copy(i, slot):
            return pltpu.make_async_copy(src_hbm.at[pl.ds(i * rows, rows)], stage_ref.at[slot],
                                         sem_ref.at[slot])

        for i in range(min(LOAD_SLOTS, n_chunks)):
            chunk_copy(i, i).start()

        def body(i, carry):
            slot = i % LOAD_SLOTS
            chunk_copy(i, slot).wait()
            dst_ref[pl.ds(pl.multiple_of(i * rows, rows), rows), :] = stage_ref[slot].astype(_BF16)

            @pl.when(i + LOAD_SLOTS < n_chunks)
            def _():
                chunk_copy(i + LOAD_SLOTS, slot).start()

            return carry

        lax.fori_loop(0, n_chunks, body, 0)

    pl.run_scoped(load, pltpu.VMEM((LOAD_SLOTS, rows, src_hbm.shape[1]), _F32),
                  pltpu.SemaphoreType.DMA((LOAD_SLOTS,)))


def _load_many_as_bf16(pairs, rows):
    chunks = [(src, dst, r0) for src, dst in pairs for r0 in range(0, src.shape[0], rows)]
    width = pairs[0][0].shape[1]

    def load(stage_ref, sem_ref):
        def chunk_copy(k):
            src, _, r0 = chunks[k]
            slot = k % LOAD_SLOTS
            return pltpu.make_async_copy(src.at[pl.ds(r0, rows)], stage_ref.at[slot], sem_ref.at[slot])

        for k in range(min(LOAD_SLOTS, len(chunks))):
            chunk_copy(k).start()
        for k, (_, dst, r0) in enumerate(chunks):
            chunk_copy(k).wait()
            dst[pl.ds(r0, rows), :] = stage_ref[k % LOAD_SLOTS].astype(_BF16)
            if k + LOAD_SLOTS < len(chunks):
                chunk_copy(k + LOAD_SLOTS).start()

    pl.run_scoped(load, pltpu.VMEM((LOAD_SLOTS, rows, width), _F32),
                  pltpu.SemaphoreType.DMA((LOAD_SLOTS,)))


def _start_row(pos_ref, mem_ref, mem_g_ref, w_mkv_ref, kmask_ref, vmask_ref, kpos_ref, conv_ref,
               km_ref, vm_ref):
    kmask_ref[:, 0:BLOCK, :] = jnp.zeros((4, BLOCK, KV_W), _BF16)
    vmask_ref[:, 0:BLOCK, :] = jnp.zeros((4, BLOCK, KV_W), _BF16)
    conv_ref[:, 0:CONV_CARRY, :] = jnp.zeros((CONV_W // LANES, CONV_CARRY, LANES), _F32)
    kpos_ref[...] = jnp.broadcast_to(pos_ref[0, 0][:, 0:1] - (WINDOW + 1), (1, BLOCK))
    mem_n = _rmsnorm(mem_ref[0], mem_g_ref[...]).astype(_BF16)
    mem_kv = _dot(mem_n, w_mkv_ref[...])
    km_ref[...] = mem_kv[:, :CROSS_W].astype(_BF16)
    vm_ref[...] = mem_kv[:, CROSS_W:].astype(_BF16)


def _tile(sinks_ref, x_ref, pos_ref, norm_g_ref, b_glu_ref, w_dw_ref, b_dw_ref, ln_g_ref, ln_b_ref,
          b_pw_ref, fng_ref, out_ref, kmask_ref, vmask_ref, kpos_ref, attn_ref, conv_ref,
          w_in_ref, w_out_ref, w_oa_ref, w_pw_ref, w_oc_ref, km_ref, vm_ref):
    ts = x_ref.shape[1]
    n_blk = ts // BLOCK
    n_col = D_MODEL // MXU_COLS

    x = x_ref[0]
    h = _rmsnorm(x, norm_g_ref[...]).astype(_BF16)

    def proj(c0, width=MXU_COLS):
        return _dot(h, w_in_ref[:, c0:c0 + width])

    def cols(c):
        return slice(c * MXU_COLS, (c + 1) * MXU_COLS)

    qkvg = proj(_Q0, _UC0 - _Q0)
    q = (qkvg[:, _Q0:_K0] * (1.0 / math.sqrt(HEAD_DIM))).astype(_BF16)
    k = qkvg[:, _K0:_V0]
    v = qkvg[:, _V0:_GA0]
    g_attn = qkvg[:, _GA0:_UC0]

    lo_half = lax.broadcasted_iota(jnp.int32, (ts, KV_W), 1) < HEAD_DIM
    zero = jnp.zeros((ts, KV_W), _F32)
    for src, dst_ref in ((k, kmask_ref), (v, vmask_ref)):
        rolled = pltpu.roll(src, HEAD_DIM, 1)
        dst_ref[0, BLOCK:, :] = jnp.where(lo_half, src, zero).astype(_BF16)
        dst_ref[1, BLOCK:, :] = jnp.where(lo_half, zero, rolled).astype(_BF16)
        dst_ref[2, BLOCK:, :] = jnp.where(lo_half, rolled, zero).astype(_BF16)
        dst_ref[3, BLOCK:, :] = jnp.where(lo_half, zero, src).astype(_BF16)

    lo_out = lax.broadcasted_iota(jnp.int32, (2 * BLOCK, 2 * HEAD_DIM), 1) < HEAD_DIM
    band_cache = {}

    def band_terms(j):
        if j not in band_cache:
            qcol = jnp.broadcast_to(pos_ref[0, j], (BLOCK, BLOCK)).T
            prev_pos = kpos_ref[...] if j == 0 else pos_ref[0, j - 1]
            delta = jnp.concatenate([qcol - prev_pos, qcol - pos_ref[0, j]], axis=1)
            band_cache[j] = ((delta >= 0) & (delta < WINDOW), delta.astype(_F32))
        return band_cache[j]

    def scores(j, g):
        rows = slice(j * BLOCK, (j + 1) * BLOCK)
        band = slice(j * BLOCK, (j + 2) * BLOCK)
        qg = jnp.concatenate([q[rows, (2 * g) * LANES:(2 * g + 1) * LANES],
                              q[rows, (2 * g + 1) * LANES:(2 * g + 2) * LANES]], axis=0)
        return [_dot_nt(qg, kmask_ref[2 * g + a, band, :]) for a in range(2)]

    def attend(j, g, s_pair, anchors):
        rows = slice(j * BLOCK, (j + 1) * BLOCK)
        band = slice(j * BLOCK, (j + 2) * BLOCK)
        allowed, delta_f = band_terms(j)
        probs = []
        inv_l = []
        for a in range(2):
            s = s_pair[a]
            e_parts = []
            l_parts = []
            for r in range(2):
                head = 2 * (2 * g + r) + a
                sr = s[r * BLOCK:(r + 1) * BLOCK]
                sr = jnp.where(allowed, sr - ALIBI_SLOPES[head] * delta_f, NEG_BIG)
                sink = sinks_ref[0, head]
                m = jnp.maximum(jnp.max(sr, axis=1, keepdims=True), sink)
                e = jnp.exp(sr - m)
                l_parts.append(jnp.sum(e, axis=1, keepdims=True) + jnp.exp(sink - m))
                e_parts.append(e.astype(_BF16))
            p_a = jnp.concatenate(e_parts, axis=0)
            if a < len(anchors):
                p_a = _after(p_a, anchors[a])
            probs.append(p_a)
            inv_l.append(1.0 / jnp.concatenate(l_parts, axis=0))
        o = (_dot(probs[0], vmask_ref[2 * g, band, :])
             + _dot(probs[1], vmask_ref[2 * g + 1, band, :]))
        o = o * jnp.where(lo_out, inv_l[0], inv_l[1])
        attn_ref[rows, (2 * g) * LANES:(2 * g + 1) * LANES] = o[:BLOCK]
        attn_ref[rows, (2 * g + 1) * LANES:(2 * g + 2) * LANES] = o[BLOCK:]

    n_uc = (_QX0 - _UC0) // MXU_COLS
    attn_todo = [(j, g) for j in range(n_blk) for g in range(N_KV_HEADS)]
    s_all = [scores(j, g) for j, g in attn_todo]
    uc = [_dot(_after(h, s_all[i % len(attn_todo)][0]), w_in_ref[:, _UC0 + i * MXU_COLS:_UC0 + (i + 1) * MXU_COLS])
          for i in range(n_uc)]
    for i, (j, g) in enumerate(attn_todo):
        attend(j, g, s_all[i], uc[i::len(attn_todo)])

    kmask_ref[:, 0:BLOCK, :] = kmask_ref[:, ts:ts + BLOCK, :]
    vmask_ref[:, 0:BLOCK, :] = vmask_ref[:, ts:ts + BLOCK, :]
    kpos_ref[...] = pos_ref[0, n_blk - 1]

    n_glu = CONV_W // MXU_COLS
    for c in range(n_glu):
        a_c = ((uc[c] + b_glu_ref[:, cols(c)])
               * _sigmoid(uc[n_glu + c] + b_glu_ref[:, CONV_W + c * MXU_COLS:CONV_W + (c + 1) * MXU_COLS]))
        for i in range(MXU_COLS // LANES):
            conv_ref[c * (MXU_COLS // LANES) + i, CONV_CARRY:, :] = a_c[:, i * LANES:(i + 1) * LANES]
    g_conv = jnp.concatenate(uc[2 * n_glu:], axis=1)

    first_tap_row = CONV_CARRY - (CONV_K - 1)
    conv_out = {}

    def conv_piece(lc, r0):
        lanes = slice(lc * LANES, (lc + 1) * LANES)
        acc = jnp.broadcast_to(b_dw_ref[:, lanes], (CONV_ROWS, LANES))
        for t in range(CONV_K):
            off = r0 + first_tap_row + t
            acc = acc + w_dw_ref[t:t + 1, lanes] * conv_ref[lc, off:off + CONV_ROWS, :]
        conv_out[(lc, r0)] = acc

    conv_act = {}

    def conv_norm(r0):
        c = jnp.concatenate([conv_out.pop((lc, r0)) for lc in range(CONV_W // LANES)], axis=1)
        mu = jnp.mean(c, axis=-1, keepdims=True)
        cc = c - mu
        var = jnp.mean(cc * cc, axis=-1, keepdims=True)
        cn = cc * lax.rsqrt(var + LN_EPS) * ln_g_ref[...] + ln_b_ref[...]
        conv_act[r0] = (_silu(cn) * _silu(g_conv[r0:r0 + CONV_ROWS])).astype(_BF16)

    cross_out = {}
    qx = {}

    def cross_head(hh):
        per_chunk = MXU_COLS // CROSS_HEAD_DIM
        qh = qx[hh // per_chunk][:, (hh % per_chunk) * CROSS_HEAD_DIM:(hh % per_chunk + 1) * CROSS_HEAD_DIM]
        kv_cols = slice(hh * CROSS_HEAD_DIM, (hh + 1) * CROSS_HEAD_DIM)
        s = _dot_nt(qh.astype(_BF16), km_ref[:, kv_cols]) * (1.0 / math.sqrt(CROSS_HEAD_DIM))
        m = jnp.max(s, axis=1, keepdims=True)
        e = jnp.exp(s - m)
        l = jnp.sum(e, axis=1, keepdims=True)
        cross_out[hh] = _dot(e.astype(_BF16), vm_ref[:, kv_cols]) * (1.0 / l)

    vec_todo = []
    for r0 in range(0, ts, CONV_ROWS):
        vec_todo += [(conv_piece, (lc, r0)) for lc in range(CONV_W // LANES)]
        vec_todo.append((conv_norm, (r0,)))
    n_qx = (_MG0 - _QX0) // MXU_COLS
    gates = []
    for i in range(n_qx + 3 * n_col):
        if i < n_qx:
            qx[i] = proj(_QX0 + i * MXU_COLS)
            if i == CROSS_W // MXU_COLS - 1:
                vec_todo[2:2] = [(cross_head, (hh,)) for hh in range(N_CROSS_HEADS)]
        else:
            gates.append(_sigmoid(proj(_MG0 + (i - n_qx) * MXU_COLS)))
        if vec_todo:
            fn, args = vec_todo.pop(0)
            fn(*args)
    while vec_todo:
        fn, args = vec_todo.pop(0)
        fn(*args)
    conv_ref[:, 0:CONV_CARRY, :] = conv_ref[:, ts:ts + CONV_CARRY, :]

    g_cross = jnp.concatenate([qx[i] for i in range(CROSS_W // MXU_COLS, n_qx)], axis=1)
    attn_act = (attn_ref[...] * _silu(g_attn)).astype(_BF16)
    conv_all = jnp.concatenate([conv_act[r0] for r0 in range(0, ts, CONV_ROWS)], axis=0)
    cross_act = (jnp.concatenate([cross_out[hh] for hh in range(N_CROSS_HEADS)], axis=1)
                 * _silu(g_cross)).astype(_BF16)
    merged = []
    wide = OUT_CHUNKS * MXU_COLS
    for c in range(D_MODEL // wide):
        wcols = slice(c * wide, (c + 1) * wide)
        gate = [jnp.concatenate(gates[br * n_col + OUT_CHUNKS * c:br * n_col + OUT_CHUNKS * (c + 1)], axis=1)
                for br in range(3)]
        y_attn = _dot(attn_act, w_oa_ref[:, wcols])
        y_conv = _dot(conv_all, w_pw_ref[:, wcols]) + b_pw_ref[:, wcols]
        y_cross = _dot(cross_act, w_oc_ref[:, wcols])
        merged.append((gate[0] * y_attn + gate[1] * y_conv + gate[2] * y_cross).astype(_BF16))
    merged = jnp.concatenate(merged, axis=1)

    x_new = jnp.concatenate([x[:, c * wide:(c + 1) * wide] + _dot(merged, w_out_ref[:, c * wide:(c + 1) * wide])
                             for c in range(D_MODEL // wide)], axis=1)
    out_ref[0] = _rmsnorm(x_new, fng_ref[...])


def _trunk_kernel(sinks_ref, x_hbm, pos_ref, mem_hbm, mem_g_ref, norm_g_ref, w_in_hbm,
                  w_out_hbm, w_oa_hbm, w_pw_hbm, w_oc_hbm, w_mkv_hbm, b_glu_ref, w_dw_ref, b_dw_ref,
                  ln_g_ref, ln_b_ref, b_pw_ref, fng_ref, out_hbm,
                  kmask_ref, vmask_ref, kpos_ref, attn_ref, conv_ref,
                  w_in_ref, w_out_ref, w_oa_ref, w_pw_ref, w_oc_ref, w_mkv_ref, km_ref, vm_ref,
                  x_buf, out_buf, mem_buf, x_sem, out_sem, mem_sem):
    batch, seq, _ = x_hbm.shape
    ts = x_buf.shape[1]
    n_seq = seq // ts
    n_blk = ts // BLOCK
    n_tiles = batch * n_seq

    def tile_rows(t):
        if isinstance(t, int):
            return t // n_seq, pl.ds((t % n_seq) * ts, ts)
        return t // n_seq, pl.ds(pl.multiple_of((t % n_seq) * ts, ts), ts)

    def x_copy(t):
        b, rows = tile_rows(t)
        return pltpu.make_async_copy(x_hbm.at[b, rows], x_buf.at[t % 2], x_sem.at[t % 2])

    def out_copy(t):
        b, rows = tile_rows(t)
        return pltpu.make_async_copy(out_buf.at[t % 2], out_hbm.at[b, rows], out_sem.at[t % 2])

    def mem_copy(b):
        return pltpu.make_async_copy(mem_hbm.at[b], mem_buf.at[0], mem_sem.at[0])

    x_copy(0).start()
    mem_copy(0).start()
    _load_as_bf16(w_in_hbm, w_in_ref, LOAD_ROWS)
    _load_many_as_bf16([(w_mkv_hbm, w_mkv_ref), (w_out_hbm, w_out_ref), (w_oa_hbm, w_oa_ref),
                        (w_pw_hbm, w_pw_ref), (w_oc_hbm, w_oc_ref)], LOAD_ROWS_SMALL)

    def batch_row(b, carry):
        mem_copy(b).wait()
        _start_row(pos_ref.at[pl.ds(b, 1), pl.ds(0, n_blk)], mem_buf, mem_g_ref, w_mkv_ref,
                   kmask_ref, vmask_ref, kpos_ref, conv_ref, km_ref, vm_ref)

        @pl.when(b + 1 < batch)
        def _():
            mem_copy(b + 1).start()

        def tile(s, carry):
            t = b * n_seq + s

            @pl.when(t + 1 < n_tiles)
            def _():
                x_copy(t + 1).start()

            x_copy(t).wait()

            @pl.when(t >= 2)
            def _():
                out_copy(t - 2).wait()

            slot = t % 2
            _tile(sinks_ref, x_buf.at[pl.ds(slot, 1)],
                  pos_ref.at[pl.ds(b, 1), pl.ds(pl.multiple_of(s * n_blk, n_blk), n_blk)],
                  norm_g_ref, b_glu_ref, w_dw_ref, b_dw_ref, ln_g_ref, ln_b_ref, b_pw_ref, fng_ref,
                  out_buf.at[pl.ds(slot, 1)], kmask_ref, vmask_ref, kpos_ref, attn_ref, conv_ref,
                  w_in_ref, w_out_ref, w_oa_ref, w_pw_ref, w_oc_ref, km_ref, vm_ref)
            out_copy(t).start()
            return carry

        return lax.fori_loop(0, n_seq, tile, carry)

    lax.fori_loop(0, batch, batch_row, 0)
    out_copy(n_tiles - 2).wait()
    out_copy(n_tiles - 1).wait()


def _const_spec(shape):
    nd = len(shape)
    return pl.BlockSpec(shape, lambda i: (0,) * nd, pipeline_mode=pl.Buffered(1))


@jax.jit
def kernel(x, mem, positions, norm_g, w_in, attn_sinks, w_o_attn, b_glu, w_dw, b_dw, ln_g, ln_b,
           w_pw, b_pw, mem_norm_g, w_mem_kv, w_o_cross, w_out, final_norm_g):
    batch, seq, d_model = x.shape
    assert d_model == D_MODEL and norm_g.shape[0] == 1 and w_in.shape == (1, D_MODEL, IN_W)
    assert seq % SEQ_TILE == 0 and SEQ_TILE % BLOCK == 0 and mem.shape[1] == MEM_LEN
    ts = SEQ_TILE

    hbm = pl.BlockSpec(memory_space=pl.ANY)
    in_specs = [
        pl.BlockSpec(memory_space=pltpu.SMEM),
        hbm,
        pl.BlockSpec(memory_space=pltpu.VMEM),
        hbm,
        _const_spec((1, D_MODEL)),
        _const_spec((1, D_MODEL)),
        hbm, hbm, hbm, hbm, hbm, hbm,
        _const_spec((1, 2 * CONV_W)),
        _const_spec((CONV_K, CONV_W)),
        _const_spec((1, CONV_W)),
        _const_spec((1, CONV_W)),
        _const_spec((1, CONV_W)),
        _const_spec((1, D_MODEL)),
        _const_spec((1, D_MODEL)),
    ]
    return pl.pallas_call(
        _trunk_kernel,
        grid=(1,),
        in_specs=in_specs,
        out_specs=hbm,
        out_shape=jax.ShapeDtypeStruct((batch, seq, D_MODEL), x.dtype),
        scratch_shapes=[
            pltpu.VMEM((4, ts + BLOCK, KV_W), _BF16),
            pltpu.VMEM((4, ts + BLOCK, KV_W), _BF16),
            pltpu.VMEM((1, BLOCK), jnp.int32),
            pltpu.VMEM((ts, ATTN_W), _F32),
            pltpu.VMEM((CONV_W // LANES, ts + CONV_CARRY, LANES), _F32),
            pltpu.VMEM((D_MODEL, IN_W), _BF16),
            pltpu.VMEM((D_MODEL, D_MODEL), _BF16),
            pltpu.VMEM((ATTN_W, D_MODEL), _BF16),
            pltpu.VMEM((CONV_W, D_MODEL), _BF16),
            pltpu.VMEM((CROSS_W, D_MODEL), _BF16),
            pltpu.VMEM((D_MODEL, 2 * CROSS_W), _BF16),
            pltpu.VMEM((MEM_LEN, CROSS_W), _BF16),
            pltpu.VMEM((MEM_LEN, CROSS_W), _BF16),
            pltpu.VMEM((2, ts, D_MODEL), _F32),
            pltpu.VMEM((2, ts, D_MODEL), _F32),
            pltpu.VMEM((1, MEM_LEN, D_MODEL), _F32),
            pltpu.SemaphoreType.DMA((2,)),
            pltpu.SemaphoreType.DMA((2,)),
            pltpu.SemaphoreType.DMA((1,)),
        ],
        compiler_params=pltpu.CompilerParams(
            dimension_semantics=("arbitrary",),
            vmem_limit_bytes=VMEM_LIMIT_BYTES),
        name="trunk",
    )(attn_sinks, x, positions.reshape(batch, seq // BLOCK, 1, BLOCK), mem, mem_norm_g, norm_g,
      w_in[0], w_out[0], w_o_attn[0], w_pw[0], w_o_cross[0], w_mem_kv[0], b_glu,
      w_dw.reshape(CONV_K, CONV_W), b_dw, ln_g, ln_b, b_pw, final_norm_g.reshape(1, D_MODEL))
```

```python
import math

import jax
import jax.numpy as jnp
from jax import lax
from jax.experimental import pallas as pl
from jax.experimental.pallas import tpu as pltpu

D_MODEL = 1024
MEM_LEN = 256
HEAD_DIM = 64
N_Q_HEADS = 8
N_KV_HEADS = 2
WINDOW = 128
BLOCK = 128
ATTN_W = N_Q_HEADS * HEAD_DIM
KV_W = N_KV_HEADS * HEAD_DIM
CONV_W = D_MODEL // 2
CONV_K = 31
N_CROSS_HEADS = 4
CROSS_HEAD_DIM = 128
CROSS_W = N_CROSS_HEADS * CROSS_HEAD_DIM
RMS_EPS = 1e-6
LN_EPS = 1e-5
NEG_BIG = -1e30
ALIBI_SLOPES = tuple(2.0 ** (-8.0 * (i + 1) / N_Q_HEADS) for i in range(N_Q_HEADS))

_Q0 = 0
_K0 = _Q0 + ATTN_W
_V0 = _K0 + KV_W
_GA0 = _V0 + KV_W
_UC0 = _GA0 + ATTN_W
_GC0 = _UC0 + 2 * CONV_W
_QX0 = _GC0 + CONV_W
_GX0 = _QX0 + CROSS_W
_MG0 = _GX0 + CROSS_W
IN_W = _MG0 + 3 * D_MODEL

LANES = 128
VREG_ROW_BYTES = 32
MXU_COLS = 256
SEQ_TILE = 512
CONV_CARRY = 32
CONV_ROWS = 128
OUT_CHUNKS = 4
LOAD_SLOTS = 8
LOAD_ROWS = 32
LOAD_ROWS_SMALL = 128
VMEM_LIMIT_BYTES = 56 * 1024 * 1024

_BF16 = jnp.bfloat16
_F32 = jnp.float32


def _dot(a, b):
    return jnp.dot(a, b, preferred_element_type=_F32)


def _dot_nt(a, b):
    return lax.dot_general(a, b, (((1,), (1,)), ((), ())), preferred_element_type=_F32)


def _rmsnorm(xf, g):
    y = xf * lax.rsqrt(jnp.mean(xf * xf, axis=-1, keepdims=True) + RMS_EPS)
    return y * g


def _sigmoid(x):
    return jax.nn.sigmoid(x)


def _silu(x):
    return x * jax.nn.sigmoid(x)


def _after(value, anchor):
    rows = VREG_ROW_BYTES // value.dtype.itemsize
    never = pl.program_id(0) < 0
    head = jnp.where(never, anchor[:rows, :LANES].astype(value.dtype), value[:rows, :LANES])
    top = jnp.concatenate([head, value[:rows, LANES:]], axis=1)
    return jnp.concatenate([top, value[rows:]], axis=0)


def _load_as_bf16(src_hbm, dst_ref, rows):
    n_chunks = src_hbm.shape[0] // rows

    def load(stage_ref, sem_ref):
        def chunk_copy(i, slot):
            return pltpu.make_async_copy(src_hbm.at[pl.ds(i * rows, rows)], stage_ref.at[slot],
                                         sem_ref.at[slot])

        for i in range(min(LOAD_SLOTS, n_chunks)):
            chunk_copy(i, i).start()

        def body(i, carry):
            slot = i % LOAD_SLOTS
            chunk_copy(i, slot).wait()
            dst_ref[pl.ds(pl.multiple_of(i * rows, rows), rows), :] = stage_ref[slot].astype(_BF16)

            @pl.when(i + LOAD_SLOTS < n_chunks)
            def _():
                chunk_copy(i + LOAD_SLOTS, slot).start()

            return carry

        lax.fori_loop(0, n_chunks, body, 0)

    pl.run_scoped(load, pltpu.VMEM((LOAD_SLOTS, rows, src_hbm.shape[1]), _F32),
                  pltpu.SemaphoreType.DMA((LOAD_SLOTS,)))


def _load_many_as_bf16(pairs, rows):
    chunks = [(src, dst, r0) for src, dst in pairs for r0 in range(0, src.shape[0], rows)]
    width = pairs[0][0].shape[1]

    def load(stage_ref, sem_ref):
        def chunk_copy(k):
            src, _, r0 = chunks[k]
            slot = k % LOAD_SLOTS
            return pltpu.make_async_copy(src.at[pl.ds(r0, rows)], stage_ref.at[slot], sem_ref.at[slot])

        for k in range(min(LOAD_SLOTS, len(chunks))):
            chunk_copy(k).start()
        for k, (_, dst, r0) in enumerate(chunks):
            chunk_copy(k).wait()
            dst[pl.ds(r0, rows), :] = stage_ref[k % LOAD_SLOTS].astype(_BF16)
            if k + LOAD_SLOTS < len(chunks):
                chunk_copy(k + LOAD_SLOTS).start()

    pl.run_scoped(load, pltpu.VMEM((LOAD_SLOTS, rows, width), _F32),
                  pltpu.SemaphoreType.DMA((LOAD_SLOTS,)))


def _start_row(pos_ref, mem_ref, mem_g_ref, w_mkv_ref, kmask_ref, vmask_ref, kpos_ref, conv_ref,
               km_ref, vm_ref):
    kmask_ref[:, 0:BLOCK, :] = jnp.zeros((4, BLOCK, KV_W), _BF16)
    vmask_ref[:, 0:BLOCK, :] = jnp.zeros((4, BLOCK, KV_W), _BF16)
    conv_ref[:, 0:CONV_CARRY, :] = jnp.zeros((CONV_W // LANES, CONV_CARRY, LANES), _F32)
    kpos_ref[...] = jnp.broadcast_to(pos_ref[:, 0:1] - (WINDOW + 1), (1, BLOCK))
    mem_n = _rmsnorm(mem_ref[0], mem_g_ref[...]).astype(_BF16)
    mem_kv = _dot(mem_n, w_mkv_ref[...])
    km_ref[...] = mem_kv[:, :CROSS_W].astype(_BF16)
    vm_ref[...] = mem_kv[:, CROSS_W:].astype(_BF16)


def _tile(sinks_ref, x_ref, pos_ref, norm_g_ref, b_glu_ref, w_dw_ref, b_dw_ref, ln_g_ref, ln_b_ref,
          b_pw_ref, fng_ref, out_ref, kmask_ref, vmask_ref, kpos_ref, attn_ref, conv_ref,
          w_in_ref, w_out_ref, w_oa_ref, w_pw_ref, w_oc_ref, km_ref, vm_ref):
    ts = x_ref.shape[1]
    n_blk = ts // BLOCK
    n_col = D_MODEL // MXU_COLS

    x = x_ref[0]
    h = _rmsnorm(x, norm_g_ref[...]).astype(_BF16)

    def proj(c0, width=MXU_COLS):
        return _dot(h, w_in_ref[:, c0:c0 + width])

    def cols(c):
        return slice(c * MXU_COLS, (c + 1) * MXU_COLS)

    qkvg = proj(_Q0, _UC0 - _Q0)
    q = (qkvg[:, _Q0:_K0] * (1.0 / math.sqrt(HEAD_DIM))).astype(_BF16)
    k = qkvg[:, _K0:_V0]
    v = qkvg[:, _V0:_GA0]
    g_attn = qkvg[:, _GA0:_UC0]

    lo_half = lax.broadcasted_iota(jnp.int32, (ts, KV_W), 1) < HEAD_DIM
    zero = jnp.zeros((ts, KV_W), _F32)
    for src, dst_ref in ((k, kmask_ref), (v, vmask_ref)):
        rolled = pltpu.roll(src, HEAD_DIM, 1)
        dst_ref[0, BLOCK:, :] = jnp.where(lo_half, src, zero).astype(_BF16)
        dst_ref[1, BLOCK:, :] = jnp.where(lo_half, zero, rolled).astype(_BF16)
        dst_ref[2, BLOCK:, :] = jnp.where(lo_half, rolled, zero).astype(_BF16)
        dst_ref[3, BLOCK:, :] = jnp.where(lo_half, zero, src).astype(_BF16)

    lo_out = lax.broadcasted_iota(jnp.int32, (2 * BLOCK, 2 * HEAD_DIM), 1) < HEAD_DIM
    band_cache = {}

    def pos_blk(j):
        return pos_ref[:, j * BLOCK:(j + 1) * BLOCK]

    def band_terms(j):
        if j not in band_cache:
            qcol = jnp.broadcast_to(pos_blk(j), (BLOCK, BLOCK)).T
            prev_pos = kpos_ref[...] if j == 0 else pos_blk(j - 1)
            delta = jnp.concatenate([qcol - prev_pos, qcol - pos_blk(j)], axis=1)
            band_cache[j] = ((delta >= 0) & (delta < WINDOW), delta.astype(_F32))
        return band_cache[j]

    def scores(j, g):
        rows = slice(j * BLOCK, (j + 1) * BLOCK)
        band = slice(j * BLOCK, (j + 2) * BLOCK)
        qg = jnp.concatenate([q[rows, (2 * g) * LANES:(2 * g + 1) * LANES],
                              q[rows, (2 * g + 1) * LANES:(2 * g + 2) * LANES]], axis=0)
        return [_dot_nt(qg, kmask_ref[2 * g + a, band, :]) for a in range(2)]

    def attend(j, g, s_pair, anchors):
        rows = slice(j * BLOCK, (j + 1) * BLOCK)
        band = slice(j * BLOCK, (j + 2) * BLOCK)
        allowed, delta_f = band_terms(j)
        probs = []
        inv_l = []
        for a in range(2):
            s = s_pair[a]
            e_parts = []
            l_parts = []
            for r in range(2):
                head = 2 * (2 * g + r) + a
                sr = s[r * BLOCK:(r + 1) * BLOCK]
                sr = jnp.where(allowed, sr - ALIBI_SLOPES[head] * delta_f, NEG_BIG)
                sink = sinks_ref[0, head]
                m = jnp.maximum(jnp.max(sr, axis=1, keepdims=True), sink)
                e = jnp.exp(sr - m)
                l_parts.append(jnp.sum(e, axis=1, keepdims=True) + jnp.exp(sink - m))
                e_parts.append(e.astype(_BF16))
            p_a = jnp.concatenate(e_parts, axis=0)
            if a < len(anchors):
                p_a = _after(p_a, anchors[a])
            probs.append(p_a)
            inv_l.append(1.0 / jnp.concatenate(l_parts, axis=0))
        o = (_dot(probs[0], vmask_ref[2 * g, band, :])
             + _dot(probs[1], vmask_ref[2 * g + 1, band, :]))
        o = o * jnp.where(lo_out, inv_l[0], inv_l[1])
        attn_ref[rows, (2 * g) * LANES:(2 * g + 1) * LANES] = o[:BLOCK]
        attn_ref[rows, (2 * g + 1) * LANES:(2 * g + 2) * LANES] = o[BLOCK:]

    n_uc = (_QX0 - _UC0) // MXU_COLS
    attn_todo = [(j, g) for j in range(n_blk) for g in range(N_KV_HEADS)]
    s_all = [scores(j, g) for j, g in attn_todo]
    uc = [_dot(_after(h, s_all[i % len(attn_todo)][0]), w_in_ref[:, _UC0 + i * MXU_COLS:_UC0 + (i + 1) * MXU_COLS])
          for i in range(n_uc)]
    for i, (j, g) in enumerate(attn_todo):
        attend(j, g, s_all[i], uc[i::len(attn_todo)])

    kmask_ref[:, 0:BLOCK, :] = kmask_ref[:, ts:ts + BLOCK, :]
    vmask_ref[:, 0:BLOCK, :] = vmask_ref[:, ts:ts + BLOCK, :]
    kpos_ref[...] = pos_blk(n_blk - 1)

    n_glu = CONV_W // MXU_COLS
    for c in range(n_glu):
        a_c = ((uc[c] + b_glu_ref[:, cols(c)])
               * _sigmoid(uc[n_glu + c] + b_glu_ref[:, CONV_W + c * MXU_COLS:CONV_W + (c + 1) * MXU_COLS]))
        for i in range(MXU_COLS // LANES):
            conv_ref[c * (MXU_COLS // LANES) + i, CONV_CARRY:, :] = a_c[:, i * LANES:(i + 1) * LANES]
    g_conv = jnp.concatenate(uc[2 * n_glu:], axis=1)

    first_tap_row = CONV_CARRY - (CONV_K - 1)
    conv_out = {}

    def conv_piece(lc, r0):
        lanes = slice(lc * LANES, (lc + 1) * LANES)
        acc = jnp.broadcast_to(b_dw_ref[:, lanes], (CONV_ROWS, LANES))
        for t in range(CONV_K):
            off = r0 + first_tap_row + t
            acc = acc + w_dw_ref[0, t, :, lanes] * conv_ref[lc, off:off + CONV_ROWS, :]
        conv_out[(lc, r0)] = acc

    conv_act = {}

    def conv_norm(r0):
        c = jnp.concatenate([conv_out.pop((lc, r0)) for lc in range(CONV_W // LANES)], axis=1)
        mu = jnp.mean(c, axis=-1, keepdims=True)
        cc = c - mu
        var = jnp.mean(cc * cc, axis=-1, keepdims=True)
        cn = cc * lax.rsqrt(var + LN_EPS) * ln_g_ref[...] + ln_b_ref[...]
        conv_act[r0] = (_silu(cn) * _silu(g_conv[r0:r0 + CONV_ROWS])).astype(_BF16)

    cross_out = {}
    qx = {}

    def cross_head(hh):
        per_chunk = MXU_COLS // CROSS_HEAD_DIM
        qh = qx[hh // per_chunk][:, (hh % per_chunk) * CROSS_HEAD_DIM:(hh % per_chunk + 1) * CROSS_HEAD_DIM]
        kv_cols = slice(hh * CROSS_HEAD_DIM, (hh + 1) * CROSS_HEAD_DIM)
        s = _dot_nt(qh.astype(_BF16), km_ref[:, kv_cols]) * (1.0 / math.sqrt(CROSS_HEAD_DIM))
        m = jnp.max(s, axis=1, keepdims=True)
        e = jnp.exp(s - m)
        l = jnp.sum(e, axis=1, keepdims=True)
        cross_out[hh] = _dot(e.astype(_BF16), vm_ref[:, kv_cols]) * (1.0 / l)

    vec_todo = []
    for r0 in range(0, ts, CONV_ROWS):
        vec_todo += [(conv_piece, (lc, r0)) for lc in range(CONV_W // LANES)]
        vec_todo.append((conv_norm, (r0,)))
    n_qx = (_MG0 - _QX0) // MXU_COLS
    gates = []
    for i in range(n_qx + 3 * n_col):
        if i < n_qx:
            qx[i] = proj(_QX0 + i * MXU_COLS)
            if i == CROSS_W // MXU_COLS - 1:
                vec_todo[2:2] = [(cross_head, (hh,)) for hh in range(N_CROSS_HEADS)]
        else:
            gates.append(_sigmoid(proj(_MG0 + (i - n_qx) * MXU_COLS)))
        if vec_todo:
            fn, args = vec_todo.pop(0)
            fn(*args)
    while vec_todo:
        fn, args = vec_todo.pop(0)
        fn(*args)
    conv_ref[:, 0:CONV_CARRY, :] = conv_ref[:, ts:ts + CONV_CARRY, :]

    g_cross = jnp.concatenate([qx[i] for i in range(CROSS_W // MXU_COLS, n_qx)], axis=1)
    attn_act = (attn_ref[...] * _silu(g_attn)).astype(_BF16)
    conv_all = jnp.concatenate([conv_act[r0] for r0 in range(0, ts, CONV_ROWS)], axis=0)
    cross_act = (jnp.concatenate([cross_out[hh] for hh in range(N_CROSS_HEADS)], axis=1)
                 * _silu(g_cross)).astype(_BF16)
    merged = []
    wide = OUT_CHUNKS * MXU_COLS
    for c in range(D_MODEL // wide):
        wcols = slice(c * wide, (c + 1) * wide)
        gate = [jnp.concatenate(gates[br * n_col + OUT_CHUNKS * c:br * n_col + OUT_CHUNKS * (c + 1)], axis=1)
                for br in range(3)]
        y_attn = _dot(attn_act, w_oa_ref[:, wcols])
        y_conv = _dot(conv_all, w_pw_ref[:, wcols]) + b_pw_ref[:, wcols]
        y_cross = _dot(cross_act, w_oc_ref[:, wcols])
        merged.append((gate[0] * y_attn + gate[1] * y_conv + gate[2] * y_cross).astype(_BF16))
    merged = jnp.concatenate(merged, axis=1)

    x_new = jnp.concatenate([x[:, c * wide:(c + 1) * wide] + _dot(merged, w_out_ref[:, c * wide:(c + 1) * wide])
                             for c in range(D_MODEL // wide)], axis=1)
    out_ref[0] = _rmsnorm(x_new, fng_ref[...])


def _trunk_kernel(sinks_ref, x_hbm, pos_hbm, mem_hbm, mem_g_ref, norm_g_ref, w_in_hbm,
                  w_out_hbm, w_oa_hbm, w_pw_hbm, w_oc_hbm, w_mkv_hbm, b_glu_ref, w_dw_ref, b_dw_ref,
                  ln_g_ref, ln_b_ref, b_pw_ref, fng_ref, out_hbm,
                  kmask_ref, vmask_ref, kpos_ref, attn_ref, conv_ref,
                  w_in_ref, w_out_ref, w_oa_ref, w_pw_ref, w_oc_ref, w_mkv_ref, km_ref, vm_ref,
                  x_buf, out_buf, mem_buf, pos_buf, x_sem, out_sem, mem_sem, pos_sem):
    batch, seq, _ = x_hbm.shape
    ts = x_buf.shape[1]
    n_seq = seq // ts
    n_tiles = batch * n_seq

    def tile_rows(t):
        if isinstance(t, int):
            return t // n_seq, pl.ds((t % n_seq) * ts, ts)
        return t // n_seq, pl.ds(pl.multiple_of((t % n_seq) * ts, ts), ts)

    def x_copy(t):
        b, rows = tile_rows(t)
        return pltpu.make_async_copy(x_hbm.at[b, rows], x_buf.at[t % 2], x_sem.at[t % 2])

    def pos_copy(t):
        b, rows = tile_rows(t)
        return pltpu.make_async_copy(pos_hbm.at[pl.ds(b, 1), rows], pos_buf.at[t % 2], pos_sem.at[t % 2])

    def out_copy(t):
        b, rows = tile_rows(t)
        return pltpu.make_async_copy(out_buf.at[t % 2], out_hbm.at[b, rows], out_sem.at[t % 2])

    def mem_copy(b):
        return pltpu.make_async_copy(mem_hbm.at[b], mem_buf.at[0], mem_sem.at[0])

    x_copy(0).start()
    pos_copy(0).start()
    mem_copy(0).start()
    _load_as_bf16(w_in_hbm, w_in_ref, LOAD_ROWS)
    _load_many_as_bf16([(w_mkv_hbm, w_mkv_ref), (w_out_hbm, w_out_ref), (w_oa_hbm, w_oa_ref),
                        (w_pw_hbm, w_pw_ref), (w_oc_hbm, w_oc_ref)], LOAD_ROWS_SMALL)
    pos_copy(0).wait()

    def batch_row(b, carry):
        mem_copy(b).wait()
        _start_row(pos_buf.at[(b * n_seq) % 2], mem_buf, mem_g_ref, w_mkv_ref,
                   kmask_ref, vmask_ref, kpos_ref, conv_ref, km_ref, vm_ref)

        @pl.when(b + 1 < batch)
        def _():
            mem_copy(b + 1).start()

        def tile(s, carry):
            t = b * n_seq + s

            @pl.when(t + 1 < n_tiles)
            def _():
                x_copy(t + 1).start()
                pos_copy(t + 1).start()

            x_copy(t).wait()

            @pl.when(t >= 2)
            def _():
                out_copy(t - 2).wait()

            slot = t % 2
            _tile(sinks_ref, x_buf.at[pl.ds(slot, 1)],
                  pos_buf.at[slot],
                  norm_g_ref, b_glu_ref, w_dw_ref, b_dw_ref, ln_g_ref, ln_b_ref, b_pw_ref, fng_ref,
                  out_buf.at[pl.ds(slot, 1)], kmask_ref, vmask_ref, kpos_ref, attn_ref, conv_ref,
                  w_in_ref, w_out_ref, w_oa_ref, w_pw_ref, w_oc_ref, km_ref, vm_ref)
            out_copy(t).start()

            @pl.when(t + 1 < n_tiles)
            def _():
                pos_copy(t + 1).wait()

            return carry

        return lax.fori_loop(0, n_seq, tile, carry)

    lax.fori_loop(0, batch, batch_row, 0)
    out_copy(n_tiles - 2).wait()
    out_copy(n_tiles - 1).wait()


def _const_spec(shape):
    nd = len(shape)
    return pl.BlockSpec(shape, lambda i: (0,) * nd, pipeline_mode=pl.Buffered(1))


@jax.jit
def kernel(x, mem, positions, norm_g, w_in, attn_sinks, w_o_attn, b_glu, w_dw, b_dw, ln_g, ln_b,
           w_pw, b_pw, mem_norm_g, w_mem_kv, w_o_cross, w_out, final_norm_g):
    batch, seq, d_model = x.shape
    assert d_model == D_MODEL and norm_g.shape[0] == 1 and w_in.shape == (1, D_MODEL, IN_W)
    assert seq % SEQ_TILE == 0 and SEQ_TILE % BLOCK == 0 and mem.shape[1] == MEM_LEN
    ts = SEQ_TILE

    hbm = pl.BlockSpec(memory_space=pl.ANY)
    in_specs = [
        pl.BlockSpec(memory_space=pltpu.SMEM),
        hbm,
        hbm,
        hbm,
        _const_spec((1, D_MODEL)),
        _const_spec((1, D_MODEL)),
        hbm, hbm, hbm, hbm, hbm, hbm,
        _const_spec((1, 2 * CONV_W)),
        pl.BlockSpec(memory_space=pltpu.VMEM),
        _const_spec((1, CONV_W)),
        _const_spec((1, CONV_W)),
        _const_spec((1, CONV_W)),
        _const_spec((1, D_MODEL)),
        _const_spec((1, D_MODEL)),
    ]
    return pl.pallas_call(
        _trunk_kernel,
        grid=(1,),
        in_specs=in_specs,
        out_specs=hbm,
        out_shape=jax.ShapeDtypeStruct((batch, seq, D_MODEL), x.dtype),
        scratch_shapes=[
            pltpu.VMEM((4, ts + BLOCK, KV_W), _BF16),
            pltpu.VMEM((4, ts + BLOCK, KV_W), _BF16),
            pltpu.VMEM((1, BLOCK), jnp.int32),
            pltpu.VMEM((ts, ATTN_W), _F32),
            pltpu.VMEM((CONV_W // LANES, ts + CONV_CARRY, LANES), _F32),
            pltpu.VMEM((D_MODEL, IN_W), _BF16),
            pltpu.VMEM((D_MODEL, D_MODEL), _BF16),
            pltpu.VMEM((ATTN_W, D_MODEL), _BF16),
            pltpu.VMEM((CONV_W, D_MODEL), _BF16),
            pltpu.VMEM((CROSS_W, D_MODEL), _BF16),
            pltpu.VMEM((D_MODEL, 2 * CROSS_W), _BF16),
            pltpu.VMEM((MEM_LEN, CROSS_W), _BF16),
            pltpu.VMEM((MEM_LEN, CROSS_W), _BF16),
            pltpu.VMEM((2, ts, D_MODEL), _F32),
            pltpu.VMEM((2, ts, D_MODEL), _F32),
            pltpu.VMEM((1, MEM_LEN, D_MODEL), _F32),
            pltpu.VMEM((2, 1, ts), jnp.int32),
            pltpu.SemaphoreType.DMA((2,)),
            pltpu.SemaphoreType.DMA((2,)),
            pltpu.SemaphoreType.DMA((1,)),
            pltpu.SemaphoreType.DMA((2,)),
        ],
        compiler_params=pltpu.CompilerParams(
            dimension_semantics=("arbitrary",),
            vmem_limit_bytes=VMEM_LIMIT_BYTES),
        name="trunk",
    )(attn_sinks, x, positions, mem, mem_norm_g, norm_g,
      w_in[0], w_out[0], w_o_attn[0], w_pw[0], w_o_cross[0], w_mem_kv[0], b_glu,
      w_dw, b_dw, ln_g, ln_b, b_pw, final_norm_g.reshape(1, D_MODEL))
```

```python
import math

import jax
import jax.numpy as jnp
from jax import lax
from jax.experimental import pallas as pl
from jax.experimental.pallas import tpu as pltpu

D_MODEL = 1024
MEM_LEN = 256
HEAD_DIM = 64
N_Q_HEADS = 8
N_KV_HEADS = 2
WINDOW = 128
BLOCK = 128
ATTN_W = N_Q_HEADS * HEAD_DIM
KV_W = N_KV_HEADS * HEAD_DIM
CONV_W = D_MODEL // 2
CONV_K = 31
N_CROSS_HEADS = 4
CROSS_HEAD_DIM = 128
CROSS_W = N_CROSS_HEADS * CROSS_HEAD_DIM
RMS_EPS = 1e-6
LN_EPS = 1e-5
NEG_BIG = -1e30
ALIBI_SLOPES = tuple(2.0 ** (-8.0 * (i + 1) / N_Q_HEADS) for i in range(N_Q_HEADS))

_Q0 = 0
_K0 = _Q0 + ATTN_W
_V0 = _K0 + KV_W
_GA0 = _V0 + KV_W
_UC0 = _GA0 + ATTN_W
_GC0 = _UC0 + 2 * CONV_W
_QX0 = _GC0 + CONV_W
_GX0 = _QX0 + CROSS_W
_MG0 = _GX0 + CROSS_W
IN_W = _MG0 + 3 * D_MODEL

LANES = 128
VREG_ROW_BYTES = 32
MXU_COLS = 256
SEQ_TILE = 512
CONV_CARRY = 32
CONV_ROWS = 128
X_SLOTS = 3
OUT_CHUNKS = 4
LOAD_SLOTS = 8
LOAD_ROWS = 32
LOAD_ROWS_SMALL = 128
VMEM_LIMIT_BYTES = 56 * 1024 * 1024

_BF16 = jnp.bfloat16
_F32 = jnp.float32


def _dot(a, b):
    return jnp.dot(a, b, preferred_element_type=_F32)


def _dot_nt(a, b):
    return lax.dot_general(a, b, (((1,), (1,)), ((), ())), preferred_element_type=_F32)


def _rmsnorm(xf, g):
    y = xf * lax.rsqrt(jnp.mean(xf * xf, axis=-1, keepdims=True) + RMS_EPS)
    return y * g


def _sigmoid(x):
    return jax.nn.sigmoid(x)


def _silu(x):
    return x * jax.nn.sigmoid(x)


def _after(value, anchor):
    rows = VREG_ROW_BYTES // value.dtype.itemsize
    never = pl.program_id(0) < 0
    head = jnp.where(never, anchor[:rows, :LANES].astype(value.dtype), value[:rows, :LANES])
    top = jnp.concatenate([head, value[:rows, LANES:]], axis=1)
    return jnp.concatenate([top, value[rows:]], axis=0)


def _load_as_bf16(src_hbm, dst_ref, rows):
    n_chunks = src_hbm.shape[0] // rows

    def load(stage_ref, sem_ref):
        def chunk_copy(i, slot):
            return pltpu.make_async_copy(src_hbm.at[pl.ds(i * rows, rows)], stage_ref.at[slot],
                                         sem_ref.at[slot])

        for i in range(min(LOAD_SLOTS, n_chunks)):
            chunk_copy(i, i).start()

        def body(i, carry):
            slot = i % LOAD_SLOTS
            chunk_copy(i, slot).wait()
            dst_ref[pl.ds(pl.multiple_of(i * rows, rows), rows), :] = stage_ref[slot].astype(_BF16)

            @pl.when(i + LOAD_SLOTS < n_chunks)
            def _():
                chunk_copy(i + LOAD_SLOTS, slot).start()

            return carry

        lax.fori_loop(0, n_chunks, body, 0)

    pl.run_scoped(load, pltpu.VMEM((LOAD_SLOTS, rows, src_hbm.shape[1]), _F32),
                  pltpu.SemaphoreType.DMA((LOAD_SLOTS,)))


def _load_many_as_bf16(pairs, rows):
    chunks = [(src, dst, r0) for src, dst in pairs for r0 in range(0, src.shape[0], rows)]
    width = pairs[0][0].shape[1]

    def load(stage_ref, sem_ref):
        def chunk_copy(k):
            src, _, r0 = chunks[k]
            slot = k % LOAD_SLOTS
            return pltpu.make_async_copy(src.at[pl.ds(r0, rows)], stage_ref.at[slot], sem_ref.at[slot])

        for k in range(min(LOAD_SLOTS, len(chunks))):
            chunk_copy(k).start()
        for k, (_, dst, r0) in enumerate(chunks):
            chunk_copy(k).wait()
            dst[pl.ds(r0, rows), :] = stage_ref[k % LOAD_SLOTS].astype(_BF16)
            if k + LOAD_SLOTS < len(chunks):
                chunk_copy(k + LOAD_SLOTS).start()

    pl.run_scoped(load, pltpu.VMEM((LOAD_SLOTS, rows, width), _F32),
                  pltpu.SemaphoreType.DMA((LOAD_SLOTS,)))


def _start_row(pos_ref, mem_ref, mem_g_ref, w_mkv_ref, kmask_ref, vmask_ref, kpos_ref, conv_ref,
               km_ref, vm_ref):
    kmask_ref[:, 0:BLOCK, :] = jnp.zeros((4, BLOCK, KV_W), _BF16)
    vmask_ref[:, 0:BLOCK, :] = jnp.zeros((4, BLOCK, KV_W), _BF16)
    conv_ref[:, 0:CONV_CARRY, :] = jnp.zeros((CONV_W // LANES, CONV_CARRY, LANES), _F32)
    kpos_ref[...] = jnp.broadcast_to(pos_ref[:, 0:1] - (WINDOW + 1), (1, BLOCK))
    mem_n = _rmsnorm(mem_ref[0], mem_g_ref[...]).astype(_BF16)
    mem_kv = _dot(mem_n, w_mkv_ref[...])
    km_ref[...] = mem_kv[:, :CROSS_W].astype(_BF16)
    vm_ref[...] = mem_kv[:, CROSS_W:].astype(_BF16)


def _tile_back(x_ref, merged_ref, w_out_ref, fng_ref, out_ref):
    x_new = x_ref[0] + _dot(merged_ref[...], w_out_ref[...])
    out_ref[0] = _rmsnorm(x_new, fng_ref[...])


def _tile_front(sinks_ref, x_ref, pos_ref, norm_g_ref, b_glu_ref, w_dw_ref, b_dw_ref, ln_g_ref, ln_b_ref,
                b_pw_ref, merged_ref, kmask_ref, vmask_ref, kpos_ref, attn_ref, conv_ref,
                w_in_ref, w_oa_ref, w_pw_ref, w_oc_ref, km_ref, vm_ref):
    ts = x_ref.shape[1]
    n_blk = ts // BLOCK
    n_col = D_MODEL // MXU_COLS

    x = x_ref[0]
    h = _rmsnorm(x, norm_g_ref[...]).astype(_BF16)

    def proj(c0, width=MXU_COLS):
        return _dot(h, w_in_ref[:, c0:c0 + width])

    def cols(c):
        return slice(c * MXU_COLS, (c + 1) * MXU_COLS)

    qkvg = proj(_Q0, _UC0 - _Q0)
    q = (qkvg[:, _Q0:_K0] * (1.0 / math.sqrt(HEAD_DIM))).astype(_BF16)
    k = qkvg[:, _K0:_V0]
    v = qkvg[:, _V0:_GA0]
    g_attn = qkvg[:, _GA0:_UC0]

    lo_half = lax.broadcasted_iota(jnp.int32, (ts, KV_W), 1) < HEAD_DIM
    zero = jnp.zeros((ts, KV_W), _F32)
    for src, dst_ref in ((k, kmask_ref), (v, vmask_ref)):
        rolled = pltpu.roll(src, HEAD_DIM, 1)
        dst_ref[0, BLOCK:, :] = jnp.where(lo_half, src, zero).astype(_BF16)
        dst_ref[1, BLOCK:, :] = jnp.where(lo_half, zero, rolled).astype(_BF16)
        dst_ref[2, BLOCK:, :] = jnp.where(lo_half, rolled, zero).astype(_BF16)
        dst_ref[3, BLOCK:, :] = jnp.where(lo_half, zero, src).astype(_BF16)

    lo_out = lax.broadcasted_iota(jnp.int32, (2 * BLOCK, 2 * HEAD_DIM), 1) < HEAD_DIM
    band_cache = {}

    def pos_blk(j):
        return pos_ref[:, j * BLOCK:(j + 1) * BLOCK]

    def band_terms(j):
        if j not in band_cache:
            qcol = jnp.broadcast_to(pos_blk(j), (BLOCK, BLOCK)).T
            prev_pos = kpos_ref[...] if j == 0 else pos_blk(j - 1)
            delta = jnp.concatenate([qcol - prev_pos, qcol - pos_blk(j)], axis=1)
            band_cache[j] = ((delta >= 0) & (delta < WINDOW), delta.astype(_F32))
        return band_cache[j]

    def scores(j, g):
        rows = slice(j * BLOCK, (j + 1) * BLOCK)
        band = slice(j * BLOCK, (j + 2) * BLOCK)
        qg = jnp.concatenate([q[rows, (2 * g) * LANES:(2 * g + 1) * LANES],
                              q[rows, (2 * g + 1) * LANES:(2 * g + 2) * LANES]], axis=0)
        return [_dot_nt(qg, kmask_ref[2 * g + a, band, :]) for a in range(2)]

    def attend(j, g, s_pair, anchors):
        rows = slice(j * BLOCK, (j + 1) * BLOCK)
        band = slice(j * BLOCK, (j + 2) * BLOCK)
        allowed, delta_f = band_terms(j)
        probs = []
        inv_l = []
        for a in range(2):
            s = s_pair[a]
            e_parts = []
            l_parts = []
            for r in range(2):
                head = 2 * (2 * g + r) + a
                sr = s[r * BLOCK:(r + 1) * BLOCK]
                sr = jnp.where(allowed, sr - ALIBI_SLOPES[head] * delta_f, NEG_BIG)
                sink = sinks_ref[0, head]
                m = jnp.maximum(jnp.max(sr, axis=1, keepdims=True), sink)
                e = jnp.exp(sr - m)
                l_parts.append(jnp.sum(e, axis=1, keepdims=True) + jnp.exp(sink - m))
                e_parts.append(e.astype(_BF16))
            p_a = jnp.concatenate(e_parts, axis=0)
            if a < len(anchors):
                p_a = _after(p_a, anchors[a])
            probs.append(p_a)
            inv_l.append(1.0 / jnp.concatenate(l_parts, axis=0))
        o = (_dot(probs[0], vmask_ref[2 * g, band, :])
             + _dot(probs[1], vmask_ref[2 * g + 1, band, :]))
        o = o * jnp.where(lo_out, inv_l[0], inv_l[1])
        attn_ref[rows, (2 * g) * LANES:(2 * g + 1) * LANES] = o[:BLOCK]
        attn_ref[rows, (2 * g + 1) * LANES:(2 * g + 2) * LANES] = o[BLOCK:]

    n_uc = (_QX0 - _UC0) // MXU_COLS
    attn_todo = [(j, g) for j in range(n_blk) for g in range(N_KV_HEADS)]
    s_all = [scores(j, g) for j, g in attn_todo]
    uc = [_dot(_after(h, s_all[i % len(attn_todo)][0]), w_in_ref[:, _UC0 + i * MXU_COLS:_UC0 + (i + 1) * MXU_COLS])
          for i in range(n_uc)]
    for i, (j, g) in enumerate(attn_todo):
        attend(j, g, s_all[i], uc[i::len(attn_todo)])

    kmask_ref[:, 0:BLOCK, :] = kmask_ref[:, ts:ts + BLOCK, :]
    vmask_ref[:, 0:BLOCK, :] = vmask_ref[:, ts:ts + BLOCK, :]
    kpos_ref[...] = pos_blk(n_blk - 1)

    n_glu = CONV_W // MXU_COLS
    for c in range(n_glu):
        a_c = ((uc[c] + b_glu_ref[:, cols(c)])
               * _sigmoid(uc[n_glu + c] + b_glu_ref[:, CONV_W + c * MXU_COLS:CONV_W + (c + 1) * MXU_COLS]))
        for i in range(MXU_COLS // LANES):
            conv_ref[c * (MXU_COLS // LANES) + i, CONV_CARRY:, :] = a_c[:, i * LANES:(i + 1) * LANES]
    g_conv = jnp.concatenate(uc[2 * n_glu:], axis=1)

    first_tap_row = CONV_CARRY - (CONV_K - 1)
    conv_out = {}

    def conv_piece(lc, r0):
        lanes = slice(lc * LANES, (lc + 1) * LANES)
        acc = jnp.broadcast_to(b_dw_ref[:, lanes], (CONV_ROWS, LANES))
        for t in range(CONV_K):
            off = r0 + first_tap_row + t
            acc = acc + w_dw_ref[0, t, :, lanes] * conv_ref[lc, off:off + CONV_ROWS, :]
        conv_out[(lc, r0)] = acc

    conv_act = {}

    def conv_norm(r0):
        c = jnp.concatenate([conv_out.pop((lc, r0)) for lc in range(CONV_W // LANES)], axis=1)
        mu = jnp.mean(c, axis=-1, keepdims=True)
        cc = c - mu
        var = jnp.mean(cc * cc, axis=-1, keepdims=True)
        cn = cc * lax.rsqrt(var + LN_EPS) * ln_g_ref[...] + ln_b_ref[...]
        conv_act[r0] = (_silu(cn) * _silu(g_conv[r0:r0 + CONV_ROWS])).astype(_BF16)

    cross_out = {}
    qx = {}

    def cross_head(hh):
        per_chunk = MXU_COLS // CROSS_HEAD_DIM
        qh = qx[hh // per_chunk][:, (hh % per_chunk) * CROSS_HEAD_DIM:(hh % per_chunk + 1) * CROSS_HEAD_DIM]
        kv_cols = slice(hh * CROSS_HEAD_DIM, (hh + 1) * CROSS_HEAD_DIM)
        s = _dot_nt(qh.astype(_BF16), km_ref[:, kv_cols]) * (1.0 / math.sqrt(CROSS_HEAD_DIM))
        m = jnp.max(s, axis=1, keepdims=True)
        e = jnp.exp(s - m)
        l = jnp.sum(e, axis=1, keepdims=True)
        cross_out[hh] = _dot(e.astype(_BF16), vm_ref[:, kv_cols]) * (1.0 / l)

    vec_todo = []
    for r0 in range(0, ts, CONV_ROWS):
        vec_todo += [(conv_piece, (lc, r0)) for lc in range(CONV_W // LANES)]
        vec_todo.append((conv_norm, (r0,)))
    n_qx = (_MG0 - _QX0) // MXU_COLS
    gates = []
    for i in range(n_qx + 3 * n_col):
        if i < n_qx:
            qx[i] = proj(_QX0 + i * MXU_COLS)
            if i == CROSS_W // MXU_COLS - 1:
                vec_todo[2:2] = [(cross_head, (hh,)) for hh in range(N_CROSS_HEADS)]
        else:
            gates.append(_sigmoid(proj(_MG0 + (i - n_qx) * MXU_COLS)))
        if vec_todo:
            fn, args = vec_todo.pop(0)
            fn(*args)
    while vec_todo:
        fn, args = vec_todo.pop(0)
        fn(*args)
    conv_ref[:, 0:CONV_CARRY, :] = conv_ref[:, ts:ts + CONV_CARRY, :]

    g_cross = jnp.concatenate([qx[i] for i in range(CROSS_W // MXU_COLS, n_qx)], axis=1)
    attn_act = (attn_ref[...] * _silu(g_attn)).astype(_BF16)
    conv_all = jnp.concatenate([conv_act[r0] for r0 in range(0, ts, CONV_ROWS)], axis=0)
    cross_act = (jnp.concatenate([cross_out[hh] for hh in range(N_CROSS_HEADS)], axis=1)
                 * _silu(g_cross)).astype(_BF16)
    merged = []
    wide = OUT_CHUNKS * MXU_COLS
    for c in range(D_MODEL // wide):
        wcols = slice(c * wide, (c + 1) * wide)
        gate = [jnp.concatenate(gates[br * n_col + OUT_CHUNKS * c:br * n_col + OUT_CHUNKS * (c + 1)], axis=1)
                for br in range(3)]
        y_attn = _dot(attn_act, w_oa_ref[:, wcols])
        y_conv = _dot(conv_all, w_pw_ref[:, wcols]) + b_pw_ref[:, wcols]
        y_cross = _dot(cross_act, w_oc_ref[:, wcols])
        merged.append((gate[0] * y_attn + gate[1] * y_conv + gate[2] * y_cross).astype(_BF16))
    merged_ref[...] = jnp.concatenate(merged, axis=1)


def _trunk_kernel(sinks_ref, x_hbm, pos_hbm, mem_hbm, mem_g_ref, norm_g_ref, w_in_hbm,
                  w_out_hbm, w_oa_hbm, w_pw_hbm, w_oc_hbm, w_mkv_hbm, b_glu_ref, w_dw_ref, b_dw_ref,
                  ln_g_ref, ln_b_ref, b_pw_ref, fng_ref, out_hbm,
                  kmask_ref, vmask_ref, kpos_ref, attn_ref, conv_ref,
                  w_in_ref, w_out_ref, w_oa_ref, w_pw_ref, w_oc_ref, w_mkv_ref, km_ref, vm_ref,
                  x_buf, out_buf, mem_buf, pos_buf, merged_ref, x_sem, out_sem, mem_sem, pos_sem):
    batch, seq, _ = x_hbm.shape
    ts = x_buf.shape[1]
    n_seq = seq // ts
    n_tiles = batch * n_seq

    def tile_rows(t):
        if isinstance(t, int):
            return t // n_seq, pl.ds((t % n_seq) * ts, ts)
        return t // n_seq, pl.ds(pl.multiple_of((t % n_seq) * ts, ts), ts)

    def x_copy(t):
        b, rows = tile_rows(t)
        return pltpu.make_async_copy(x_hbm.at[b, rows], x_buf.at[t % X_SLOTS], x_sem.at[t % X_SLOTS])

    def pos_copy(t):
        b, rows = tile_rows(t)
        return pltpu.make_async_copy(pos_hbm.at[pl.ds(b, 1), rows], pos_buf.at[t % 2], pos_sem.at[t % 2])

    def out_copy(t):
        b, rows = tile_rows(t)
        return pltpu.make_async_copy(out_buf.at[t % 2], out_hbm.at[b, rows], out_sem.at[t % 2])

    def mem_copy(b):
        return pltpu.make_async_copy(mem_hbm.at[b], mem_buf.at[0], mem_sem.at[0])

    x_copy(0).start()
    pos_copy(0).start()
    mem_copy(0).start()
    merged_ref[...] = jnp.zeros(merged_ref.shape, _BF16)
    x_buf[X_SLOTS - 1] = jnp.zeros(x_buf.shape[1:], _F32)
    _load_as_bf16(w_in_hbm, w_in_ref, LOAD_ROWS)
    _load_many_as_bf16([(w_mkv_hbm, w_mkv_ref), (w_out_hbm, w_out_ref), (w_oa_hbm, w_oa_ref),
                        (w_pw_hbm, w_pw_ref), (w_oc_hbm, w_oc_ref)], LOAD_ROWS_SMALL)
    pos_copy(0).wait()

    def batch_row(b, carry):
        mem_copy(b).wait()
        _start_row(pos_buf.at[(b * n_seq) % 2], mem_buf, mem_g_ref, w_mkv_ref,
                   kmask_ref, vmask_ref, kpos_ref, conv_ref, km_ref, vm_ref)

        @pl.when(b + 1 < batch)
        def _():
            mem_copy(b + 1).start()

        def tile(s, carry):
            t = b * n_seq + s

            @pl.when(t + 1 < n_tiles)
            def _():
                x_copy(t + 1).start()
                pos_copy(t + 1).start()

            x_copy(t).wait()

            @pl.when(t >= 3)
            def _():
                out_copy(t - 3).wait()

            _tile_back(x_buf.at[pl.ds((t + X_SLOTS - 1) % X_SLOTS, 1)], merged_ref, w_out_ref, fng_ref,
                       out_buf.at[pl.ds((t + 1) % 2, 1)])
            _tile_front(sinks_ref, x_buf.at[pl.ds(t % X_SLOTS, 1)], pos_buf.at[t % 2],
                        norm_g_ref, b_glu_ref, w_dw_ref, b_dw_ref, ln_g_ref, ln_b_ref, b_pw_ref,
                        merged_ref, kmask_ref, vmask_ref, kpos_ref, attn_ref, conv_ref,
                        w_in_ref, w_oa_ref, w_pw_ref, w_oc_ref, km_ref, vm_ref)

            @pl.when(t >= 1)
            def _():
                out_copy(t - 1).start()

            @pl.when(t + 1 < n_tiles)
            def _():
                pos_copy(t + 1).wait()

            return carry

        return lax.fori_loop(0, n_seq, tile, carry)

    lax.fori_loop(0, batch, batch_row, 0)
    last = n_tiles - 1
    out_copy(last - 2).wait()
    _tile_back(x_buf.at[pl.ds(last % X_SLOTS, 1)], merged_ref, w_out_ref, fng_ref,
               out_buf.at[pl.ds(last % 2, 1)])
    out_copy(last).start()
    out_copy(last - 1).wait()
    out_copy(last).wait()


def _const_spec(shape):
    nd = len(shape)
    return pl.BlockSpec(shape, lambda i: (0,) * nd, pipeline_mode=pl.Buffered(1))


@jax.jit
def kernel(x, mem, positions, norm_g, w_in, attn_sinks, w_o_attn, b_glu, w_dw, b_dw, ln_g, ln_b,
           w_pw, b_pw, mem_norm_g, w_mem_kv, w_o_cross, w_out, final_norm_g):
    batch, seq, d_model = x.shape
    assert d_model == D_MODEL and norm_g.shape[0] == 1 and w_in.shape == (1, D_MODEL, IN_W)
    assert seq % SEQ_TILE == 0 and SEQ_TILE % BLOCK == 0 and mem.shape[1] == MEM_LEN
    ts = SEQ_TILE

    hbm = pl.BlockSpec(memory_space=pl.ANY)
    in_specs = [
        pl.BlockSpec(memory_space=pltpu.SMEM),
        hbm,
        hbm,
        hbm,
        _const_spec((1, D_MODEL)),
        _const_spec((1, D_MODEL)),
        hbm, hbm, hbm, hbm, hbm, hbm,
        _const_spec((1, 2 * CONV_W)),
        pl.BlockSpec(memory_space=pltpu.VMEM),
        _const_spec((1, CONV_W)),
        _const_spec((1, CONV_W)),
        _const_spec((1, CONV_W)),
        _const_spec((1, D_MODEL)),
        _const_spec((1, D_MODEL)),
    ]
    return pl.pallas_call(
        _trunk_kernel,
        grid=(1,),
        in_specs=in_specs,
        out_specs=hbm,
        out_shape=jax.ShapeDtypeStruct((batch, seq, D_MODEL), x.dtype),
        scratch_shapes=[
            pltpu.VMEM((4, ts + BLOCK, KV_W), _BF16),
            pltpu.VMEM((4, ts + BLOCK, KV_W), _BF16),
            pltpu.VMEM((1, BLOCK), jnp.int32),
            pltpu.VMEM((ts, ATTN_W), _F32),
            pltpu.VMEM((CONV_W // LANES, ts + CONV_CARRY, LANES), _F32),
            pltpu.VMEM((D_MODEL, IN_W), _BF16),
            pltpu.VMEM((D_MODEL, D_MODEL), _BF16),
            pltpu.VMEM((ATTN_W, D_MODEL), _BF16),
            pltpu.VMEM((CONV_W, D_MODEL), _BF16),
            pltpu.VMEM((CROSS_W, D_MODEL), _BF16),
            pltpu.VMEM((D_MODEL, 2 * CROSS_W), _BF16),
            pltpu.VMEM((MEM_LEN, CROSS_W), _BF16),
            pltpu.VMEM((MEM_LEN, CROSS_W), _BF16),
            pltpu.VMEM((X_SLOTS, ts, D_MODEL), _F32),
            pltpu.VMEM((2, ts, D_MODEL), _F32),
            pltpu.VMEM((1, MEM_LEN, D_MODEL), _F32),
            pltpu.VMEM((2, 1, ts), jnp.int32),
            pltpu.VMEM((ts, D_MODEL), _BF16),
            pltpu.SemaphoreType.DMA((X_SLOTS,)),
            pltpu.SemaphoreType.DMA((2,)),
            pltpu.SemaphoreType.DMA((1,)),
            pltpu.SemaphoreType.DMA((2,)),
        ],
        compiler_params=pltpu.CompilerParams(
            dimension_semantics=("arbitrary",),
            vmem_limit_bytes=VMEM_LIMIT_BYTES),
        name="trunk",
    )(attn_sinks, x, positions, mem, mem_norm_g, norm_g,
      w_in[0], w_out[0], w_o_attn[0], w_pw[0], w_o_cross[0], w_mem_kv[0], b_glu,
      w_dw, b_dw, ln_g, ln_b, b_pw, final_norm_g.reshape(1, D_MODEL))
```

```python
import math

import jax
import jax.numpy as jnp
from jax import lax
from jax.experimental import pallas as pl
from jax.experimental.pallas import tpu as pltpu

D_MODEL = 1024
MEM_LEN = 256
HEAD_DIM = 64
N_Q_HEADS = 8
N_KV_HEADS = 2
WINDOW = 128
BLOCK = 128
ATTN_W = N_Q_HEADS * HEAD_DIM
KV_W = N_KV_HEADS * HEAD_DIM
CONV_W = D_MODEL // 2
CONV_K = 31
N_CROSS_HEADS = 4
CROSS_HEAD_DIM = 128
CROSS_W = N_CROSS_HEADS * CROSS_HEAD_DIM
RMS_EPS = 1e-6
LN_EPS = 1e-5
NEG_BIG = -1e30
ALIBI_SLOPES = tuple(2.0 ** (-8.0 * (i + 1) / N_Q_HEADS) for i in range(N_Q_HEADS))

_Q0 = 0
_K0 = _Q0 + ATTN_W
_V0 = _K0 + KV_W
_GA0 = _V0 + KV_W
_UC0 = _GA0 + ATTN_W
_GC0 = _UC0 + 2 * CONV_W
_QX0 = _GC0 + CONV_W
_GX0 = _QX0 + CROSS_W
_MG0 = _GX0 + CROSS_W
IN_W = _MG0 + 3 * D_MODEL

LANES = 128
VREG_ROW_BYTES = 32
MXU_COLS = 256
SEQ_TILE = 512
CONV_CARRY = 32
CONV_ROWS = 128
OUT_CHUNKS = 4
LOAD_SLOTS = 12
LOAD_ROWS = 32
LOAD_ROWS_SMALL = 128
VMEM_LIMIT_BYTES = 56 * 1024 * 1024

_BF16 = jnp.bfloat16
_F32 = jnp.float32


def _dot(a, b):
    return jnp.dot(a, b, preferred_element_type=_F32)


def _dot_nt(a, b):
    return lax.dot_general(a, b, (((1,), (1,)), ((), ())), preferred_element_type=_F32)


def _rmsnorm(xf, g):
    y = xf * lax.rsqrt(jnp.mean(xf * xf, axis=-1, keepdims=True) + RMS_EPS)
    return y * g


def _sigmoid(x):
    return jax.nn.sigmoid(x)


def _silu(x):
    return x * jax.nn.sigmoid(x)


def _after(value, anchor):
    rows = VREG_ROW_BYTES // value.dtype.itemsize
    never = pl.program_id(0) < 0
    head = jnp.where(never, anchor[:rows, :LANES].astype(value.dtype), value[:rows, :LANES])
    top = jnp.concatenate([head, value[:rows, LANES:]], axis=1)
    return jnp.concatenate([top, value[rows:]], axis=0)


def _load_as_bf16(src_hbm, dst_ref, rows):
    n_chunks = src_hbm.shape[0] // rows

    def load(stage_ref, sem_ref):
        def chunk_copy(i, slot):
            return pltpu.make_async_copy(src_hbm.at[pl.ds(i * rows, rows)], stage_ref.at[slot],
                                         sem_ref.at[slot])

        for i in range(min(LOAD_SLOTS, n_chunks)):
            chunk_copy(i, i).start()

        def body(i, carry):
            slot = i % LOAD_SLOTS
            chunk_copy(i, slot).wait()
            dst_ref[pl.ds(pl.multiple_of(i * rows, rows), rows), :] = stage_ref[slot].astype(_BF16)

            @pl.when(i + LOAD_SLOTS < n_chunks)
            def _():
                chunk_copy(i + LOAD_SLOTS, slot).start()

            return carry

        lax.fori_loop(0, n_chunks, body, 0)

    pl.run_scoped(load, pltpu.VMEM((LOAD_SLOTS, rows, src_hbm.shape[1]), _F32),
                  pltpu.SemaphoreType.DMA((LOAD_SLOTS,)))


def _load_many_as_bf16(pairs, rows):
    chunks = [(src, dst, r0) for src, dst in pairs for r0 in range(0, src.shape[0], rows)]
    width = pairs[0][0].shape[1]

    def load(stage_ref, sem_ref):
        def chunk_copy(k):
            src, _, r0 = chunks[k]
            slot = k % LOAD_SLOTS
            return pltpu.make_async_copy(src.at[pl.ds(r0, rows)], stage_ref.at[slot], sem_ref.at[slot])

        for k in range(min(LOAD_SLOTS, len(chunks))):
            chunk_copy(k).start()
        for k, (_, dst, r0) in enumerate(chunks):
            chunk_copy(k).wait()
            dst[pl.ds(r0, rows), :] = stage_ref[k % LOAD_SLOTS].astype(_BF16)
            if k + LOAD_SLOTS < len(chunks):
                chunk_copy(k + LOAD_SLOTS).start()

    pl.run_scoped(load, pltpu.VMEM((LOAD_SLOTS, rows, width), _F32),
                  pltpu.SemaphoreType.DMA((LOAD_SLOTS,)))


def _start_row(pos_ref, mem_ref, mem_g_ref, w_mkv_ref, kmask_ref, vmask_ref, kpos_ref, conv_ref,
               km_ref, vm_ref):
    kmask_ref[:, 0:BLOCK, :] = jnp.zeros((4, BLOCK, KV_W), _BF16)
    vmask_ref[:, 0:BLOCK, :] = jnp.zeros((4, BLOCK, KV_W), _BF16)
    conv_ref[:, 0:CONV_CARRY, :] = jnp.zeros((CONV_W // LANES, CONV_CARRY, LANES), _F32)
    kpos_ref[...] = jnp.broadcast_to(pos_ref[:, 0:1] - (WINDOW + 1), (1, BLOCK))
    mem_n = _rmsnorm(mem_ref[0], mem_g_ref[...]).astype(_BF16)
    mem_kv = _dot(mem_n, w_mkv_ref[...])
    km_ref[...] = mem_kv[:, :CROSS_W].astype(_BF16)
    vm_ref[...] = mem_kv[:, CROSS_W:].astype(_BF16)


def _tile(sinks_ref, x_ref, pos_ref, norm_g_ref, b_glu_ref, w_dw_ref, b_dw_ref, ln_g_ref, ln_b_ref,
          b_pw_ref, fng_ref, out_ref, kmask_ref, vmask_ref, kpos_ref, attn_ref, conv_ref,
          w_in_ref, w_out_ref, w_oa_ref, w_pw_ref, w_oc_ref, km_ref, vm_ref):
    ts = x_ref.shape[1]
    n_blk = ts // BLOCK
    n_col = D_MODEL // MXU_COLS

    x = x_ref[0]
    h = _rmsnorm(x, norm_g_ref[...]).astype(_BF16)

    def proj(c0, width=MXU_COLS):
        return _dot(h, w_in_ref[:, c0:c0 + width])

    def cols(c):
        return slice(c * MXU_COLS, (c + 1) * MXU_COLS)

    qkvg = proj(_Q0, _UC0 - _Q0)
    q = (qkvg[:, _Q0:_K0] * (1.0 / math.sqrt(HEAD_DIM))).astype(_BF16)
    k = qkvg[:, _K0:_V0]
    v = qkvg[:, _V0:_GA0]
    g_attn = qkvg[:, _GA0:_UC0]

    lo_half = lax.broadcasted_iota(jnp.int32, (ts, KV_W), 1) < HEAD_DIM
    zero = jnp.zeros((ts, KV_W), _F32)
    for src, dst_ref in ((k, kmask_ref), (v, vmask_ref)):
        rolled = pltpu.roll(src, HEAD_DIM, 1)
        dst_ref[0, BLOCK:, :] = jnp.where(lo_half, src, zero).astype(_BF16)
        dst_ref[1, BLOCK:, :] = jnp.where(lo_half, zero, rolled).astype(_BF16)
        dst_ref[2, BLOCK:, :] = jnp.where(lo_half, rolled, zero).astype(_BF16)
        dst_ref[3, BLOCK:, :] = jnp.where(lo_half, zero, src).astype(_BF16)

    lo_out = lax.broadcasted_iota(jnp.int32, (2 * BLOCK, 2 * HEAD_DIM), 1) < HEAD_DIM
    band_cache = {}

    def pos_blk(j):
        return pos_ref[:, j * BLOCK:(j + 1) * BLOCK]

    def band_terms(j):
        if j not in band_cache:
            qcol = jnp.broadcast_to(pos_blk(j), (BLOCK, BLOCK)).T
            prev_pos = kpos_ref[...] if j == 0 else pos_blk(j - 1)
            delta = jnp.concatenate([qcol - prev_pos, qcol - pos_blk(j)], axis=1)
            band_cache[j] = ((delta >= 0) & (delta < WINDOW), delta.astype(_F32))
        return band_cache[j]

    def scores(j, g):
        rows = slice(j * BLOCK, (j + 1) * BLOCK)
        band = slice(j * BLOCK, (j + 2) * BLOCK)
        qg = jnp.concatenate([q[rows, (2 * g) * LANES:(2 * g + 1) * LANES],
                              q[rows, (2 * g + 1) * LANES:(2 * g + 2) * LANES]], axis=0)
        return [_dot_nt(qg, kmask_ref[2 * g + a, band, :]) for a in range(2)]

    def attend(j, g, s_pair, anchors):
        rows = slice(j * BLOCK, (j + 1) * BLOCK)
        band = slice(j * BLOCK, (j + 2) * BLOCK)
        allowed, delta_f = band_terms(j)
        probs = []
        inv_l = []
        for a in range(2):
            s = s_pair[a]
            e_parts = []
            l_parts = []
            for r in range(2):
                head = 2 * (2 * g + r) + a
                sr = s[r * BLOCK:(r + 1) * BLOCK]
                sr = jnp.where(allowed, sr - ALIBI_SLOPES[head] * delta_f, NEG_BIG)
                sink = sinks_ref[0, head]
                m = jnp.maximum(jnp.max(sr, axis=1, keepdims=True), sink)
                e = jnp.exp(sr - m)
                l_parts.append(jnp.sum(e, axis=1, keepdims=True) + jnp.exp(sink - m))
                e_parts.append(e.astype(_BF16))
            p_a = jnp.concatenate(e_parts, axis=0)
            if a < len(anchors):
                p_a = _after(p_a, anchors[a])
            probs.append(p_a)
            inv_l.append(1.0 / jnp.concatenate(l_parts, axis=0))
        o = (_dot(probs[0], vmask_ref[2 * g, band, :])
             + _dot(probs[1], vmask_ref[2 * g + 1, band, :]))
        o = o * jnp.where(lo_out, inv_l[0], inv_l[1])
        attn_ref[rows, (2 * g) * LANES:(2 * g + 1) * LANES] = o[:BLOCK]
        attn_ref[rows, (2 * g + 1) * LANES:(2 * g + 2) * LANES] = o[BLOCK:]

    n_uc = (_QX0 - _UC0) // MXU_COLS
    attn_todo = [(j, g) for j in range(n_blk) for g in range(N_KV_HEADS)]
    s_all = [scores(j, g) for j, g in attn_todo]
    uc = [_dot(_after(h, s_all[i % len(attn_todo)][0]), w_in_ref[:, _UC0 + i * MXU_COLS:_UC0 + (i + 1) * MXU_COLS])
          for i in range(n_uc)]
    for i, (j, g) in enumerate(attn_todo):
        attend(j, g, s_all[i], uc[i::len(attn_todo)])

    kmask_ref[:, 0:BLOCK, :] = kmask_ref[:, ts:ts + BLOCK, :]
    vmask_ref[:, 0:BLOCK, :] = vmask_ref[:, ts:ts + BLOCK, :]
    kpos_ref[...] = pos_blk(n_blk - 1)

    n_glu = CONV_W // MXU_COLS
    for c in range(n_glu):
        a_c = ((uc[c] + b_glu_ref[:, cols(c)])
               * _sigmoid(uc[n_glu + c] + b_glu_ref[:, CONV_W + c * MXU_COLS:CONV_W + (c + 1) * MXU_COLS]))
        for i in range(MXU_COLS // LANES):
            conv_ref[c * (MXU_COLS // LANES) + i, CONV_CARRY:, :] = a_c[:, i * LANES:(i + 1) * LANES]
    g_conv = jnp.concatenate(uc[2 * n_glu:], axis=1)

    first_tap_row = CONV_CARRY - (CONV_K - 1)
    conv_out = {}

    def conv_piece(lc, r0):
        lanes = slice(lc * LANES, (lc + 1) * LANES)
        acc = jnp.broadcast_to(b_dw_ref[:, lanes], (CONV_ROWS, LANES))
        for t in range(CONV_K):
            off = r0 + first_tap_row + t
            acc = acc + w_dw_ref[0, t, :, lanes] * conv_ref[lc, off:off + CONV_ROWS, :]
        conv_out[(lc, r0)] = acc

    conv_act = {}

    def conv_norm(r0):
        c = jnp.concatenate([conv_out.pop((lc, r0)) for lc in range(CONV_W // LANES)], axis=1)
        mu = jnp.mean(c, axis=-1, keepdims=True)
        cc = c - mu
        var = jnp.mean(cc * cc, axis=-1, keepdims=True)
        cn = cc * lax.rsqrt(var + LN_EPS) * ln_g_ref[...] + ln_b_ref[...]
        conv_act[r0] = (_silu(cn) * _silu(g_conv[r0:r0 + CONV_ROWS])).astype(_BF16)

    cross_out = {}
    qx = {}

    def cross_head(hh):
        per_chunk = MXU_COLS // CROSS_HEAD_DIM
        qh = qx[hh // per_chunk][:, (hh % per_chunk) * CROSS_HEAD_DIM:(hh % per_chunk + 1) * CROSS_HEAD_DIM]
        kv_cols = slice(hh * CROSS_HEAD_DIM, (hh + 1) * CROSS_HEAD_DIM)
        s = _dot_nt(qh.astype(_BF16), km_ref[:, kv_cols]) * (1.0 / math.sqrt(CROSS_HEAD_DIM))
        m = jnp.max(s, axis=1, keepdims=True)
        e = jnp.exp(s - m)
        l = jnp.sum(e, axis=1, keepdims=True)
        cross_out[hh] = _dot(e.astype(_BF16), vm_ref[:, kv_cols]) * (1.0 / l)

    vec_todo = []
    for r0 in range(0, ts, CONV_ROWS):
        vec_todo += [(conv_piece, (lc, r0)) for lc in range(CONV_W // LANES)]
        vec_todo.append((conv_norm, (r0,)))
    n_qx = (_MG0 - _QX0) // MXU_COLS
    gates = []
    for i in range(n_qx + 3 * n_col):
        if i < n_qx:
            qx[i] = proj(_QX0 + i * MXU_COLS)
            if i == CROSS_W // MXU_COLS - 1:
                vec_todo[2:2] = [(cross_head, (hh,)) for hh in range(N_CROSS_HEADS)]
        else:
            gates.append(_sigmoid(proj(_MG0 + (i - n_qx) * MXU_COLS)))
        if vec_todo:
            fn, args = vec_todo.pop(0)
            fn(*args)
    while vec_todo:
        fn, args = vec_todo.pop(0)
        fn(*args)
    conv_ref[:, 0:CONV_CARRY, :] = conv_ref[:, ts:ts + CONV_CARRY, :]

    g_cross = jnp.concatenate([qx[i] for i in range(CROSS_W // MXU_COLS, n_qx)], axis=1)
    attn_act = (attn_ref[...] * _silu(g_attn)).astype(_BF16)
    conv_all = jnp.concatenate([conv_act[r0] for r0 in range(0, ts, CONV_ROWS)], axis=0)
    cross_act = (jnp.concatenate([cross_out[hh] for hh in range(N_CROSS_HEADS)], axis=1)
                 * _silu(g_cross)).astype(_BF16)
    merged = []
    wide = OUT_CHUNKS * MXU_COLS
    for c in range(D_MODEL // wide):
        wcols = slice(c * wide, (c + 1) * wide)
        gate = [jnp.concatenate(gates[br * n_col + OUT_CHUNKS * c:br * n_col + OUT_CHUNKS * (c + 1)], axis=1)
                for br in range(3)]
        y_attn = _dot(attn_act, w_oa_ref[:, wcols])
        y_conv = _dot(conv_all, w_pw_ref[:, wcols]) + b_pw_ref[:, wcols]
        y_cross = _dot(cross_act, w_oc_ref[:, wcols])
        merged.append((gate[0] * y_attn + gate[1] * y_conv + gate[2] * y_cross).astype(_BF16))
    merged = jnp.concatenate(merged, axis=1)

    x_new = jnp.concatenate([x[:, c * wide:(c + 1) * wide] + _dot(merged, w_out_ref[:, c * wide:(c + 1) * wide])
                             for c in range(D_MODEL // wide)], axis=1)
    out_ref[0] = _rmsnorm(x_new, fng_ref[...])


def _trunk_kernel(sinks_ref, x_hbm, pos_hbm, mem_hbm, mem_g_ref, norm_g_ref, w_in_hbm,
                  w_out_hbm, w_oa_hbm, w_pw_hbm, w_oc_hbm, w_mkv_hbm, b_glu_ref, w_dw_ref, b_dw_ref,
                  ln_g_ref, ln_b_ref, b_pw_ref, fng_ref, out_hbm,
                  kmask_ref, vmask_ref, kpos_ref, attn_ref, conv_ref,
                  w_in_ref, w_out_ref, w_oa_ref, w_pw_ref, w_oc_ref, w_mkv_ref, km_ref, vm_ref,
                  x_buf, out_buf, mem_buf, pos_buf, x_sem, out_sem, mem_sem, pos_sem):
    batch, seq, _ = x_hbm.shape
    ts = x_buf.shape[1]
    n_seq = seq // ts
    n_tiles = batch * n_seq

    def tile_rows(t):
        if isinstance(t, int):
            return t // n_seq, pl.ds((t % n_seq) * ts, ts)
        return t // n_seq, pl.ds(pl.multiple_of((t % n_seq) * ts, ts), ts)

    def x_copy(t):
        b, rows = tile_rows(t)
        return pltpu.make_async_copy(x_hbm.at[b, rows], x_buf.at[t % 2], x_sem.at[t % 2])

    def pos_copy(t):
        b, rows = tile_rows(t)
        return pltpu.make_async_copy(pos_hbm.at[pl.ds(b, 1), rows], pos_buf.at[t % 2], pos_sem.at[t % 2])

    def out_copy(t):
        b, rows = tile_rows(t)
        return pltpu.make_async_copy(out_buf.at[t % 2], out_hbm.at[b, rows], out_sem.at[t % 2])

    def mem_copy(b):
        return pltpu.make_async_copy(mem_hbm.at[b], mem_buf.at[0], mem_sem.at[0])

    x_copy(0).start()
    pos_copy(0).start()
    mem_copy(0).start()
    _load_as_bf16(w_in_hbm, w_in_ref, LOAD_ROWS)
    _load_many_as_bf16([(w_mkv_hbm, w_mkv_ref), (w_out_hbm, w_out_ref), (w_oa_hbm, w_oa_ref),
                        (w_pw_hbm, w_pw_ref), (w_oc_hbm, w_oc_ref)], LOAD_ROWS_SMALL)
    pos_copy(0).wait()

    def batch_row(b, carry):
        mem_copy(b).wait()
        _start_row(pos_buf.at[(b * n_seq) % 2], mem_buf, mem_g_ref, w_mkv_ref,
                   kmask_ref, vmask_ref, kpos_ref, conv_ref, km_ref, vm_ref)

        @pl.when(b + 1 < batch)
        def _():
            mem_copy(b + 1).start()

        def tile(s, carry):
            t = b * n_seq + s

            @pl.when(t + 1 < n_tiles)
            def _():
                x_copy(t + 1).start()
                pos_copy(t + 1).start()

            x_copy(t).wait()

            @pl.when(t >= 2)
            def _():
                out_copy(t - 2).wait()

            slot = t % 2
            _tile(sinks_ref, x_buf.at[pl.ds(slot, 1)],
                  pos_buf.at[slot],
                  norm_g_ref, b_glu_ref, w_dw_ref, b_dw_ref, ln_g_ref, ln_b_ref, b_pw_ref, fng_ref,
                  out_buf.at[pl.ds(slot, 1)], kmask_ref, vmask_ref, kpos_ref, attn_ref, conv_ref,
                  w_in_ref, w_out_ref, w_oa_ref, w_pw_ref, w_oc_ref, km_ref, vm_ref)
            out_copy(t).start()

            @pl.when(t + 1 < n_tiles)
            def _():
                pos_copy(t + 1).wait()

            return carry

        return lax.fori_loop(0, n_seq, tile, carry)

    lax.fori_loop(0, batch, batch_row, 0)
    out_copy(n_tiles - 2).wait()
    out_copy(n_tiles - 1).wait()


def _const_spec(shape):
    nd = len(shape)
    return pl.BlockSpec(shape, lambda i: (0,) * nd, pipeline_mode=pl.Buffered(1))


@jax.jit
def kernel(x, mem, positions, norm_g, w_in, attn_sinks, w_o_attn, b_glu, w_dw, b_dw, ln_g, ln_b,
           w_pw, b_pw, mem_norm_g, w_mem_kv, w_o_cross, w_out, final_norm_g):
    batch, seq, d_model = x.shape
    assert d_model == D_MODEL and norm_g.shape[0] == 1 and w_in.shape == (1, D_MODEL, IN_W)
    assert seq % SEQ_TILE == 0 and SEQ_TILE % BLOCK == 0 and mem.shape[1] == MEM_LEN
    ts = SEQ_TILE

    hbm = pl.BlockSpec(memory_space=pl.ANY)
    in_specs = [
        pl.BlockSpec(memory_space=pltpu.SMEM),
        hbm,
        hbm,
        hbm,
        _const_spec((1, D_MODEL)),
        _const_spec((1, D_MODEL)),
        hbm, hbm, hbm, hbm, hbm, hbm,
        _const_spec((1, 2 * CONV_W)),
        pl.BlockSpec(memory_space=pltpu.VMEM),
        _const_spec((1, CONV_W)),
        _const_spec((1, CONV_W)),
        _const_spec((1, CONV_W)),
        _const_spec((1, D_MODEL)),
        _const_spec((1, D_MODEL)),
    ]
    return pl.pallas_call(
        _trunk_kernel,
        grid=(1,),
        in_specs=in_specs,
        out_specs=hbm,
        out_shape=jax.ShapeDtypeStruct((batch, seq, D_MODEL), x.dtype),
        scratch_shapes=[
            pltpu.VMEM((4, ts + BLOCK, KV_W), _BF16),
            pltpu.VMEM((4, ts + BLOCK, KV_W), _BF16),
            pltpu.VMEM((1, BLOCK), jnp.int32),
            pltpu.VMEM((ts, ATTN_W), _F32),
            pltpu.VMEM((CONV_W // LANES, ts + CONV_CARRY, LANES), _F32),
            pltpu.VMEM((D_MODEL, IN_W), _BF16),
            pltpu.VMEM((D_MODEL, D_MODEL), _BF16),
            pltpu.VMEM((ATTN_W, D_MODEL), _BF16),
            pltpu.VMEM((CONV_W, D_MODEL), _BF16),
            pltpu.VMEM((CROSS_W, D_MODEL), _BF16),
            pltpu.VMEM((D_MODEL, 2 * CROSS_W), _BF16),
            pltpu.VMEM((MEM_LEN, CROSS_W), _BF16),
            pltpu.VMEM((MEM_LEN, CROSS_W), _BF16),
            pltpu.VMEM((2, ts, D_MODEL), _F32),
            pltpu.VMEM((2, ts, D_MODEL), _F32),
            pltpu.VMEM((1, MEM_LEN, D_MODEL), _F32),
            pltpu.VMEM((2, 1, ts), jnp.int32),
            pltpu.SemaphoreType.DMA((2,)),
            pltpu.SemaphoreType.DMA((2,)),
            pltpu.SemaphoreType.DMA((1,)),
            pltpu.SemaphoreType.DMA((2,)),
        ],
        compiler_params=pltpu.CompilerParams(
            dimension_semantics=("arbitrary",),
            vmem_limit_bytes=VMEM_LIMIT_BYTES),
        name="trunk",
    )(attn_sinks, x, positions, mem, mem_norm_g, norm_g,
      w_in[0], w_out[0], w_o_attn[0], w_pw[0], w_o_cross[0], w_mem_kv[0], b_glu,
      w_dw, b_dw, ln_g, ln_b, b_pw, final_norm_g.reshape(1, D_MODEL))
```

```python
import math

import jax
import jax.numpy as jnp
from jax import lax
from jax.experimental import pallas as pl
from jax.experimental.pallas import tpu as pltpu

D_MODEL = 1024
MEM_LEN = 256
HEAD_DIM = 64
N_Q_HEADS = 8
N_KV_HEADS = 2
WINDOW = 128
BLOCK = 128
ATTN_W = N_Q_HEADS * HEAD_DIM
KV_W = N_KV_HEADS * HEAD_DIM
CONV_W = D_MODEL // 2
CONV_K = 31
N_CROSS_HEADS = 4
CROSS_HEAD_DIM = 128
CROSS_W = N_CROSS_HEADS * CROSS_HEAD_DIM
RMS_EPS = 1e-6
LN_EPS = 1e-5
NEG_BIG = -1e30
ALIBI_SLOPES = tuple(2.0 ** (-8.0 * (i + 1) / N_Q_HEADS) for i in range(N_Q_HEADS))

_Q0 = 0
_K0 = _Q0 + ATTN_W
_V0 = _K0 + KV_W
_GA0 = _V0 + KV_W
_UC0 = _GA0 + ATTN_W
_GC0 = _UC0 + 2 * CONV_W
_QX0 = _GC0 + CONV_W
_GX0 = _QX0 + CROSS_W
_MG0 = _GX0 + CROSS_W
IN_W = _MG0 + 3 * D_MODEL

LANES = 128
VREG_ROW_BYTES = 32
MXU_COLS = 256
SEQ_TILE = 512
CONV_CARRY = 32
CONV_ROWS = 128
OUT_CHUNKS = 4
LOAD_SLOTS = 12
LOAD_ROWS = 32
LOAD_ROWS_SMALL = 128
VMEM_LIMIT_BYTES = 56 * 1024 * 1024

_BF16 = jnp.bfloat16
_F32 = jnp.float32


def _dot(a, b):
    return jnp.dot(a, b, preferred_element_type=_F32)


def _dot_nt(a, b):
    return lax.dot_general(a, b, (((1,), (1,)), ((), ())), preferred_element_type=_F32)


def _rmsnorm(xf, g):
    y = xf * lax.rsqrt(jnp.mean(xf * xf, axis=-1, keepdims=True) + RMS_EPS)
    return y * g


def _sigmoid(x):
    return jax.nn.sigmoid(x)


def _silu(x):
    return x * jax.nn.sigmoid(x)


def _after(value, anchor):
    rows = VREG_ROW_BYTES // value.dtype.itemsize
    never = pl.program_id(0) < 0
    head = jnp.where(never, anchor[:rows, :LANES].astype(value.dtype), value[:rows, :LANES])
    top = jnp.concatenate([head, value[:rows, LANES:]], axis=1)
    return jnp.concatenate([top, value[rows:]], axis=0)


def _load_as_bf16(src_hbm, dst_ref, rows):
    n_chunks = src_hbm.shape[0] // rows

    def load(stage_ref, sem_ref):
        def chunk_copy(i, slot):
            return pltpu.make_async_copy(src_hbm.at[pl.ds(i * rows, rows)], stage_ref.at[slot],
                                         sem_ref.at[slot])

        for i in range(min(LOAD_SLOTS, n_chunks)):
            chunk_copy(i, i).start()

        def body(i, carry):
            slot = i % LOAD_SLOTS
            chunk_copy(i, slot).wait()
            dst_ref[pl.ds(pl.multiple_of(i * rows, rows), rows), :] = stage_ref[slot].astype(_BF16)

            @pl.when(i + LOAD_SLOTS < n_chunks)
            def _():
                chunk_copy(i + LOAD_SLOTS, slot).start()

            return carry

        lax.fori_loop(0, n_chunks, body, 0)

    pl.run_scoped(load, pltpu.VMEM((LOAD_SLOTS, rows, src_hbm.shape[1]), _F32),
                  pltpu.SemaphoreType.DMA((LOAD_SLOTS,)))


def _load_many_as_bf16(pairs, rows, while_first_in_flight=None):
    chunks = [(src, dst, r0) for src, dst in pairs for r0 in range(0, src.shape[0], rows)]
    width = pairs[0][0].shape[1]

    def load(stage_ref, sem_ref):
        def chunk_copy(k):
            src, _, r0 = chunks[k]
            slot = k % LOAD_SLOTS
            return pltpu.make_async_copy(src.at[pl.ds(r0, rows)], stage_ref.at[slot], sem_ref.at[slot])

        for k in range(min(LOAD_SLOTS, len(chunks))):
            chunk_copy(k).start()
        if while_first_in_flight is not None:
            while_first_in_flight()
        for k, (_, dst, r0) in enumerate(chunks):
            chunk_copy(k).wait()
            dst[pl.ds(r0, rows), :] = stage_ref[k % LOAD_SLOTS].astype(_BF16)
            if k + LOAD_SLOTS < len(chunks):
                chunk_copy(k + LOAD_SLOTS).start()

    pl.run_scoped(load, pltpu.VMEM((LOAD_SLOTS, rows, width), _F32),
                  pltpu.SemaphoreType.DMA((LOAD_SLOTS,)))


def _start_row(pos_ref, mem_ref, mem_g_ref, w_mkv_ref, kmask_ref, vmask_ref, kpos_ref, conv_ref,
               km_ref, vm_ref):
    kmask_ref[:, 0:BLOCK, :] = jnp.zeros((4, BLOCK, KV_W), _BF16)
    vmask_ref[:, 0:BLOCK, :] = jnp.zeros((4, BLOCK, KV_W), _BF16)
    conv_ref[:, 0:CONV_CARRY, :] = jnp.zeros((CONV_W // LANES, CONV_CARRY, LANES), _F32)
    kpos_ref[...] = jnp.broadcast_to(pos_ref[:, 0:1] - (WINDOW + 1), (1, BLOCK))
    mem_n = _rmsnorm(mem_ref[0], mem_g_ref[...]).astype(_BF16)
    mem_kv = _dot(mem_n, w_mkv_ref[...])
    km_ref[...] = mem_kv[:, :CROSS_W].astype(_BF16)
    vm_ref[...] = mem_kv[:, CROSS_W:].astype(_BF16)


def _tile(sinks_ref, x_ref, pos_ref, norm_g_ref, b_glu_ref, w_dw_ref, b_dw_ref, ln_g_ref, ln_b_ref,
          b_pw_ref, fng_ref, out_ref, kmask_ref, vmask_ref, kpos_ref, attn_ref, conv_ref,
          w_in_ref, w_out_ref, w_oa_ref, w_pw_ref, w_oc_ref, km_ref, vm_ref):
    ts = x_ref.shape[1]
    n_blk = ts // BLOCK
    n_col = D_MODEL // MXU_COLS

    x = x_ref[0]
    h = _rmsnorm(x, norm_g_ref[...]).astype(_BF16)

    def proj(c0, width=MXU_COLS):
        return _dot(h, w_in_ref[:, c0:c0 + width])

    def cols(c):
        return slice(c * MXU_COLS, (c + 1) * MXU_COLS)

    qkvg = proj(_Q0, _UC0 - _Q0)
    q = (qkvg[:, _Q0:_K0] * (1.0 / math.sqrt(HEAD_DIM))).astype(_BF16)
    k = qkvg[:, _K0:_V0]
    v = qkvg[:, _V0:_GA0]
    g_attn = qkvg[:, _GA0:_UC0]

    lo_half = lax.broadcasted_iota(jnp.int32, (ts, KV_W), 1) < HEAD_DIM
    zero = jnp.zeros((ts, KV_W), _F32)
    for src, dst_ref in ((k, kmask_ref), (v, vmask_ref)):
        rolled = pltpu.roll(src, HEAD_DIM, 1)
        dst_ref[0, BLOCK:, :] = jnp.where(lo_half, src, zero).astype(_BF16)
        dst_ref[1, BLOCK:, :] = jnp.where(lo_half, zero, rolled).astype(_BF16)
        dst_ref[2, BLOCK:, :] = jnp.where(lo_half, rolled, zero).astype(_BF16)
        dst_ref[3, BLOCK:, :] = jnp.where(lo_half, zero, src).astype(_BF16)

    lo_out = lax.broadcasted_iota(jnp.int32, (2 * BLOCK, 2 * HEAD_DIM), 1) < HEAD_DIM
    band_cache = {}

    def pos_blk(j):
        return pos_ref[:, j * BLOCK:(j + 1) * BLOCK]

    def band_terms(j):
        if j not in band_cache:
            qcol = jnp.broadcast_to(pos_blk(j), (BLOCK, BLOCK)).T
            prev_pos = kpos_ref[...] if j == 0 else pos_blk(j - 1)
            delta = jnp.concatenate([qcol - prev_pos, qcol - pos_blk(j)], axis=1)
            band_cache[j] = ((delta >= 0) & (delta < WINDOW), delta.astype(_F32))
        return band_cache[j]

    def scores(j, g):
        rows = slice(j * BLOCK, (j + 1) * BLOCK)
        band = slice(j * BLOCK, (j + 2) * BLOCK)
        qg = jnp.concatenate([q[rows, (2 * g) * LANES:(2 * g + 1) * LANES],
                              q[rows, (2 * g + 1) * LANES:(2 * g + 2) * LANES]], axis=0)
        return [_dot_nt(qg, kmask_ref[2 * g + a, band, :]) for a in range(2)]

    def attend(j, g, s_pair, anchors):
        rows = slice(j * BLOCK, (j + 1) * BLOCK)
        band = slice(j * BLOCK, (j + 2) * BLOCK)
        allowed, delta_f = band_terms(j)
        probs = []
        inv_l = []
        for a in range(2):
            s = s_pair[a]
            e_parts = []
            l_parts = []
            for r in range(2):
                head = 2 * (2 * g + r) + a
                sr = s[r * BLOCK:(r + 1) * BLOCK]
                sr = jnp.where(allowed, sr - ALIBI_SLOPES[head] * delta_f, NEG_BIG)
                sink = sinks_ref[0, head]
                m = jnp.maximum(jnp.max(sr, axis=1, keepdims=True), sink)
                e = jnp.exp(sr - m)
                l_parts.append(jnp.sum(e, axis=1, keepdims=True) + jnp.exp(sink - m))
                e_parts.append(e.astype(_BF16))
            p_a = jnp.concatenate(e_parts, axis=0)
            if a < len(anchors):
                p_a = _after(p_a, anchors[a])
            probs.append(p_a)
            inv_l.append(1.0 / jnp.concatenate(l_parts, axis=0))
        o = (_dot(probs[0], vmask_ref[2 * g, band, :])
             + _dot(probs[1], vmask_ref[2 * g + 1, band, :]))
        o = o * jnp.where(lo_out, inv_l[0], inv_l[1])
        attn_ref[rows, (2 * g) * LANES:(2 * g + 1) * LANES] = o[:BLOCK]
        attn_ref[rows, (2 * g + 1) * LANES:(2 * g + 2) * LANES] = o[BLOCK:]

    n_uc = (_QX0 - _UC0) // MXU_COLS
    attn_todo = [(j, g) for j in range(n_blk) for g in range(N_KV_HEADS)]
    s_all = [scores(j, g) for j, g in attn_todo]
    uc = [_dot(_after(h, s_all[i % len(attn_todo)][0]), w_in_ref[:, _UC0 + i * MXU_COLS:_UC0 + (i + 1) * MXU_COLS])
          for i in range(n_uc)]
    for i, (j, g) in enumerate(attn_todo):
        attend(j, g, s_all[i], uc[i::len(attn_todo)])

    kmask_ref[:, 0:BLOCK, :] = kmask_ref[:, ts:ts + BLOCK, :]
    vmask_ref[:, 0:BLOCK, :] = vmask_ref[:, ts:ts + BLOCK, :]
    kpos_ref[...] = pos_blk(n_blk - 1)

    n_glu = CONV_W // MXU_COLS
    for c in range(n_glu):
        a_c = ((uc[c] + b_glu_ref[:, cols(c)])
               * _sigmoid(uc[n_glu + c] + b_glu_ref[:, CONV_W + c * MXU_COLS:CONV_W + (c + 1) * MXU_COLS]))
        for i in range(MXU_COLS // LANES):
            conv_ref[c * (MXU_COLS // LANES) + i, CONV_CARRY:, :] = a_c[:, i * LANES:(i + 1) * LANES]
    g_conv = jnp.concatenate(uc[2 * n_glu:], axis=1)

    first_tap_row = CONV_CARRY - (CONV_K - 1)
    conv_out = {}

    def conv_piece(lc, r0):
        lanes = slice(lc * LANES, (lc + 1) * LANES)
        acc = jnp.broadcast_to(b_dw_ref[:, lanes], (CONV_ROWS, LANES))
        for t in range(CONV_K):
            off = r0 + first_tap_row + t
            acc = acc + w_dw_ref[0, t, :, lanes] * conv_ref[lc, off:off + CONV_ROWS, :]
        conv_out[(lc, r0)] = acc

    conv_act = {}

    def conv_norm(r0):
        c = jnp.concatenate([conv_out.pop((lc, r0)) for lc in range(CONV_W // LANES)], axis=1)
        mu = jnp.mean(c, axis=-1, keepdims=True)
        cc = c - mu
        var = jnp.mean(cc * cc, axis=-1, keepdims=True)
        cn = cc * lax.rsqrt(var + LN_EPS) * ln_g_ref[...] + ln_b_ref[...]
        conv_act[r0] = (_silu(cn) * _silu(g_conv[r0:r0 + CONV_ROWS])).astype(_BF16)

    cross_out = {}
    qx = {}

    def cross_head(hh):
        per_chunk = MXU_COLS // CROSS_HEAD_DIM
        qh = qx[hh // per_chunk][:, (hh % per_chunk) * CROSS_HEAD_DIM:(hh % per_chunk + 1) * CROSS_HEAD_DIM]
        kv_cols = slice(hh * CROSS_HEAD_DIM, (hh + 1) * CROSS_HEAD_DIM)
        s = _dot_nt(qh.astype(_BF16), km_ref[:, kv_cols]) * (1.0 / math.sqrt(CROSS_HEAD_DIM))
        m = jnp.max(s, axis=1, keepdims=True)
        e = jnp.exp(s - m)
        l = jnp.sum(e, axis=1, keepdims=True)
        cross_out[hh] = _dot(e.astype(_BF16), vm_ref[:, kv_cols]) * (1.0 / l)

    vec_todo = []
    for r0 in range(0, ts, CONV_ROWS):
        vec_todo += [(conv_piece, (lc, r0)) for lc in range(CONV_W // LANES)]
        vec_todo.append((conv_norm, (r0,)))
    n_qx = (_MG0 - _QX0) // MXU_COLS
    gates = []
    for i in range(n_qx + 3 * n_col):
        if i < n_qx:
            qx[i] = proj(_QX0 + i * MXU_COLS)
            if i == CROSS_W // MXU_COLS - 1:
                vec_todo[2:2] = [(cross_head, (hh,)) for hh in range(N_CROSS_HEADS)]
        else:
            gates.append(_sigmoid(proj(_MG0 + (i - n_qx) * MXU_COLS)))
        if vec_todo:
            fn, args = vec_todo.pop(0)
            fn(*args)
    while vec_todo:
        fn, args = vec_todo.pop(0)
        fn(*args)
    conv_ref[:, 0:CONV_CARRY, :] = conv_ref[:, ts:ts + CONV_CARRY, :]

    g_cross = jnp.concatenate([qx[i] for i in range(CROSS_W // MXU_COLS, n_qx)], axis=1)
    attn_act = (attn_ref[...] * _silu(g_attn)).astype(_BF16)
    conv_all = jnp.concatenate([conv_act[r0] for r0 in range(0, ts, CONV_ROWS)], axis=0)
    cross_act = (jnp.concatenate([cross_out[hh] for hh in range(N_CROSS_HEADS)], axis=1)
                 * _silu(g_cross)).astype(_BF16)
    merged = []
    wide = OUT_CHUNKS * MXU_COLS
    for c in range(D_MODEL // wide):
        wcols = slice(c * wide, (c + 1) * wide)
        gate = [jnp.concatenate(gates[br * n_col + OUT_CHUNKS * c:br * n_col + OUT_CHUNKS * (c + 1)], axis=1)
                for br in range(3)]
        y_attn = _dot(attn_act, w_oa_ref[:, wcols])
        y_conv = _dot(conv_all, w_pw_ref[:, wcols]) + b_pw_ref[:, wcols]
        y_cross = _dot(cross_act, w_oc_ref[:, wcols])
        merged.append((gate[0] * y_attn + gate[1] * y_conv + gate[2] * y_cross).astype(_BF16))
    merged = jnp.concatenate(merged, axis=1)

    x_new = jnp.concatenate([x[:, c * wide:(c + 1) * wide] + _dot(merged, w_out_ref[:, c * wide:(c + 1) * wide])
                             for c in range(D_MODEL // wide)], axis=1)
    out_ref[0] = _rmsnorm(x_new, fng_ref[...])


def _trunk_kernel(sinks_ref, x_hbm, pos_hbm, mem_hbm, mem_g_ref, norm_g_ref, w_in_hbm,
                  w_out_hbm, w_oa_hbm, w_pw_hbm, w_oc_hbm, w_mkv_hbm, b_glu_ref, w_dw_ref, b_dw_ref,
                  ln_g_ref, ln_b_ref, b_pw_ref, fng_ref, out_hbm,
                  kmask_ref, vmask_ref, kpos_ref, attn_ref, conv_ref,
                  w_in_ref, w_out_ref, w_oa_ref, w_pw_ref, w_oc_ref, w_mkv_ref, km_ref, vm_ref,
                  x_buf, out_buf, mem_buf, pos_buf, x_sem, out_sem, mem_sem, pos_sem):
    batch, seq, _ = x_hbm.shape
    ts = x_buf.shape[1]
    n_seq = seq // ts
    n_tiles = batch * n_seq

    def tile_rows(t):
        if isinstance(t, int):
            return t // n_seq, pl.ds((t % n_seq) * ts, ts)
        return t // n_seq, pl.ds(pl.multiple_of((t % n_seq) * ts, ts), ts)

    def x_copy(t):
        b, rows = tile_rows(t)
        return pltpu.make_async_copy(x_hbm.at[b, rows], x_buf.at[t % 2], x_sem.at[t % 2])

    def pos_copy(t):
        b, rows = tile_rows(t)
        return pltpu.make_async_copy(pos_hbm.at[pl.ds(b, 1), rows], pos_buf.at[t % 2], pos_sem.at[t % 2])

    def out_copy(t):
        b, rows = tile_rows(t)
        return pltpu.make_async_copy(out_buf.at[t % 2], out_hbm.at[b, rows], out_sem.at[t % 2])

    def mem_copy(b):
        return pltpu.make_async_copy(mem_hbm.at[b], mem_buf.at[0], mem_sem.at[0])

    x_copy(0).start()
    pos_copy(0).start()
    mem_copy(0).start()
    _load_many_as_bf16([(w_mkv_hbm, w_mkv_ref), (w_out_hbm, w_out_ref), (w_oa_hbm, w_oa_ref),
                        (w_pw_hbm, w_pw_ref), (w_oc_hbm, w_oc_ref)], LOAD_ROWS_SMALL,
                       while_first_in_flight=lambda: _load_as_bf16(w_in_hbm, w_in_ref, LOAD_ROWS))
    pos_copy(0).wait()

    def batch_row(b, carry):
        mem_copy(b).wait()
        _start_row(pos_buf.at[(b * n_seq) % 2], mem_buf, mem_g_ref, w_mkv_ref,
                   kmask_ref, vmask_ref, kpos_ref, conv_ref, km_ref, vm_ref)

        @pl.when(b + 1 < batch)
        def _():
            mem_copy(b + 1).start()

        def tile(s, carry):
            t = b * n_seq + s

            @pl.when(t + 1 < n_tiles)
            def _():
                x_copy(t + 1).start()
                pos_copy(t + 1).start()

            x_copy(t).wait()

            @pl.when(t >= 2)
            def _():
                out_copy(t - 2).wait()

            slot = t % 2
            _tile(sinks_ref, x_buf.at[pl.ds(slot, 1)],
                  pos_buf.at[slot],
                  norm_g_ref, b_glu_ref, w_dw_ref, b_dw_ref, ln_g_ref, ln_b_ref, b_pw_ref, fng_ref,
                  out_buf.at[pl.ds(slot, 1)], kmask_ref, vmask_ref, kpos_ref, attn_ref, conv_ref,
                  w_in_ref, w_out_ref, w_oa_ref, w_pw_ref, w_oc_ref, km_ref, vm_ref)
            out_copy(t).start()

            @pl.when(t + 1 < n_tiles)
            def _():
                pos_copy(t + 1).wait()

            return carry

        return lax.fori_loop(0, n_seq, tile, carry)

    lax.fori_loop(0, batch, batch_row, 0)
    out_copy(n_tiles - 2).wait()
    out_copy(n_tiles - 1).wait()


def _const_spec(shape):
    nd = len(shape)
    return pl.BlockSpec(shape, lambda i: (0,) * nd, pipeline_mode=pl.Buffered(1))


@jax.jit
def kernel(x, mem, positions, norm_g, w_in, attn_sinks, w_o_attn, b_glu, w_dw, b_dw, ln_g, ln_b,
           w_pw, b_pw, mem_norm_g, w_mem_kv, w_o_cross, w_out, final_norm_g):
    batch, seq, d_model = x.shape
    assert d_model == D_MODEL and norm_g.shape[0] == 1 and w_in.shape == (1, D_MODEL, IN_W)
    assert seq % SEQ_TILE == 0 and SEQ_TILE % BLOCK == 0 and mem.shape[1] == MEM_LEN
    ts = SEQ_TILE

    hbm = pl.BlockSpec(memory_space=pl.ANY)
    in_specs = [
        pl.BlockSpec(memory_space=pltpu.SMEM),
        hbm,
        hbm,
        hbm,
        _const_spec((1, D_MODEL)),
        _const_spec((1, D_MODEL)),
        hbm, hbm, hbm, hbm, hbm, hbm,
        _const_spec((1, 2 * CONV_W)),
        pl.BlockSpec(memory_space=pltpu.VMEM),
        _const_spec((1, CONV_W)),
        _const_spec((1, CONV_W)),
        _const_spec((1, CONV_W)),
        _const_spec((1, D_MODEL)),
        _const_spec((1, D_MODEL)),
    ]
    return pl.pallas_call(
        _trunk_kernel,
        grid=(1,),
        in_specs=in_specs,
        out_specs=hbm,
        out_shape=jax.ShapeDtypeStruct((batch, seq, D_MODEL), x.dtype),
        scratch_shapes=[
            pltpu.VMEM((4, ts + BLOCK, KV_W), _BF16),
            pltpu.VMEM((4, ts + BLOCK, KV_W), _BF16),
            pltpu.VMEM((1, BLOCK), jnp.int32),
            pltpu.VMEM((ts, ATTN_W), _F32),
            pltpu.VMEM((CONV_W // LANES, ts + CONV_CARRY, LANES), _F32),
            pltpu.VMEM((D_MODEL, IN_W), _BF16),
            pltpu.VMEM((D_MODEL, D_MODEL), _BF16),
            pltpu.VMEM((ATTN_W, D_MODEL), _BF16),
            pltpu.VMEM((CONV_W, D_MODEL), _BF16),
            pltpu.VMEM((CROSS_W, D_MODEL), _BF16),
            pltpu.VMEM((D_MODEL, 2 * CROSS_W), _BF16),
            pltpu.VMEM((MEM_LEN, CROSS_W), _BF16),
            pltpu.VMEM((MEM_LEN, CROSS_W), _BF16),
            pltpu.VMEM((2, ts, D_MODEL), _F32),
            pltpu.VMEM((2, ts, D_MODEL), _F32),
            pltpu.VMEM((1, MEM_LEN, D_MODEL), _F32),
            pltpu.VMEM((2, 1, ts), jnp.int32),
            pltpu.SemaphoreType.DMA((2,)),
            pltpu.SemaphoreType.DMA((2,)),
            pltpu.SemaphoreType.DMA((1,)),
            pltpu.SemaphoreType.DMA((2,)),
        ],
        compiler_params=pltpu.CompilerParams(
            dimension_semantics=("arbitrary",),
            vmem_limit_bytes=VMEM_LIMIT_BYTES),
        name="trunk",
    )(attn_sinks, x, positions, mem, mem_norm_g, norm_g,
      w_in[0], w_out[0], w_o_attn[0], w_pw[0], w_o_cross[0], w_mem_kv[0], b_glu,
      w_dw, b_dw, ln_g, ln_b, b_pw, final_norm_g.reshape(1, D_MODEL))
```

```python
import math

import jax
import jax.numpy as jnp
from jax import lax
from jax.experimental import pallas as pl
from jax.experimental.pallas import tpu as pltpu

D_MODEL = 1024
MEM_LEN = 256
HEAD_DIM = 64
N_Q_HEADS = 8
N_KV_HEADS = 2
WINDOW = 128
BLOCK = 128
ATTN_W = N_Q_HEADS * HEAD_DIM
KV_W = N_KV_HEADS * HEAD_DIM
CONV_W = D_MODEL // 2
CONV_K = 31
N_CROSS_HEADS = 4
CROSS_HEAD_DIM = 128
CROSS_W = N_CROSS_HEADS * CROSS_HEAD_DIM
RMS_EPS = 1e-6
LN_EPS = 1e-5
NEG_BIG = -1e30
ALIBI_SLOPES = tuple(2.0 ** (-8.0 * (i + 1) / N_Q_HEADS) for i in range(N_Q_HEADS))

_Q0 = 0
_K0 = _Q0 + ATTN_W
_V0 = _K0 + KV_W
_GA0 = _V0 + KV_W
_UC0 = _GA0 + ATTN_W
_GC0 = _UC0 + 2 * CONV_W
_QX0 = _GC0 + CONV_W
_GX0 = _QX0 + CROSS_W
_MG0 = _GX0 + CROSS_W
IN_W = _MG0 + 3 * D_MODEL

LANES = 128
VREG_ROW_BYTES = 32
MXU_COLS = 256
SEQ_TILE = 512
CONV_CARRY = 32
CONV_ROWS = 128
OUT_CHUNKS = 4
LOAD_SLOTS = 12
LOAD_ROWS = 32
LOAD_ROWS_SMALL = 128
VMEM_LIMIT_BYTES = 56 * 1024 * 1024

_BF16 = jnp.bfloat16
_F32 = jnp.float32


def _dot(a, b):
    return jnp.dot(a, b, preferred_element_type=_F32)


def _dot_nt(a, b):
    return lax.dot_general(a, b, (((1,), (1,)), ((), ())), preferred_element_type=_F32)


def _rmsnorm(xf, g):
    y = xf * lax.rsqrt(jnp.mean(xf * xf, axis=-1, keepdims=True) + RMS_EPS)
    return y * g


def _sigmoid(x):
    return jax.nn.sigmoid(x)


def _silu(x):
    return x * jax.nn.sigmoid(x)


def _after(value, anchor):
    rows = VREG_ROW_BYTES // value.dtype.itemsize
    never = pl.program_id(0) < 0
    head = jnp.where(never, anchor[:rows, :LANES].astype(value.dtype), value[:rows, :LANES])
    top = jnp.concatenate([head, value[:rows, LANES:]], axis=1)
    return jnp.concatenate([top, value[rows:]], axis=0)


def _load_as_bf16(src_hbm, dst_ref, rows):
    n_chunks = src_hbm.shape[0] // rows

    def load(stage_ref, sem_ref):
        def chunk_copy(i, slot):
            return pltpu.make_async_copy(src_hbm.at[pl.ds(i * rows, rows)], stage_ref.at[slot],
                                         sem_ref.at[slot])

        for i in range(min(LOAD_SLOTS, n_chunks)):
            chunk_copy(i, i).start()

        def body(i, carry):
            slot = i % LOAD_SLOTS
            chunk_copy(i, slot).wait()
            dst_ref[pl.ds(pl.multiple_of(i * rows, rows), rows), :] = stage_ref[slot].astype(_BF16)

            @pl.when(i + LOAD_SLOTS < n_chunks)
            def _():
                chunk_copy(i + LOAD_SLOTS, slot).start()

            return carry

        lax.fori_loop(0, n_chunks, body, 0)

    pl.run_scoped(load, pltpu.VMEM((LOAD_SLOTS, rows, src_hbm.shape[1]), _F32),
                  pltpu.SemaphoreType.DMA((LOAD_SLOTS,)))


def _load_many_as_bf16(pairs, rows):
    chunks = [(src, dst, r0) for src, dst in pairs for r0 in range(0, src.shape[0], rows)]
    width = pairs[0][0].shape[1]

    def load(stage_ref, sem_ref):
        def chunk_copy(k):
            src, _, r0 = chunks[k]
            slot = k % LOAD_SLOTS
            return pltpu.make_async_copy(src.at[pl.ds(r0, rows)], stage_ref.at[slot], sem_ref.at[slot])

        for k in range(min(LOAD_SLOTS, len(chunks))):
            chunk_copy(k).start()
        for k, (_, dst, r0) in enumerate(chunks):
            chunk_copy(k).wait()
            dst[pl.ds(r0, rows), :] = stage_ref[k % LOAD_SLOTS].astype(_BF16)
            if k + LOAD_SLOTS < len(chunks):
                chunk_copy(k + LOAD_SLOTS).start()

    pl.run_scoped(load, pltpu.VMEM((LOAD_SLOTS, rows, width), _F32),
                  pltpu.SemaphoreType.DMA((LOAD_SLOTS,)))


def _start_row(pos_ref, mem_ref, mem_g_ref, w_mkv_ref, kmask_ref, vmask_ref, kpos_ref, conv_ref,
               km_ref, vm_ref):
    kmask_ref[:, 0:BLOCK, :] = jnp.zeros((4, BLOCK, KV_W), _BF16)
    vmask_ref[:, 0:BLOCK, :] = jnp.zeros((4, BLOCK, KV_W), _BF16)
    conv_ref[:, 0:CONV_CARRY, :] = jnp.zeros((CONV_W // LANES, CONV_CARRY, LANES), _F32)
    kpos_ref[...] = jnp.broadcast_to(pos_ref[:, 0:1] - (WINDOW + 1), (1, BLOCK))
    mem_n = _rmsnorm(mem_ref[0], mem_g_ref[...]).astype(_BF16)
    mem_kv = _dot(mem_n, w_mkv_ref[...])
    km_ref[...] = mem_kv[:, :CROSS_W].astype(_BF16)
    vm_ref[...] = mem_kv[:, CROSS_W:].astype(_BF16)


def _tile(sinks_ref, x_ref, pos_ref, norm_g_ref, b_glu_ref, w_dw_ref, b_dw_ref, ln_g_ref, ln_b_ref,
          b_pw_ref, fng_ref, out_ref, kmask_ref, vmask_ref, kpos_ref, attn_ref, conv_ref,
          w_in_ref, w_out_ref, w_oa_ref, w_pw_ref, w_oc_ref, km_ref, vm_ref):
    ts = x_ref.shape[1]
    n_blk = ts // BLOCK
    n_col = D_MODEL // MXU_COLS

    x = x_ref[0]
    h = _rmsnorm(x, norm_g_ref[...]).astype(_BF16)

    def proj(c0, width=MXU_COLS):
        return _dot(h, w_in_ref[:, c0:c0 + width])

    def cols(c):
        return slice(c * MXU_COLS, (c + 1) * MXU_COLS)

    qkvg = proj(_Q0, _UC0 - _Q0)
    q = (qkvg[:, _Q0:_K0] * (1.0 / math.sqrt(HEAD_DIM))).astype(_BF16)
    k = qkvg[:, _K0:_V0]
    v = qkvg[:, _V0:_GA0]
    g_attn = qkvg[:, _GA0:_UC0]

    lo_half = lax.broadcasted_iota(jnp.int32, (ts, KV_W), 1) < HEAD_DIM
    zero = jnp.zeros((ts, KV_W), _F32)
    for src, dst_ref in ((k, kmask_ref), (v, vmask_ref)):
        rolled = pltpu.roll(src, HEAD_DIM, 1)
        dst_ref[0, BLOCK:, :] = jnp.where(lo_half, src, zero).astype(_BF16)
        dst_ref[1, BLOCK:, :] = jnp.where(lo_half, zero, rolled).astype(_BF16)
        dst_ref[2, BLOCK:, :] = jnp.where(lo_half, rolled, zero).astype(_BF16)
        dst_ref[3, BLOCK:, :] = jnp.where(lo_half, zero, src).astype(_BF16)

    lo_out = lax.broadcasted_iota(jnp.int32, (2 * BLOCK, 2 * HEAD_DIM), 1) < HEAD_DIM
    band_cache = {}

    def pos_blk(j):
        return pos_ref[:, j * BLOCK:(j + 1) * BLOCK]

    def band_terms(j):
        if j not in band_cache:
            qcol = jnp.broadcast_to(pos_blk(j), (BLOCK, BLOCK)).T
            prev_pos = kpos_ref[...] if j == 0 else pos_blk(j - 1)
            delta = jnp.concatenate([qcol - prev_pos, qcol - pos_blk(j)], axis=1)
            band_cache[j] = ((delta >= 0) & (delta < WINDOW), delta.astype(_F32))
        return band_cache[j]

    def scores(j, g):
        rows = slice(j * BLOCK, (j + 1) * BLOCK)
        band = slice(j * BLOCK, (j + 2) * BLOCK)
        qg = jnp.concatenate([q[rows, (2 * g) * LANES:(2 * g + 1) * LANES],
                              q[rows, (2 * g + 1) * LANES:(2 * g + 2) * LANES]], axis=0)
        return [_dot_nt(qg, kmask_ref[2 * g + a, band, :]) for a in range(2)]

    def attend(j, g, s_pair, anchors):
        rows = slice(j * BLOCK, (j + 1) * BLOCK)
        band = slice(j * BLOCK, (j + 2) * BLOCK)
        allowed, delta_f = band_terms(j)
        probs = []
        inv_l = []
        for a in range(2):
            s = s_pair[a]
            e_parts = []
            l_parts = []
            for r in range(2):
                head = 2 * (2 * g + r) + a
                sr = s[r * BLOCK:(r + 1) * BLOCK]
                sr = jnp.where(allowed, sr - ALIBI_SLOPES[head] * delta_f, NEG_BIG)
                sink = sinks_ref[0, head]
                m = jnp.maximum(jnp.max(sr, axis=1, keepdims=True), sink)
                e = jnp.exp(sr - m)
                l_parts.append(jnp.sum(e, axis=1, keepdims=True) + jnp.exp(sink - m))
                e_parts.append(e.astype(_BF16))
            p_a = jnp.concatenate(e_parts, axis=0)
            if a < len(anchors):
                p_a = _after(p_a, anchors[a])
            probs.append(p_a)
            inv_l.append(1.0 / jnp.concatenate(l_parts, axis=0))
        o = (_dot(probs[0], vmask_ref[2 * g, band, :])
             + _dot(probs[1], vmask_ref[2 * g + 1, band, :]))
        o = o * jnp.where(lo_out, inv_l[0], inv_l[1])
        attn_ref[rows, (2 * g) * LANES:(2 * g + 1) * LANES] = o[:BLOCK]
        attn_ref[rows, (2 * g + 1) * LANES:(2 * g + 2) * LANES] = o[BLOCK:]

    n_uc = (_QX0 - _UC0) // MXU_COLS
    attn_todo = [(j, g) for j in range(n_blk) for g in range(N_KV_HEADS)]
    s_all = [scores(j, g) for j, g in attn_todo]
    uc = [_dot(_after(h, s_all[i % len(attn_todo)][0]), w_in_ref[:, _UC0 + i * MXU_COLS:_UC0 + (i + 1) * MXU_COLS])
          for i in range(n_uc)]
    for i, (j, g) in enumerate(attn_todo):
        attend(j, g, s_all[i], uc[i::len(attn_todo)])

    kmask_ref[:, 0:BLOCK, :] = kmask_ref[:, ts:ts + BLOCK, :]
    vmask_ref[:, 0:BLOCK, :] = vmask_ref[:, ts:ts + BLOCK, :]
    kpos_ref[...] = pos_blk(n_blk - 1)

    n_glu = CONV_W // MXU_COLS
    for c in range(n_glu):
        a_c = ((uc[c] + b_glu_ref[:, cols(c)])
               * _sigmoid(uc[n_glu + c] + b_glu_ref[:, CONV_W + c * MXU_COLS:CONV_W + (c + 1) * MXU_COLS]))
        for i in range(MXU_COLS // LANES):
            conv_ref[c * (MXU_COLS // LANES) + i, CONV_CARRY:, :] = a_c[:, i * LANES:(i + 1) * LANES]
    g_conv = jnp.concatenate(uc[2 * n_glu:], axis=1)

    first_tap_row = CONV_CARRY - (CONV_K - 1)
    conv_out = {}

    def conv_piece(lc, r0):
        lanes = slice(lc * LANES, (lc + 1) * LANES)
        acc = jnp.broadcast_to(b_dw_ref[:, lanes], (CONV_ROWS, LANES))
        for t in range(CONV_K):
            off = r0 + first_tap_row + t
            acc = acc + w_dw_ref[0, t, :, lanes] * conv_ref[lc, off:off + CONV_ROWS, :]
        conv_out[(lc, r0)] = acc

    conv_act = {}

    def conv_norm(r0):
        c = jnp.concatenate([conv_out.pop((lc, r0)) for lc in range(CONV_W // LANES)], axis=1)
        mu = jnp.mean(c, axis=-1, keepdims=True)
        cc = c - mu
        var = jnp.mean(cc * cc, axis=-1, keepdims=True)
        cn = cc * lax.rsqrt(var + LN_EPS) * ln_g_ref[...] + ln_b_ref[...]
        conv_act[r0] = (_silu(cn) * _silu(g_conv[r0:r0 + CONV_ROWS])).astype(_BF16)

    cross_out = {}
    qx = {}

    def cross_head(hh):
        per_chunk = MXU_COLS // CROSS_HEAD_DIM
        qh = qx[hh // per_chunk][:, (hh % per_chunk) * CROSS_HEAD_DIM:(hh % per_chunk + 1) * CROSS_HEAD_DIM]
        kv_cols = slice(hh * CROSS_HEAD_DIM, (hh + 1) * CROSS_HEAD_DIM)
        s = _dot_nt(qh.astype(_BF16), km_ref[:, kv_cols]) * (1.0 / math.sqrt(CROSS_HEAD_DIM))
        m = jnp.max(s, axis=1, keepdims=True)
        e = jnp.exp(s - m)
        l = jnp.sum(e, axis=1, keepdims=True)
        cross_out[hh] = _dot(e.astype(_BF16), vm_ref[:, kv_cols]) * (1.0 / l)

    vec_todo = []
    for r0 in range(0, ts, CONV_ROWS):
        vec_todo += [(conv_piece, (lc, r0)) for lc in range(CONV_W // LANES)]
        vec_todo.append((conv_norm, (r0,)))
    n_qx = (_MG0 - _QX0) // MXU_COLS
    gates = []
    for i in range(n_qx + 3 * n_col):
        if i < n_qx:
            qx[i] = proj(_QX0 + i * MXU_COLS)
            if i == CROSS_W // MXU_COLS - 1:
                vec_todo[2:2] = [(cross_head, (hh,)) for hh in range(N_CROSS_HEADS)]
        else:
            gates.append(_sigmoid(proj(_MG0 + (i - n_qx) * MXU_COLS)))
        if vec_todo:
            fn, args = vec_todo.pop(0)
            fn(*args)
    while vec_todo:
        fn, args = vec_todo.pop(0)
        fn(*args)
    conv_ref[:, 0:CONV_CARRY, :] = conv_ref[:, ts:ts + CONV_CARRY, :]

    g_cross = jnp.concatenate([qx[i] for i in range(CROSS_W // MXU_COLS, n_qx)], axis=1)
    attn_act = (attn_ref[...] * _silu(g_attn)).astype(_BF16)
    conv_all = jnp.concatenate([conv_act[r0] for r0 in range(0, ts, CONV_ROWS)], axis=0)
    cross_act = (jnp.concatenate([cross_out[hh] for hh in range(N_CROSS_HEADS)], axis=1)
                 * _silu(g_cross)).astype(_BF16)
    merged = []
    wide = OUT_CHUNKS * MXU_COLS
    for c in range(D_MODEL // wide):
        wcols = slice(c * wide, (c + 1) * wide)
        gate = [jnp.concatenate(gates[br * n_col + OUT_CHUNKS * c:br * n_col + OUT_CHUNKS * (c + 1)], axis=1)
                for br in range(3)]
        y_attn = _dot(attn_act, w_oa_ref[:, wcols])
        y_conv = _dot(conv_all, w_pw_ref[:, wcols]) + b_pw_ref[:, wcols]
        y_cross = _dot(cross_act, w_oc_ref[:, wcols])
        merged.append((gate[0] * y_attn + gate[1] * y_conv + gate[2] * y_cross).astype(_BF16))
    merged = jnp.concatenate(merged, axis=1)

    x_new = jnp.concatenate([x[:, c * wide:(c + 1) * wide] + _dot(merged, w_out_ref[:, c * wide:(c + 1) * wide])
                             for c in range(D_MODEL // wide)], axis=1)
    out_ref[0] = _rmsnorm(x_new, fng_ref[...])


def _trunk_kernel(sinks_ref, x_hbm, pos_hbm, mem_hbm, mem_g_ref, norm_g_ref, w_in_hbm,
                  w_out_hbm, w_oa_hbm, w_pw_hbm, w_oc_hbm, w_mkv_hbm, b_glu_ref, w_dw_ref, b_dw_ref,
                  ln_g_ref, ln_b_ref, b_pw_ref, fng_ref, out_hbm,
                  kmask_ref, vmask_ref, attn_ref, conv_ref,
                  w_in_ref, w_out_ref, w_oa_ref, w_pw_ref, w_oc_ref, w_mkv_ref, km_ref, vm_ref,
                  x_buf, out_buf, mem_buf, pos_buf, kpos_ref, x_sem, out_sem, mem_sem, pos_sem):
    batch, seq, _ = x_hbm.shape
    ts = x_buf.shape[1]
    n_seq = seq // ts
    n_tiles = batch * n_seq

    def tile_rows(t):
        if isinstance(t, int):
            return t // n_seq, pl.ds((t % n_seq) * ts, ts)
        return t // n_seq, pl.ds(pl.multiple_of((t % n_seq) * ts, ts), ts)

    def x_copy(t):
        b, rows = tile_rows(t)
        return pltpu.make_async_copy(x_hbm.at[b, rows], x_buf.at[t % 2], x_sem.at[t % 2])

    def pos_copy(t):
        b, rows = tile_rows(t)
        return pltpu.make_async_copy(pos_hbm.at[pl.ds(b, 1), rows], pos_buf.at[t % 2], pos_sem.at[t % 2])

    def out_copy(t):
        b, rows = tile_rows(t)
        return pltpu.make_async_copy(out_buf.at[t % 2], out_hbm.at[b, rows], out_sem.at[t % 2])

    def mem_copy(b):
        return pltpu.make_async_copy(mem_hbm.at[b], mem_buf.at[0], mem_sem.at[0])

    x_copy(0).start()
    pos_copy(0).start()
    mem_copy(0).start()
    _load_as_bf16(w_in_hbm, w_in_ref, LOAD_ROWS)
    _load_many_as_bf16([(w_mkv_hbm, w_mkv_ref), (w_out_hbm, w_out_ref), (w_oa_hbm, w_oa_ref),
                        (w_pw_hbm, w_pw_ref), (w_oc_hbm, w_oc_ref)], LOAD_ROWS_SMALL)
    pos_copy(0).wait()

    def batch_row(b, carry):
        mem_copy(b).wait()
        _start_row(pos_buf.at[(b * n_seq) % 2], mem_buf, mem_g_ref, w_mkv_ref,
                   kmask_ref, vmask_ref, kpos_ref, conv_ref, km_ref, vm_ref)

        @pl.when(b + 1 < batch)
        def _():
            mem_copy(b + 1).start()

        def tile(s, carry):
            t = b * n_seq + s

            @pl.when(t + 1 < n_tiles)
            def _():
                x_copy(t + 1).start()
                pos_copy(t + 1).start()

            x_copy(t).wait()

            @pl.when(t >= 2)
            def _():
                out_copy(t - 2).wait()

            slot = t % 2
            _tile(sinks_ref, x_buf.at[pl.ds(slot, 1)],
                  pos_buf.at[slot],
                  norm_g_ref, b_glu_ref, w_dw_ref, b_dw_ref, ln_g_ref, ln_b_ref, b_pw_ref, fng_ref,
                  out_buf.at[pl.ds(slot, 1)], kmask_ref, vmask_ref, kpos_ref, attn_ref, conv_ref,
                  w_in_ref, w_out_ref, w_oa_ref, w_pw_ref, w_oc_ref, km_ref, vm_ref)
            out_copy(t).start()

            @pl.when(t + 1 < n_tiles)
            def _():
                pos_copy(t + 1).wait()

            return carry

        return lax.fori_loop(0, n_seq, tile, carry)

    lax.fori_loop(0, batch, batch_row, 0)
    out_copy(n_tiles - 2).wait()
    out_copy(n_tiles - 1).wait()


def _const_spec(shape):
    nd = len(shape)
    return pl.BlockSpec(shape, lambda i: (0,) * nd, pipeline_mode=pl.Buffered(1))


@jax.jit
def kernel(x, mem, positions, norm_g, w_in, attn_sinks, w_o_attn, b_glu, w_dw, b_dw, ln_g, ln_b,
           w_pw, b_pw, mem_norm_g, w_mem_kv, w_o_cross, w_out, final_norm_g):
    batch, seq, d_model = x.shape
    assert d_model == D_MODEL and norm_g.shape[0] == 1 and w_in.shape == (1, D_MODEL, IN_W)
    assert seq % SEQ_TILE == 0 and SEQ_TILE % BLOCK == 0 and mem.shape[1] == MEM_LEN
    ts = SEQ_TILE

    hbm = pl.BlockSpec(memory_space=pl.ANY)
    in_specs = [
        pl.BlockSpec(memory_space=pltpu.SMEM),
        hbm,
        hbm,
        hbm,
        _const_spec((1, D_MODEL)),
        _const_spec((1, D_MODEL)),
        hbm, hbm, hbm, hbm, hbm, hbm,
        _const_spec((1, 2 * CONV_W)),
        pl.BlockSpec(memory_space=pltpu.VMEM),
        _const_spec((1, CONV_W)),
        _const_spec((1, CONV_W)),
        _const_spec((1, CONV_W)),
        _const_spec((1, D_MODEL)),
        _const_spec((1, D_MODEL)),
    ]
    return pl.pallas_call(
        _trunk_kernel,
        grid=(1,),
        in_specs=in_specs,
        out_specs=hbm,
        out_shape=jax.ShapeDtypeStruct((batch, seq, D_MODEL), x.dtype),
        scratch_shapes=[
            pltpu.VMEM((4, ts + BLOCK, KV_W), _BF16),
            pltpu.VMEM((4, ts + BLOCK, KV_W), _BF16),
            pltpu.VMEM((ts, ATTN_W), _F32),
            pltpu.VMEM((CONV_W // LANES, ts + CONV_CARRY, LANES), _F32),
            pltpu.VMEM((D_MODEL, IN_W), _BF16),
            pltpu.VMEM((D_MODEL, D_MODEL), _BF16),
            pltpu.VMEM((ATTN_W, D_MODEL), _BF16),
            pltpu.VMEM((CONV_W, D_MODEL), _BF16),
            pltpu.VMEM((CROSS_W, D_MODEL), _BF16),
            pltpu.VMEM((D_MODEL, 2 * CROSS_W), _BF16),
            pltpu.VMEM((MEM_LEN, CROSS_W), _BF16),
            pltpu.VMEM((MEM_LEN, CROSS_W), _BF16),
            pltpu.VMEM((2, ts, D_MODEL), _F32),
            pltpu.VMEM((2, ts, D_MODEL), _F32),
            pltpu.VMEM((1, MEM_LEN, D_MODEL), _F32),
            pltpu.VMEM((2, 1, ts), jnp.int32),
            pltpu.VMEM((1, BLOCK), jnp.int32),
            pltpu.SemaphoreType.DMA((2,)),
            pltpu.SemaphoreType.DMA((2,)),
            pltpu.SemaphoreType.DMA((1,)),
            pltpu.SemaphoreType.DMA((2,)),
        ],
        compiler_params=pltpu.CompilerParams(
            dimension_semantics=("arbitrary",),
            vmem_limit_bytes=VMEM_LIMIT_BYTES),
        name="trunk",
    )(attn_sinks, x, positions, mem, mem_norm_g, norm_g,
      w_in[0], w_out[0], w_o_attn[0], w_pw[0], w_o_cross[0], w_mem_kv[0], b_glu,
      w_dw, b_dw, ln_g, ln_b, b_pw, final_norm_g.reshape(1, D_MODEL))
```

```python
import math

import jax
import jax.numpy as jnp
from jax import lax
from jax.experimental import pallas as pl
from jax.experimental.pallas import tpu as pltpu

D_MODEL = 1024
MEM_LEN = 256
HEAD_DIM = 64
N_Q_HEADS = 8
N_KV_HEADS = 2
WINDOW = 128
BLOCK = 128
ATTN_W = N_Q_HEADS * HEAD_DIM
KV_W = N_KV_HEADS * HEAD_DIM
CONV_W = D_MODEL // 2
CONV_K = 31
N_CROSS_HEADS = 4
CROSS_HEAD_DIM = 128
CROSS_W = N_CROSS_HEADS * CROSS_HEAD_DIM
RMS_EPS = 1e-6
LN_EPS = 1e-5
NEG_BIG = -1e30
ALIBI_SLOPES = tuple(2.0 ** (-8.0 * (i + 1) / N_Q_HEADS) for i in range(N_Q_HEADS))

_Q0 = 0
_K0 = _Q0 + ATTN_W
_V0 = _K0 + KV_W
_GA0 = _V0 + KV_W
_UC0 = _GA0 + ATTN_W
_GC0 = _UC0 + 2 * CONV_W
_QX0 = _GC0 + CONV_W
_GX0 = _QX0 + CROSS_W
_MG0 = _GX0 + CROSS_W
IN_W = _MG0 + 3 * D_MODEL

LANES = 128
VREG_ROW_BYTES = 32
MXU_COLS = 256
SEQ_TILE = 512
CONV_CARRY = 32
CONV_ROWS = 128
OUT_CHUNKS = 4
LOAD_SLOTS = 12
LOAD_ROWS = 32
LOAD_ROWS_SMALL = 128
VMEM_LIMIT_BYTES = 56 * 1024 * 1024

_BF16 = jnp.bfloat16
_F32 = jnp.float32


def _dot(a, b):
    return jnp.dot(a, b, preferred_element_type=_F32)


def _dot_nt(a, b):
    return lax.dot_general(a, b, (((1,), (1,)), ((), ())), preferred_element_type=_F32)


def _rmsnorm(xf, g):
    y = xf * lax.rsqrt(jnp.mean(xf * xf, axis=-1, keepdims=True) + RMS_EPS)
    return y * g


def _sigmoid(x):
    return jax.nn.sigmoid(x)


def _silu(x):
    return x * jax.nn.sigmoid(x)


def _after(value, anchor):
    rows = VREG_ROW_BYTES // value.dtype.itemsize
    never = pl.program_id(0) < 0
    head = jnp.where(never, anchor[:rows, :LANES].astype(value.dtype), value[:rows, :LANES])
    top = jnp.concatenate([head, value[:rows, LANES:]], axis=1)
    return jnp.concatenate([top, value[rows:]], axis=0)


def _load_as_bf16(src_hbm, dst_ref, rows):
    n_chunks = src_hbm.shape[0] // rows

    def load(stage_ref, sem_ref):
        def chunk_copy(i, slot):
            return pltpu.make_async_copy(src_hbm.at[pl.ds(i * rows, rows)], stage_ref.at[slot],
                                         sem_ref.at[slot])

        for i in range(min(LOAD_SLOTS, n_chunks)):
            chunk_copy(i, i).start()

        def body(i, carry):
            slot = i % LOAD_SLOTS
            chunk_copy(i, slot).wait()
            dst_ref[pl.ds(pl.multiple_of(i * rows, rows), rows), :] = stage_ref[slot].astype(_BF16)

            @pl.when(i + LOAD_SLOTS < n_chunks)
            def _():
                chunk_copy(i + LOAD_SLOTS, slot).start()

            return carry

        lax.fori_loop(0, n_chunks, body, 0)

    pl.run_scoped(load, pltpu.VMEM((LOAD_SLOTS, rows, src_hbm.shape[1]), _F32),
                  pltpu.SemaphoreType.DMA((LOAD_SLOTS,)))


def _load_many_as_bf16(pairs, rows):
    chunks = [(src, dst, r0) for src, dst in pairs for r0 in range(0, src.shape[0], rows)]
    width = pairs[0][0].shape[1]

    def load(stage_ref, sem_ref):
        def chunk_copy(k):
            src, _, r0 = chunks[k]
            slot = k % LOAD_SLOTS
            return pltpu.make_async_copy(src.at[pl.ds(r0, rows)], stage_ref.at[slot], sem_ref.at[slot])

        for k in range(min(LOAD_SLOTS, len(chunks))):
            chunk_copy(k).start()
        for k, (_, dst, r0) in enumerate(chunks):
            chunk_copy(k).wait()
            dst[pl.ds(r0, rows), :] = stage_ref[k % LOAD_SLOTS].astype(_BF16)
            if k + LOAD_SLOTS < len(chunks):
                chunk_copy(k + LOAD_SLOTS).start()

    pl.run_scoped(load, pltpu.VMEM((LOAD_SLOTS, rows, width), _F32),
                  pltpu.SemaphoreType.DMA((LOAD_SLOTS,)))


def _start_row(pos_ref, mem_ref, mem_g_ref, w_mkv_ref, kmask_ref, vmask_ref, kpos_ref, conv_ref,
               km_ref, vm_ref):
    kmask_ref[:, 0:BLOCK, :] = jnp.zeros((4, BLOCK, KV_W), _BF16)
    vmask_ref[:, 0:BLOCK, :] = jnp.zeros((4, BLOCK, KV_W), _BF16)
    conv_ref[:, 0:CONV_CARRY, :] = jnp.zeros((CONV_W // LANES, CONV_CARRY, LANES), _F32)
    kpos_ref[...] = jnp.broadcast_to(pos_ref[:, 0:1] - (WINDOW + 1), (1, BLOCK))
    mem_n = _rmsnorm(mem_ref[0], mem_g_ref[...]).astype(_BF16)
    mem_kv = _dot(mem_n, w_mkv_ref[...])
    km_ref[...] = mem_kv[:, :CROSS_W].astype(_BF16)
    vm_ref[...] = mem_kv[:, CROSS_W:].astype(_BF16)


def _tile(sinks_ref, x_ref, pos_ref, norm_g_ref, b_glu_ref, w_dw_ref, b_dw_ref, ln_g_ref, ln_b_ref,
          b_pw_ref, fng_ref, out_ref, kmask_ref, vmask_ref, kpos_ref, attn_ref, conv_ref,
          w_in_ref, w_out_ref, w_oa_ref, w_pw_ref, w_oc_ref, km_ref, vm_ref):
    ts = x_ref.shape[1]
    n_blk = ts // BLOCK
    n_col = D_MODEL // MXU_COLS

    x = x_ref[0]
    h = _rmsnorm(x, norm_g_ref[...]).astype(_BF16)

    def proj(c0, width=MXU_COLS):
        return _dot(h, w_in_ref[:, c0:c0 + width])

    def cols(c):
        return slice(c * MXU_COLS, (c + 1) * MXU_COLS)

    qkvg = proj(_Q0, _UC0 - _Q0)
    q = (qkvg[:, _Q0:_K0] * (1.0 / math.sqrt(HEAD_DIM))).astype(_BF16)
    k = qkvg[:, _K0:_V0]
    v = qkvg[:, _V0:_GA0]
    g_attn = qkvg[:, _GA0:_UC0]

    lo_half = lax.broadcasted_iota(jnp.int32, (ts, KV_W), 1) < HEAD_DIM
    zero = jnp.zeros((ts, KV_W), _F32)
    for src, dst_ref in ((k, kmask_ref), (v, vmask_ref)):
        rolled = pltpu.roll(src, HEAD_DIM, 1)
        dst_ref[0, BLOCK:, :] = jnp.where(lo_half, src, zero).astype(_BF16)
        dst_ref[1, BLOCK:, :] = jnp.where(lo_half, zero, rolled).astype(_BF16)
        dst_ref[2, BLOCK:, :] = jnp.where(lo_half, rolled, zero).astype(_BF16)
        dst_ref[3, BLOCK:, :] = jnp.where(lo_half, zero, src).astype(_BF16)

    lo_out = lax.broadcasted_iota(jnp.int32, (2 * BLOCK, 2 * HEAD_DIM), 1) < HEAD_DIM
    band_cache = {}

    def pos_blk(j):
        return pos_ref[:, j * BLOCK:(j + 1) * BLOCK]

    def band_terms(j):
        if j not in band_cache:
            qcol = jnp.broadcast_to(pos_blk(j), (BLOCK, BLOCK)).T
            prev_pos = kpos_ref[...] if j == 0 else pos_blk(j - 1)
            delta = jnp.concatenate([qcol - prev_pos, qcol - pos_blk(j)], axis=1)
            band_cache[j] = ((delta >= 0) & (delta < WINDOW), delta.astype(_F32))
        return band_cache[j]

    def scores(j, g):
        rows = slice(j * BLOCK, (j + 1) * BLOCK)
        band = slice(j * BLOCK, (j + 2) * BLOCK)
        qg = jnp.concatenate([q[rows, (2 * g) * LANES:(2 * g + 1) * LANES],
                              q[rows, (2 * g + 1) * LANES:(2 * g + 2) * LANES]], axis=0)
        return [_dot_nt(qg, kmask_ref[2 * g + a, band, :]) for a in range(2)]

    def attend(j, g, s_pair, anchors):
        rows = slice(j * BLOCK, (j + 1) * BLOCK)
        band = slice(j * BLOCK, (j + 2) * BLOCK)
        allowed, delta_f = band_terms(j)
        probs = []
        inv_l = []
        for a in range(2):
            s = s_pair[a]
            e_parts = []
            l_parts = []
            for r in range(2):
                head = 2 * (2 * g + r) + a
                sr = s[r * BLOCK:(r + 1) * BLOCK]
                sr = jnp.where(allowed, sr - ALIBI_SLOPES[head] * delta_f, NEG_BIG)
                sink = sinks_ref[0, head]
                m = jnp.maximum(jnp.max(sr, axis=1, keepdims=True), sink)
                e = jnp.exp(sr - m)
                l_parts.append(jnp.sum(e, axis=1, keepdims=True) + jnp.exp(sink - m))
                e_parts.append(e.astype(_BF16))
            p_a = jnp.concatenate(e_parts, axis=0)
            if a < len(anchors):
                p_a = _after(p_a, anchors[a])
            probs.append(p_a)
            inv_l.append(1.0 / jnp.concatenate(l_parts, axis=0))
        o = (_dot(probs[0], vmask_ref[2 * g, band, :])
             + _dot(probs[1], vmask_ref[2 * g + 1, band, :]))
        o = o * jnp.where(lo_out, inv_l[0], inv_l[1])
        attn_ref[rows, (2 * g) * LANES:(2 * g + 1) * LANES] = o[:BLOCK]
        attn_ref[rows, (2 * g + 1) * LANES:(2 * g + 2) * LANES] = o[BLOCK:]

    n_uc = (_QX0 - _UC0) // MXU_COLS
    attn_todo = [(j, g) for j in range(n_blk) for g in range(N_KV_HEADS)]
    s_all = [scores(j, g) for j, g in attn_todo]
    uc = [_dot(_after(h, s_all[i % len(attn_todo)][0]), w_in_ref[:, _UC0 + i * MXU_COLS:_UC0 + (i + 1) * MXU_COLS])
          for i in range(n_uc)]
    for i, (j, g) in enumerate(attn_todo):
        attend(j, g, s_all[i], uc[i::len(attn_todo)])

    kmask_ref[:, 0:BLOCK, :] = kmask_ref[:, ts:ts + BLOCK, :]
    vmask_ref[:, 0:BLOCK, :] = vmask_ref[:, ts:ts + BLOCK, :]
    kpos_ref[...] = pos_blk(n_blk - 1)

    n_glu = CONV_W // MXU_COLS
    for c in range(n_glu):
        a_c = ((uc[c] + b_glu_ref[:, cols(c)])
               * _sigmoid(uc[n_glu + c] + b_glu_ref[:, CONV_W + c * MXU_COLS:CONV_W + (c + 1) * MXU_COLS]))
        for i in range(MXU_COLS // LANES):
            conv_ref[c * (MXU_COLS // LANES) + i, CONV_CARRY:, :] = a_c[:, i * LANES:(i + 1) * LANES]
    g_conv = jnp.concatenate(uc[2 * n_glu:], axis=1)

    first_tap_row = CONV_CARRY - (CONV_K - 1)
    conv_out = {}

    def conv_piece(lc, r0):
        lanes = slice(lc * LANES, (lc + 1) * LANES)
        acc = jnp.broadcast_to(b_dw_ref[:, lanes], (CONV_ROWS, LANES))
        for t in range(CONV_K):
            off = r0 + first_tap_row + t
            acc = acc + w_dw_ref[0, t, :, lanes] * conv_ref[lc, off:off + CONV_ROWS, :]
        conv_out[(lc, r0)] = acc

    conv_act = {}

    def conv_norm(r0):
        c = jnp.concatenate([conv_out.pop((lc, r0)) for lc in range(CONV_W // LANES)], axis=1)
        mu = jnp.mean(c, axis=-1, keepdims=True)
        cc = c - mu
        var = jnp.mean(cc * cc, axis=-1, keepdims=True)
        cn = cc * lax.rsqrt(var + LN_EPS) * ln_g_ref[...] + ln_b_ref[...]
        conv_act[r0] = (_silu(cn) * _silu(g_conv[r0:r0 + CONV_ROWS])).astype(_BF16)

    cross_out = {}
    qx = {}

    def cross_head(hh):
        per_chunk = MXU_COLS // CROSS_HEAD_DIM
        qh = qx[hh // per_chunk][:, (hh % per_chunk) * CROSS_HEAD_DIM:(hh % per_chunk + 1) * CROSS_HEAD_DIM]
        kv_cols = slice(hh * CROSS_HEAD_DIM, (hh + 1) * CROSS_HEAD_DIM)
        s = _dot_nt(qh.astype(_BF16), km_ref[:, kv_cols]) * (1.0 / math.sqrt(CROSS_HEAD_DIM))
        m = jnp.max(s, axis=1, keepdims=True)
        e = jnp.exp(s - m)
        l = jnp.sum(e, axis=1, keepdims=True)
        cross_out[hh] = _dot(e.astype(_BF16), vm_ref[:, kv_cols]) * (1.0 / l)

    vec_todo = []
    for r0 in range(0, ts, CONV_ROWS):
        vec_todo += [(conv_piece, (lc, r0)) for lc in range(CONV_W // LANES)]
        vec_todo.append((conv_norm, (r0,)))
    n_qx = (_MG0 - _QX0) // MXU_COLS
    gates = []
    for i in range(n_qx + 3 * n_col):
        if i < n_qx:
            qx[i] = proj(_QX0 + i * MXU_COLS)
            if i == CROSS_W // MXU_COLS - 1:
                vec_todo[2:2] = [(cross_head, (hh,)) for hh in range(N_CROSS_HEADS)]
        else:
            gates.append(_sigmoid(proj(_MG0 + (i - n_qx) * MXU_COLS)))
        if vec_todo:
            fn, args = vec_todo.pop(0)
            fn(*args)
    while vec_todo:
        fn, args = vec_todo.pop(0)
        fn(*args)
    conv_ref[:, 0:CONV_CARRY, :] = conv_ref[:, ts:ts + CONV_CARRY, :]

    g_cross = jnp.concatenate([qx[i] for i in range(CROSS_W // MXU_COLS, n_qx)], axis=1)
    attn_act = (attn_ref[...] * _silu(g_attn)).astype(_BF16)
    conv_all = jnp.concatenate([conv_act[r0] for r0 in range(0, ts, CONV_ROWS)], axis=0)
    cross_act = (jnp.concatenate([cross_out[hh] for hh in range(N_CROSS_HEADS)], axis=1)
                 * _silu(g_cross)).astype(_BF16)
    merged = []
    wide = OUT_CHUNKS * MXU_COLS
    for c in range(D_MODEL // wide):
        wcols = slice(c * wide, (c + 1) * wide)
        gate = [jnp.concatenate(gates[br * n_col + OUT_CHUNKS * c:br * n_col + OUT_CHUNKS * (c + 1)], axis=1)
                for br in range(3)]
        y_attn = _dot(attn_act, w_oa_ref[:, wcols])
        y_conv = _dot(conv_all, w_pw_ref[:, wcols]) + b_pw_ref[:, wcols]
        y_cross = _dot(cross_act, w_oc_ref[:, wcols])
        merged.append((gate[0] * y_attn + gate[1] * y_conv + gate[2] * y_cross).astype(_BF16))
    merged = jnp.concatenate(merged, axis=1)

    x_new = jnp.concatenate([x[:, c * wide:(c + 1) * wide] + _dot(merged, w_out_ref[:, c * wide:(c + 1) * wide])
                             for c in range(D_MODEL // wide)], axis=1)
    out_ref[0] = _rmsnorm(x_new, fng_ref[...])


def _trunk_kernel(sinks_ref, x_hbm, pos_hbm, mem_hbm, mem_g_ref, norm_g_ref, w_in_hbm,
                  w_out_hbm, w_oa_hbm, w_pw_hbm, w_oc_hbm, w_mkv_hbm, b_glu_ref, w_dw_ref, b_dw_ref,
                  ln_g_ref, ln_b_ref, b_pw_ref, fng_ref, out_hbm,
                  kmask_ref, vmask_ref, kpos_ref, attn_ref, conv_ref,
                  w_in_ref, w_out_ref, w_oa_ref, w_pw_ref, w_oc_ref, w_mkv_ref, km_ref, vm_ref,
                  x_buf, out_buf, mem_buf, pos_buf, x_sem, out_sem, mem_sem, pos_sem):
    batch, seq, _ = x_hbm.shape
    ts = x_buf.shape[1]
    n_seq = seq // ts
    n_tiles = batch * n_seq

    def tile_rows(t):
        if isinstance(t, int):
            return t // n_seq, pl.ds((t % n_seq) * ts, ts)
        return t // n_seq, pl.ds(pl.multiple_of((t % n_seq) * ts, ts), ts)

    def x_copy(t):
        b, rows = tile_rows(t)
        return pltpu.make_async_copy(x_hbm.at[b, rows], x_buf.at[t % 2], x_sem.at[t % 2])

    def pos_copy(t):
        b, rows = tile_rows(t)
        return pltpu.make_async_copy(pos_hbm.at[pl.ds(b, 1), rows], pos_buf.at[t % 2], pos_sem.at[t % 2])

    def out_copy(t):
        b, rows = tile_rows(t)
        return pltpu.make_async_copy(out_buf.at[t % 2], out_hbm.at[b, rows], out_sem.at[t % 2])

    def mem_copy(b):
        return pltpu.make_async_copy(mem_hbm.at[b], mem_buf.at[0], mem_sem.at[0])

    x_copy(0).start()
    pos_copy(0).start()
    mem_copy(0).start()
    _load_as_bf16(w_in_hbm, w_in_ref, LOAD_ROWS)
    _load_many_as_bf16([(w_mkv_hbm, w_mkv_ref), (w_out_hbm, w_out_ref), (w_oa_hbm, w_oa_ref),
                        (w_pw_hbm, w_pw_ref), (w_oc_hbm, w_oc_ref)], LOAD_ROWS_SMALL)
    pos_copy(0).wait()

    def batch_row(b, carry):
        mem_copy(b).wait()
        _start_row(pos_buf.at[(b * n_seq) % 2], mem_buf, mem_g_ref, w_mkv_ref,
                   kmask_ref, vmask_ref, kpos_ref, conv_ref, km_ref, vm_ref)

        @pl.when(b + 1 < batch)
        def _():
            mem_copy(b + 1).start()

        def tile(s, carry):
            t = b * n_seq + s

            @pl.when(t + 1 < n_tiles)
            def _():
                x_copy(t + 1).start()
                pos_copy(t + 1).start()

            x_copy(t).wait()

            @pl.when(t >= 2)
            def _():
                out_copy(t - 2).wait()

            slot = t % 2
            _tile(sinks_ref, x_buf.at[pl.ds(slot, 1)],
                  pos_buf.at[slot],
                  norm_g_ref, b_glu_ref, w_dw_ref, b_dw_ref, ln_g_ref, ln_b_ref, b_pw_ref, fng_ref,
                  out_buf.at[pl.ds(slot, 1)], kmask_ref, vmask_ref, kpos_ref, attn_ref, conv_ref,
                  w_in_ref, w_out_ref, w_oa_ref, w_pw_ref, w_oc_ref, km_ref, vm_ref)
            out_copy(t).start(priority=1)

            @pl.when(t + 1 < n_tiles)
            def _():
                pos_copy(t + 1).wait()

            return carry

        return lax.fori_loop(0, n_seq, tile, carry)

    lax.fori_loop(0, batch, batch_row, 0)
    out_copy(n_tiles - 2).wait()
    out_copy(n_tiles - 1).wait()


def _const_spec(shape):
    nd = len(shape)
    return pl.BlockSpec(shape, lambda i: (0,) * nd, pipeline_mode=pl.Buffered(1))


@jax.jit
def kernel(x, mem, positions, norm_g, w_in, attn_sinks, w_o_attn, b_glu, w_dw, b_dw, ln_g, ln_b,
           w_pw, b_pw, mem_norm_g, w_mem_kv, w_o_cross, w_out, final_norm_g):
    batch, seq, d_model = x.shape
    assert d_model == D_MODEL and norm_g.shape[0] == 1 and w_in.shape == (1, D_MODEL, IN_W)
    assert seq % SEQ_TILE == 0 and SEQ_TILE % BLOCK == 0 and mem.shape[1] == MEM_LEN
    ts = SEQ_TILE

    hbm = pl.BlockSpec(memory_space=pl.ANY)
    in_specs = [
        pl.BlockSpec(memory_space=pltpu.SMEM),
        hbm,
        hbm,
        hbm,
        _const_spec((1, D_MODEL)),
        _const_spec((1, D_MODEL)),
        hbm, hbm, hbm, hbm, hbm, hbm,
        _const_spec((1, 2 * CONV_W)),
        pl.BlockSpec(memory_space=pltpu.VMEM),
        _const_spec((1, CONV_W)),
        _const_spec((1, CONV_W)),
        _const_spec((1, CONV_W)),
        _const_spec((1, D_MODEL)),
        _const_spec((1, D_MODEL)),
    ]
    return pl.pallas_call(
        _trunk_kernel,
        grid=(1,),
        in_specs=in_specs,
        out_specs=hbm,
        out_shape=jax.ShapeDtypeStruct((batch, seq, D_MODEL), x.dtype),
        scratch_shapes=[
            pltpu.VMEM((4, ts + BLOCK, KV_W), _BF16),
            pltpu.VMEM((4, ts + BLOCK, KV_W), _BF16),
            pltpu.VMEM((1, BLOCK), jnp.int32),
            pltpu.VMEM((ts, ATTN_W), _F32),
            pltpu.VMEM((CONV_W // LANES, ts + CONV_CARRY, LANES), _F32),
            pltpu.VMEM((D_MODEL, IN_W), _BF16),
            pltpu.VMEM((D_MODEL, D_MODEL), _BF16),
            pltpu.VMEM((ATTN_W, D_MODEL), _BF16),
            pltpu.VMEM((CONV_W, D_MODEL), _BF16),
            pltpu.VMEM((CROSS_W, D_MODEL), _BF16),
            pltpu.VMEM((D_MODEL, 2 * CROSS_W), _BF16),
            pltpu.VMEM((MEM_LEN, CROSS_W), _BF16),
            pltpu.VMEM((MEM_LEN, CROSS_W), _BF16),
            pltpu.VMEM((2, ts, D_MODEL), _F32),
            pltpu.VMEM((2, ts, D_MODEL), _F32),
            pltpu.VMEM((1, MEM_LEN, D_MODEL), _F32),
            pltpu.VMEM((2, 1, ts), jnp.int32),
            pltpu.SemaphoreType.DMA((2,)),
            pltpu.SemaphoreType.DMA((2,)),
            pltpu.SemaphoreType.DMA((1,)),
            pltpu.SemaphoreType.DMA((2,)),
        ],
        compiler_params=pltpu.CompilerParams(
            dimension_semantics=("arbitrary",),
            vmem_limit_bytes=VMEM_LIMIT_BYTES),
        name="trunk",
    )(attn_sinks, x, positions, mem, mem_norm_g, norm_g,
      w_in[0], w_out[0], w_o_attn[0], w_pw[0], w_o_cross[0], w_mem_kv[0], b_glu,
      w_dw, b_dw, ln_g, ln_b, b_pw, final_norm_g.reshape(1, D_MODEL))
```
